```python
import math
import jax
import jax.numpy as jnp
from jax import lax
import numpy as np

D_MODEL = 2048
BATCH = 32
SEQ = 256
DEPTH = 2
DEC_BATCH = 2
DEC_SEQ = 1024
PAST_LEN = 512

GRID_W = 64
RMS_EPS = 1e-6
N_MOD = 9
D_FF = 5632
GLA_HEADS = 4
GLA_DK = 128
GLA_DV = 256
GLA_RANK = 16
GLA_GATE_NORM = 16.0
GLA_CHUNK = 16
SSM_HEADS = 16
SSM_HEADDIM = 64
SSM_GROUPS = 4
SSM_HPG = SSM_HEADS // SSM_GROUPS
SSM_STATE = 128
SSM_CONV = 5
SSM_CHUNK = 64
SSM_INNER = SSM_HEADS * SSM_HEADDIM
SSM_BC = SSM_GROUPS * SSM_STATE
CONV_CH = SSM_INNER + 2 * SSM_BC
ATTN_HEADS = 8
KV_HEADS = 2
Q_PER_KV = ATTN_HEADS // KV_HEADS
HEAD_DIM = 128
WINDOW = 128
ATTN_BLOCK = 128
ROPE_THETA = 10000.0
GLA_QK = GLA_HEADS * GLA_DK
GLA_VAL = GLA_HEADS * GLA_DV
ATTN_Q = ATTN_HEADS * HEAD_DIM
ATTN_KV = KV_HEADS * HEAD_DIM
N_BRANCH = 3
IN_SPLITS = (GLA_QK, GLA_QK, GLA_VAL, GLA_VAL, 2 * GLA_RANK,
             SSM_INNER, CONV_CH, 2 * SSM_HEADS,
             ATTN_Q, ATTN_KV, ATTN_KV,
             N_BRANCH * D_MODEL)
D_IN = sum(IN_SPLITS)

kernel_name = 'hybrid_diffusion_trunk_step'


def _rmsnorm(x, w):
    xf = x.astype(jnp.float32)
    y = xf * lax.rsqrt(jnp.mean(xf * xf, axis=-1, keepdims=True) + RMS_EPS)
    return (y * w.astype(jnp.float32)).astype(x.dtype)


def _modulate(h, shift, scale):
    return h * (1.0 + scale[:, None]) + shift[:, None]


def _swiglu(h, w_gate, w_up, w_down):
    return (jax.nn.silu(h @ w_gate) * (h @ w_up)) @ w_down


def _flip(t):
    return jnp.flip(t, axis=1)


def _rope_1d(x, pos):
    half = x.shape[-1] // 2
    freqs = ROPE_THETA ** (-jnp.arange(half, dtype=jnp.float32) / half)
    ang = pos.astype(jnp.float32)[:, None] * freqs[None, :]
    cos = jnp.cos(ang)[:, None, :]
    sin = jnp.sin(ang)[:, None, :]
    x1 = x[..., :half].astype(jnp.float32)
    x2 = x[..., half:].astype(jnp.float32)
    return jnp.concatenate([x1 * cos - x2 * sin, x2 * cos + x1 * sin], axis=-1).astype(x.dtype)


def _rope_2d(x):
    t = jnp.arange(x.shape[1])
    half = x.shape[-1] // 2
    return jnp.concatenate([_rope_1d(x[..., :half], t // GRID_W), _rope_1d(x[..., half:], t % GRID_W)], axis=-1)


def _depthwise_conv(x, w, b):
    y = lax.conv_general_dilated(x, w[:, None, :], window_strides=(1,),
                                 padding=[(SSM_CONV // 2, SSM_CONV // 2)],
                                 dimension_numbers=('NWC', 'WIO', 'NWC'),
                                 feature_group_count=x.shape[-1])
    return y + b


def _gla_chunked(q, k, v, log_a, h0):
    B, L, H, K = q.shape
    V = v.shape[-1]
    n = L // GLA_CHUNK
    f32 = jnp.float32
    qc = q.astype(f32).reshape(B, n, GLA_CHUNK, H, K)
    kc = k.astype(f32).reshape(B, n, GLA_CHUNK, H, K)
    vc = v.astype(f32).reshape(B, n, GLA_CHUNK, H, V)
    cum = jnp.cumsum(log_a.astype(f32).reshape(B, n, GLA_CHUNK, H, K), axis=2)
    causal = jnp.tril(jnp.ones((GLA_CHUNK, GLA_CHUNK), bool))[:, :, None, None]
    decay = jnp.exp(jnp.where(causal, cum[:, :, :, None] - cum[:, :, None, :], -jnp.inf))
    scores = jnp.einsum('bcthk,bcshk,bctshk->bchts', qc, kc, decay)
    o_intra = jnp.einsum('bchts,bcshv->bcthv', scores, vc)
    cum_end = cum[:, :, -1]
    s_chunk = jnp.einsum('bcshk,bcshv->bchkv', kc * jnp.exp(cum_end[:, :, None] - cum), vc)

    def step(s, inp):
        sc, ae = inp
        return jnp.exp(ae)[..., None] * s + sc, s

    s_final, s_prev = lax.scan(step, h0.astype(f32), (jnp.moveaxis(s_chunk, 1, 0), jnp.moveaxis(cum_end, 1, 0)))
    s_prev = jnp.moveaxis(s_prev, 0, 1)
    o_inter = jnp.einsum('bcthk,bchkv->bcthv', qc * jnp.exp(cum), s_prev)
    return (o_intra + o_inter).reshape(B, L, H, V).astype(v.dtype), s_final


def _ssd_chunked(x, dt, A, bm, cm, h0):
    B, L, G, Hg, P = x.shape
    n = L // SSM_CHUNK
    f32 = jnp.float32
    xc = x.astype(f32).reshape(B, n, SSM_CHUNK, G, Hg, P)
    dtc = dt.astype(f32).reshape(B, n, SSM_CHUNK, G, Hg)
    bc = bm.astype(f32).reshape(B, n, SSM_CHUNK, G, SSM_STATE)
    cc = cm.astype(f32).reshape(B, n, SSM_CHUNK, G, SSM_STATE)
    cum = jnp.cumsum(dtc * A.astype(f32), axis=2)
    cum_t = jnp.moveaxis(cum, 2, -1)
    causal = jnp.tril(jnp.ones((SSM_CHUNK, SSM_CHUNK), bool))
    seg = jnp.exp(jnp.where(causal, cum_t[..., :, None] - cum_t[..., None, :], -jnp.inf))
    cb = jnp.einsum('bctgn,bcsgn->bcgts', cc, bc)
    w = cb[:, :, :, None] * seg * jnp.moveaxis(dtc, 2, -1)[..., None, :]
    y_intra = jnp.einsum('bcghts,bcsghp->bctghp', w, xc)
    cum_end = cum[:, :, -1]
    s_chunk = jnp.einsum('bcsgn,bcsgh,bcsghp->bcghpn', bc, dtc * jnp.exp(cum_end[:, :, None] - cum), xc)

    def step(s, inp):
        sc, ae = inp
        return jnp.exp(ae)[..., None, None] * s + sc, s

    s_final, s_prev = lax.scan(step, h0.astype(f32), (jnp.moveaxis(s_chunk, 1, 0), jnp.moveaxis(cum_end, 1, 0)))
    s_prev = jnp.moveaxis(s_prev, 0, 1)
    y_inter = jnp.einsum('bctgn,bctgh,bcghpn->bctghp', cc, jnp.exp(cum), s_prev)
    return (y_intra + y_inter).reshape(B, L, G, Hg, P).astype(x.dtype), s_final


def _sink_attention(q, k, v, mask, sink):
    s = jnp.einsum('bqkgd,bskd->bkgqs', q, k).astype(jnp.float32) * (HEAD_DIM ** -0.5)
    if mask is not None:
        s = jnp.where(mask, s, -jnp.inf)
    sink_col = jnp.broadcast_to(sink.astype(jnp.float32)[None, :, :, None, None], s.shape[:-1] + (1,))
    p = jax.nn.softmax(jnp.concatenate([sink_col, s], axis=-1), axis=-1)[..., 1:]
    return jnp.einsum('bkgqs,bskd->bqkgd', p.astype(v.dtype), v)


def _context_attention(q, k, v, sink):
    B, L = q.shape[:2]
    nb = L // ATTN_BLOCK
    q_blocks = jnp.moveaxis(q.reshape(B, nb, ATTN_BLOCK, KV_HEADS, Q_PER_KV, HEAD_DIM), 1, 0)
    out = lax.map(lambda qi: _sink_attention(qi, k, v, None, sink), q_blocks)
    return jnp.moveaxis(out, 0, 1).reshape(B, L, ATTN_Q)


def _latent_attention(q, k, v, k_ctx, v_ctx, sink):
    B, L = q.shape[:2]
    nb = L // ATTN_BLOCK
    span = ATTN_BLOCK + 2 * WINDOW
    pad = ((0, 0), (WINDOW, WINDOW), (0, 0), (0, 0))
    kp = jnp.pad(k, pad)
    vp = jnp.pad(v, pad)
    q_blocks = jnp.moveaxis(q.reshape(B, nb, ATTN_BLOCK, KV_HEADS, Q_PER_KV, HEAD_DIM), 1, 0)
    ctx_mask = jnp.ones((ATTN_BLOCK, k_ctx.shape[1]), bool)

    def block(args):
        i, qi = args
        start = i * ATTN_BLOCK
        kw = lax.dynamic_slice_in_dim(kp, start, span, axis=1)
        vw = lax.dynamic_slice_in_dim(vp, start, span, axis=1)
        qpos = start + jnp.arange(ATTN_BLOCK)
        kpos = start - WINDOW + jnp.arange(span)
        win = (jnp.abs(qpos[:, None] - kpos[None, :]) <= WINDOW) & (kpos[None, :] >= 0) & (kpos[None, :] < L)
        mask = jnp.concatenate([ctx_mask, win], axis=1)
        return _sink_attention(qi, jnp.concatenate([k_ctx, kw], axis=1),
                               jnp.concatenate([v_ctx, vw], axis=1), mask, sink)

    out = lax.map(block, (jnp.arange(nb), q_blocks))
    return jnp.moveaxis(out, 0, 1).reshape(B, L, ATTN_Q)


def _token_mix(h, lp, ctx):
    B, L, _ = h.shape
    f32 = jnp.float32
    split_at = [int(s) for s in np.cumsum(IN_SPLITS)[:-1]]
    (g_q, g_k, g_v, g_r, g_down, s_z, s_xbc, s_dt, a_q, a_k, a_v, br) = jnp.split(h @ lp['w_in'], split_at, axis=-1)
    if ctx is None:
        gla_h0 = jnp.zeros((2, B, GLA_HEADS, GLA_DK, GLA_DV), f32)
        ssm_h0 = jnp.zeros((2, B, SSM_GROUPS, SSM_HPG, SSM_HEADDIM, SSM_STATE), f32)
    else:
        gla_h0 = jnp.moveaxis(ctx['gla'], 1, 0)
        ssm_h0 = jnp.moveaxis(ctx['ssm'], 1, 0).reshape(2, B, SSM_GROUPS, SSM_HPG, SSM_HEADDIM, SSM_STATE)

    q = g_q.reshape(B, L, GLA_HEADS, GLA_DK) * (GLA_DK ** -0.5)
    k = g_k.reshape(B, L, GLA_HEADS, GLA_DK)
    v = g_v.reshape(B, L, GLA_HEADS, GLA_DV)
    gz = jnp.einsum('bldr,drk->bldk', g_down.reshape(B, L, 2, GLA_RANK), lp['gla_w_up']) + lp['gla_b_up']
    log_a = (jax.nn.log_sigmoid(gz.astype(f32)) / GLA_GATE_NORM).reshape(B, L, 2, GLA_HEADS, GLA_DK)
    o_f, sg_f = _gla_chunked(q, k, v, log_a[:, :, 0], gla_h0[0])
    o_b, sg_b = _gla_chunked(_flip(q), _flip(k), _flip(v), _flip(log_a[:, :, 1]), gla_h0[1])
    o_gla = _rmsnorm(o_f + _flip(o_b), lp['gla_norm']).reshape(B, L, GLA_VAL) * jax.nn.silu(g_r)

    xbc = jax.nn.silu(_depthwise_conv(s_xbc, lp['ssm_conv_w'], lp['ssm_conv_b']))
    xs, bm, cm = jnp.split(xbc, [SSM_INNER, SSM_INNER + SSM_BC], axis=-1)
    xs = xs.reshape(B, L, SSM_GROUPS, SSM_HPG, SSM_HEADDIM)
    bm = bm.reshape(B, L, SSM_GROUPS, SSM_STATE)
    cm = cm.reshape(B, L, SSM_GROUPS, SSM_STATE)
    dt = jax.nn.softplus(s_dt.astype(f32).reshape(B, L, 2, SSM_HEADS) + lp['ssm_dt_bias'].astype(f32))
    dt = dt.reshape(B, L, 2, SSM_GROUPS, SSM_HPG)
    A = -jnp.exp(lp['ssm_a_log'].astype(f32)).reshape(2, SSM_GROUPS, SSM_HPG)
    y_f, ss_f = _ssd_chunked(xs, dt[:, :, 0], A[0], bm, cm, ssm_h0[0])
    y_b, ss_b = _ssd_chunked(_flip(xs), _flip(dt[:, :, 1]), A[1], _flip(bm), _flip(cm), ssm_h0[1])
    y = y_f + _flip(y_b) + lp['ssm_d'].reshape(SSM_GROUPS, SSM_HPG)[:, :, None] * xs
    o_ssm = _rmsnorm(y.reshape(B, L, SSM_INNER) * jax.nn.silu(s_z), lp['ssm_norm'])

    qa = a_q.reshape(B, L, ATTN_HEADS, HEAD_DIM)
    ka = a_k.reshape(B, L, KV_HEADS, HEAD_DIM)
    va = a_v.reshape(B, L, KV_HEADS, HEAD_DIM)
    sink = lp['attn_sink'].reshape(KV_HEADS, Q_PER_KV)
    if ctx is None:
        o_att = _context_attention(qa.reshape(B, L, KV_HEADS, Q_PER_KV, HEAD_DIM), ka, va, sink)
    else:
        qa = _rope_2d(qa)
        ka = _rope_2d(ka)
        o_att = _latent_attention(qa.reshape(B, L, KV_HEADS, Q_PER_KV, HEAD_DIM), ka, va, ctx['k'], ctx['v'], sink)

    gates = jax.nn.sigmoid(br).reshape(B, L, N_BRANCH, D_MODEL)
    m = (gates[:, :, 0] * (o_gla @ lp['w_br_gla'])
         + gates[:, :, 1] * (o_ssm @ lp['w_br_ssm'])
         + gates[:, :, 2] * (o_att @ lp['w_br_attn']))
    out = m @ lp['w_out']
    gla_state = jnp.stack([sg_f, sg_b], axis=1)
    ssm_state = jnp.stack([ss_f, ss_b], axis=1).reshape(B, 2, SSM_HEADS, SSM_HEADDIM, SSM_STATE)
    return out, (ka, va, gla_state, ssm_state)


def _trunk_layer(x, mod, lp, ctx):
    sh1, sc1, g1, sh2, sc2, g2, sh3, sc3, g3 = jnp.split(mod, N_MOD, axis=-1)
    h = _modulate(_rmsnorm(x, lp['ffn1_norm']), sh1, sc1)
    x = x + 0.5 * g1[:, None] * _swiglu(h, lp['ffn1_w_gate'], lp['ffn1_w_up'], lp['ffn1_w_down'])
    h = _modulate(_rmsnorm(x, lp['mix_norm']), sh2, sc2)
    mix, ctx_tensors = _token_mix(h, lp, ctx)
    x = x + g2[:, None] * mix
    h = _modulate(_rmsnorm(x, lp['ffn2_norm']), sh3, sc3)
    x = x + 0.5 * g3[:, None] * _swiglu(h, lp['ffn2_w_gate'], lp['ffn2_w_up'], lp['ffn2_w_down'])
    return x, ctx_tensors


def setup_inputs(seed: int = 0) -> dict:
    key = jax.random.key(seed)
    ks = iter(jax.random.split(key, 48))
    D = D_MODEL

    def nrm(shape, scale=1.0):
        return jax.random.normal(next(ks), shape, jnp.float32) * scale

    def gain(shape):
        return 1.0 + nrm(shape, 0.05)

    dt0 = jnp.exp(jax.random.uniform(next(ks), (DEPTH, 2, SSM_HEADS), jnp.float32, math.log(1e-3), math.log(1e-1)))
    a0 = jax.random.uniform(next(ks), (DEPTH, 2, SSM_HEADS), jnp.float32, 1.0, 16.0)
    return {
        'x_prompt': nrm((BATCH, SEQ, D)),
        'x_sample': nrm((DEC_BATCH, DEC_SEQ, D)),
        'c': nrm((DEC_BATCH, D)),
        'cache_k': nrm((DEC_BATCH, DEPTH, PAST_LEN, KV_HEADS, HEAD_DIM)),
        'cache_v': nrm((DEC_BATCH, DEPTH, PAST_LEN, KV_HEADS, HEAD_DIM)),
        'state_gla': nrm((DEC_BATCH, DEPTH, 2, GLA_HEADS, GLA_DK, GLA_DV)),
        'state_ssm': nrm((DEC_BATCH, DEPTH, 2, SSM_HEADS, SSM_HEADDIM, SSM_STATE)),
        'c_ctx': nrm((D,)),
        'w_mod': nrm((DEPTH, D, N_MOD * D), D ** -0.5),
        'b_mod': nrm((DEPTH, N_MOD * D), 0.01),
        'ffn1_norm': gain((DEPTH, D)),
        'ffn1_w_gate': nrm((DEPTH, D, D_FF), D ** -0.5),
        'ffn1_w_up': nrm((DEPTH, D, D_FF), D ** -0.5),
        'ffn1_w_down': nrm((DEPTH, D_FF, D), D_FF ** -0.5),
        'mix_norm': gain((DEPTH, D)),
        'w_in': nrm((DEPTH, D, D_IN), D ** -0.5),
        'gla_w_up': nrm((DEPTH, 2, GLA_RANK, GLA_QK), GLA_RANK ** -0.5),
        'gla_b_up': nrm((DEPTH, 2, GLA_QK), 0.1),
        'gla_norm': gain((DEPTH, GLA_DV)),
        'ssm_conv_w': nrm((DEPTH, SSM_CONV, CONV_CH), SSM_CONV ** -0.5),
        'ssm_conv_b': nrm((DEPTH, CONV_CH), 0.01),
        'ssm_dt_bias': dt0 + jnp.log(-jnp.expm1(-dt0)),
        'ssm_a_log': jnp.log(a0),
        'ssm_d': 1.0 + nrm((DEPTH, SSM_HEADS), 0.1),
        'ssm_norm': gain((DEPTH, SSM_INNER)),
        'attn_sink': nrm((DEPTH, ATTN_HEADS), 0.5),
        'w_br_gla': nrm((DEPTH, GLA_VAL, D), GLA_VAL ** -0.5),
        'w_br_ssm': nrm((DEPTH, SSM_INNER, D), SSM_INNER ** -0.5),
        'w_br_attn': nrm((DEPTH, ATTN_Q, D), ATTN_Q ** -0.5),
        'w_out': nrm((DEPTH, D, D), D ** -0.5),
        'ffn2_norm': gain((DEPTH, D)),
        'ffn2_w_gate': nrm((DEPTH, D, D_FF), D ** -0.5),
        'ffn2_w_up': nrm((DEPTH, D, D_FF), D ** -0.5),
        'ffn2_w_down': nrm((DEPTH, D_FF, D), D_FF ** -0.5),
        'final_norm': gain((D,)),
    }


def reference(x_prompt, x_sample, c, cache_k, cache_v, state_gla, state_ssm, c_ctx,
              w_mod, b_mod, ffn1_norm, ffn1_w_gate, ffn1_w_up, ffn1_w_down,
              mix_norm, w_in, gla_w_up, gla_b_up, gla_norm,
              ssm_conv_w, ssm_conv_b, ssm_dt_bias, ssm_a_log, ssm_d, ssm_norm,
              attn_sink, w_br_gla, w_br_ssm, w_br_attn, w_out,
              ffn2_norm, ffn2_w_gate, ffn2_w_up, ffn2_w_down, final_norm):
    silu_ctx = jax.nn.silu(c_ctx)[None]
    silu_lat = jax.nn.silu(c)
    xp, xs = x_prompt, x_sample
    new_k, new_v, new_gla, new_ssm = [], [], [], []
    for l in range(DEPTH):
        lp = {
            'ffn1_norm': ffn1_norm[l], 'ffn1_w_gate': ffn1_w_gate[l], 'ffn1_w_up': ffn1_w_up[l],
            'ffn1_w_down': ffn1_w_down[l], 'mix_norm': mix_norm[l], 'w_in': w_in[l],
            'gla_w_up': gla_w_up[l], 'gla_b_up': gla_b_up[l], 'gla_norm': gla_norm[l],
            'ssm_conv_w': ssm_conv_w[l], 'ssm_conv_b': ssm_conv_b[l], 'ssm_dt_bias': ssm_dt_bias[l],
            'ssm_a_log': ssm_a_log[l], 'ssm_d': ssm_d[l], 'ssm_norm': ssm_norm[l],
            'attn_sink': attn_sink[l], 'w_br_gla': w_br_gla[l], 'w_br_ssm': w_br_ssm[l],
            'w_br_attn': w_br_attn[l], 'w_out': w_out[l], 'ffn2_norm': ffn2_norm[l],
            'ffn2_w_gate': ffn2_w_gate[l], 'ffn2_w_up': ffn2_w_up[l], 'ffn2_w_down': ffn2_w_down[l],
        }
        mod_ctx = silu_ctx @ w_mod[l] + b_mod[l]
        mod_lat = silu_lat @ w_mod[l] + b_mod[l]
        xp, (k_c, v_c, g_st, s_st) = _trunk_layer(xp, mod_ctx, lp, None)
        new_k.append(k_c)
        new_v.append(v_c)
        new_gla.append(g_st)
        new_ssm.append(s_st)
        ctx = {'k': cache_k[:, l], 'v': cache_v[:, l], 'gla': state_gla[:, l], 'ssm': state_ssm[:, l]}
        xs, _ = _trunk_layer(xs, mod_lat, lp, ctx)
    y_prompt = _rmsnorm(xp, final_norm)
    y_sample = _rmsnorm(xs, final_norm)
    return (y_prompt, y_sample, jnp.stack(new_k, axis=1), jnp.stack(new_v, axis=1),
            jnp.stack(new_gla, axis=1), jnp.stack(new_ssm, axis=1))
```

```python
import functools
import math

import numpy as np
import jax
import jax.numpy as jnp
from jax import lax
from jax.experimental import pallas as pl
from jax.experimental.pallas import tpu as pltpu

f32 = jnp.float32
bf16 = jnp.bfloat16

D_MODEL = 2048
BATCH = 32
SEQ = 256
DEPTH = 2
DEC_BATCH = 2
DEC_SEQ = 1024
PAST_LEN = 512
GRID_W = 64
RMS_EPS = 1e-6
N_MOD = 9
D_FF = 5632
GLA_HEADS = 4
GLA_DK = 128
GLA_DV = 256
GLA_RANK = 16
GLA_GATE_NORM = 16.0
GLA_CHUNK = 16
SSM_HEADS = 16
SSM_HEADDIM = 64
SSM_GROUPS = 4
SSM_HPG = SSM_HEADS // SSM_GROUPS
SSM_STATE = 128
SSM_CONV = 5
SSM_INNER = SSM_HEADS * SSM_HEADDIM
SSM_BC = SSM_GROUPS * SSM_STATE
CONV_CH = SSM_INNER + 2 * SSM_BC
ATTN_HEADS = 8
KV_HEADS = 2
Q_PER_KV = ATTN_HEADS // KV_HEADS
HEAD_DIM = 128
WINDOW = 128
ROPE_THETA = 10000.0
GLA_QK = GLA_HEADS * GLA_DK
GLA_VAL = GLA_HEADS * GLA_DV
ATTN_Q = ATTN_HEADS * HEAD_DIM
ATTN_KV = KV_HEADS * HEAD_DIM

N_PROMPT = BATCH * SEQ
N_SAMPLE = DEC_BATCH * DEC_SEQ
N_TOK = N_PROMPT + N_SAMPLE
MOD_ROWS = 8

COL_GQ = 0
COL_GK = COL_GQ + GLA_QK
COL_GV = COL_GK + GLA_QK
COL_GR = COL_GV + GLA_VAL
COL_SZ = COL_GR + GLA_VAL
COL_SX = COL_SZ + SSM_INNER
COL_SB = COL_SX + SSM_INNER
COL_SC = COL_SB + SSM_BC
COL_AQ = COL_SC + SSM_BC
COL_AK = COL_AQ + ATTN_Q
COL_AV = COL_AK + ATTN_KV
COL_SMALL = COL_AV + ATTN_KV
SMALL_W = 128
SMALL_DT = 2 * GLA_RANK
COL_BR = 8192
D_INP = COL_BR + 3 * D_MODEL

TM = 512
TF = 512
TN_IN = 1024
TN_MOD = 1024
TJ = 512
GLA_SB = 256
SSD_CHUNK = 128
VMEM_LIMIT = 56 * 1024 * 1024

NT = (((1,), (1,)), ((), ()))
TN = (((0,), (0,)), ((), ()))


def _silu(x):
    return x * jax.nn.sigmoid(x)


def _mod_row(i, tm):
    n_p = N_PROMPT // tm
    per = DEC_SEQ // tm
    return jnp.where(i < n_p, 0, 1 + (i - n_p) // per)


def _cparams(sem):
    return pltpu.CompilerParams(dimension_semantics=sem, vmem_limit_bytes=VMEM_LIMIT)


def _mod_kernel(c_ref, w_ref, b_ref, o_ref):
    s = _silu(c_ref[...]).astype(bf16)
    o_ref[0] = jnp.dot(s, w_ref[0].astype(bf16), preferred_element_type=f32) + b_ref[0]


def _modulation(cvec, w_mod, b_mod):
    n = N_MOD * D_MODEL
    return pl.pallas_call(
        _mod_kernel,
        grid=(DEPTH, n // TN_MOD),
        in_specs=[
            pl.BlockSpec((MOD_ROWS, D_MODEL), lambda l, j: (0, 0)),
            pl.BlockSpec((1, D_MODEL, TN_MOD), lambda l, j: (l, 0, j)),
            pl.BlockSpec((1, 1, TN_MOD), lambda l, j: (l, 0, j)),
        ],
        out_specs=pl.BlockSpec((1, MOD_ROWS, TN_MOD), lambda l, j: (l, 0, j)),
        out_shape=jax.ShapeDtypeStruct((DEPTH, MOD_ROWS, n), f32),
        compiler_params=_cparams(("arbitrary", "arbitrary")),
        name="modulation",
    )(cvec, w_mod, b_mod.reshape(DEPTH, 1, n))


def _norm_mod(x, nw, sh, sc):
    ms = jnp.mean(x * x, axis=-1, keepdims=True)
    h = x * lax.rsqrt(ms + RMS_EPS) * nw
    return h * (1.0 + sc) + sh


def _ffn_kernel(x_ref, sh_ref, sc_ref, gt_ref, nw_ref, wg_ref, wu_ref, wd_ref, *rest, final):
    if final:
        fw_ref, o_ref, h_ref, acc_ref = rest
    else:
        o_ref, h_ref, acc_ref = rest
    j = pl.program_id(1)

    @pl.when(j == 0)
    def _():
        h_ref[...] = _norm_mod(x_ref[...], nw_ref[...], sh_ref[0], sc_ref[0]).astype(bf16)
        acc_ref[...] = jnp.zeros_like(acc_ref)

    h = h_ref[...]
    g = jnp.dot(h, wg_ref[...], preferred_element_type=f32)
    u = jnp.dot(h, wu_ref[...], preferred_element_type=f32)
    a = (_silu(g) * u).astype(bf16)
    acc_ref[...] += jnp.dot(a, wd_ref[...], preferred_element_type=f32)

    @pl.when(j == pl.num_programs(1) - 1)
    def _():
        y = x_ref[...] + 0.5 * gt_ref[0] * acc_ref[...]
        if final:
            ms = jnp.mean(y * y, axis=-1, keepdims=True)
            y = y * lax.rsqrt(ms + RMS_EPS) * fw_ref[...]
        o_ref[...] = y


def _ffn(x, mod3, l, slot, nw, wg, wu, wd, final_w=None):
    n_tok = x.shape[0]
    final = final_w is not None

    def mod_spec(k):
        return pl.BlockSpec((1, 1, D_MODEL), lambda i, j: (_mod_row(i, TM), 0, 3 * slot + k))

    in_specs = [
        pl.BlockSpec((TM, D_MODEL), lambda i, j: (i, 0)),
        mod_spec(0), mod_spec(1), mod_spec(2),
        pl.BlockSpec((None, 1, D_MODEL), lambda i, j: (l, 0, 0)),
        pl.BlockSpec((None, D_MODEL, TF), lambda i, j: (l, 0, j)),
        pl.BlockSpec((None, D_MODEL, TF), lambda i, j: (l, 0, j)),
        pl.BlockSpec((None, TF, D_MODEL), lambda i, j: (l, j, 0)),
    ]
    args = [x, mod3, mod3, mod3, nw, wg, wu, wd]
    if final:
        in_specs.append(pl.BlockSpec((1, D_MODEL), lambda i, j: (0, 0)))
        args.append(final_w.reshape(1, D_MODEL))
    return pl.pallas_call(
        functools.partial(_ffn_kernel, final=final),
        grid=(n_tok // TM, D_FF // TF),
        in_specs=in_specs,
        out_specs=pl.BlockSpec((TM, D_MODEL), lambda i, j: (i, 0)),
        out_shape=jax.ShapeDtypeStruct((n_tok, D_MODEL), f32),
        scratch_shapes=[pltpu.VMEM((TM, D_MODEL), bf16), pltpu.VMEM((TM, D_MODEL), f32)],
        compiler_params=_cparams(("parallel", "arbitrary")),
        name="ffn_final" if final else "ffn",
    )(*args)


def _inproj_kernel(x_ref, sh_ref, sc_ref, nw_ref, w_ref, o_ref, h_ref):
    @pl.when(pl.program_id(1) == 0)
    def _():
        h_ref[...] = _norm_mod(x_ref[...], nw_ref[...], sh_ref[0], sc_ref[0]).astype(bf16)

    o_ref[...] = jnp.dot(h_ref[...], w_ref[...], preferred_element_type=f32)


def _inproj(x, mod3, l, nw, w):
    n_tok = x.shape[0]

    def mod_spec(k):
        return pl.BlockSpec((1, 1, D_MODEL), lambda i, j: (_mod_row(i, TM), 0, 3 + k))

    return pl.pallas_call(
        _inproj_kernel,
        grid=(n_tok // TM, D_INP // TN_IN),
        in_specs=[
            pl.BlockSpec((TM, D_MODEL), lambda i, j: (i, 0)),
            mod_spec(0), mod_spec(1),
            pl.BlockSpec((None, 1, D_MODEL), lambda i, j: (l, 0, 0)),
            pl.BlockSpec((None, D_MODEL, TN_IN), lambda i, j: (l, 0, j)),
        ],
        out_specs=pl.BlockSpec((TM, TN_IN), lambda i, j: (i, j)),
        out_shape=jax.ShapeDtypeStruct((n_tok, D_INP), f32),
        scratch_shapes=[pltpu.VMEM((TM, D_MODEL), bf16)],
        compiler_params=_cparams(("parallel", "arbitrary")),
        name="inproj",
    )(x, mod3, mod3, nw, w)


def _split_dot(m, x):
    hi = x.astype(bf16)
    lo = (x - hi.astype(f32)).astype(bf16)
    return (jnp.dot(m, hi, preferred_element_type=f32) + jnp.dot(m, lo, preferred_element_type=f32))


def _log_sigmoid(x):
    return jnp.minimum(x, 0.0) - jnp.log1p(jnp.exp(-jnp.abs(x)))


def _gla_kernel(q_ref, k_ref, v_ref, r_ref, sm_ref, wz_ref, bz_ref, gn_ref, *rest, seq, has_h0, emit_state):
    rest = list(rest)
    h0_ref = rest.pop(0) if has_h0 else None
    o_ref = rest.pop(0)
    st_ref = rest.pop(0) if emit_state else None
    oin_ref, qe_ref, ke_ref, dec_ref, vb_ref = rest
    n_chunk = seq // GLA_CHUNK
    sb = GLA_SB

    row = lax.broadcasted_iota(jnp.int32, (sb, sb), 0)
    col = lax.broadcasted_iota(jnp.int32, (sb, sb), 1)
    same = (row // GLA_CHUNK) == (col // GLA_CHUNK)
    masks = (same & (col <= row), same & (col >= row))
    tris = tuple(jnp.where(m, 1.0, 0.0).astype(bf16) for m in masks)
    blk = jnp.where(same, 1.0, 0.0).astype(bf16)

    for s in range(seq // sb):
        rows = slice(s * sb, (s + 1) * sb)
        sm = sm_ref[rows, :].astype(bf16)
        q = q_ref[rows, :] * (GLA_DK ** -0.5)
        k = k_ref[rows, :]
        v = v_ref[rows, :].astype(bf16)
        vb_ref[rows, :] = v
        for d in range(2):
            gz = jnp.dot(sm, wz_ref[d], preferred_element_type=f32) + bz_ref[d]
            la = _log_sigmoid(gz) * (1.0 / GLA_GATE_NORM)
            cum = _split_dot(tris[d], la)
            tot = _split_dot(blk, la)
            qe = q * jnp.exp(cum)
            kinv = k * jnp.exp(-cum)
            kend = k * jnp.exp(tot - cum)
            a = lax.dot_general(qe.astype(bf16), kinv.astype(bf16), NT, preferred_element_type=f32)
            a = jnp.where(masks[d], a, 0.0).astype(bf16)
            oin_ref[d, rows, :] = jnp.dot(a, v, preferred_element_type=f32)
            qe_ref[d, rows, :] = qe.astype(bf16)
            ke_ref[d, rows, :] = kend.astype(bf16)
            dec_ref[d, rows, :] = jnp.exp(tot)

    if has_h0:
        s_init = (h0_ref[0].T, h0_ref[1].T)
    else:
        s_init = (jnp.zeros((GLA_DV, GLA_DK), f32),) * 2

    def body(c, carry):
        new = []
        for d in range(2):
            st = carry[d]
            cc = c if d == 0 else n_chunk - 1 - c
            i0 = pl.multiple_of(cc * GLA_CHUNK, GLA_CHUNK)
            rows = pl.ds(i0, GLA_CHUNK)
            inter = lax.dot_general(qe_ref[d, rows, :], st.astype(bf16), NT, preferred_element_type=f32)
            oin_ref[d, rows, :] += inter
            upd = lax.dot_general(vb_ref[rows, :], ke_ref[d, rows, :], TN, preferred_element_type=f32)
            new.append(st * dec_ref[d, pl.ds(i0, 1), :] + upd)
        return tuple(new)

    s_fin = lax.fori_loop(0, n_chunk, body, s_init)

    o = oin_ref[0] + oin_ref[1]
    ms = jnp.mean(o * o, axis=-1, keepdims=True)
    o = o * lax.rsqrt(ms + RMS_EPS) * gn_ref[...]
    o_ref[...] = (o * _silu(r_ref[...])).astype(o_ref.dtype)
    if emit_state:
        st_ref[0, 0] = s_fin[0].T
        st_ref[0, 1] = s_fin[1].T


def _gla(proj, row_blk0, n_seq, seq, wz, bz, gn, h0):
    has_h0 = h0 is not None
    emit_state = not has_h0
    kq, kv_ = GLA_DK, GLA_DV
    in_specs = [
        pl.BlockSpec((seq, kq), lambda b, h: (row_blk0 + b, COL_GQ // kq + h)),
        pl.BlockSpec((seq, kq), lambda b, h: (row_blk0 + b, COL_GK // kq + h)),
        pl.BlockSpec((seq, kv_), lambda b, h: (row_blk0 + b, COL_GV // kv_ + h)),
        pl.BlockSpec((seq, kv_), lambda b, h: (row_blk0 + b, COL_GR // kv_ + h)),
        pl.BlockSpec((seq, SMALL_W), lambda b, h: (row_blk0 + b, COL_SMALL // SMALL_W)),
        pl.BlockSpec((2, SMALL_W, kq), lambda b, h: (0, 0, h)),
        pl.BlockSpec((2, 1, kq), lambda b, h: (0, 0, h)),
        pl.BlockSpec((1, kv_), lambda b, h: (0, 0)),
    ]
    args = [proj, proj, proj, proj, proj, wz, bz, gn]
    if has_h0:
        in_specs.append(pl.BlockSpec((None, 2, None, kq, kv_), lambda b, h: (b, 0, h, 0, 0)))
        args.append(h0)
    out_specs = [pl.BlockSpec((seq, kv_), lambda b, h: (b, h))]
    out_shape = [jax.ShapeDtypeStruct((n_seq * seq, GLA_VAL), bf16)]
    if emit_state:
        out_specs.append(pl.BlockSpec((1, 2, None, kq, kv_), lambda b, h: (b, 0, h, 0, 0)))
        out_shape.append(jax.ShapeDtypeStruct((n_seq, 2, GLA_HEADS, kq, kv_), f32))
    res = pl.pallas_call(
        functools.partial(_gla_kernel, seq=seq, has_h0=has_h0, emit_state=emit_state),
        grid=(n_seq, GLA_HEADS),
        in_specs=in_specs,
        out_specs=out_specs,
        out_shape=out_shape,
        scratch_shapes=[
            pltpu.VMEM((2, seq, kv_), f32),
            pltpu.VMEM((2, seq, kq), bf16),
            pltpu.VMEM((2, seq, kq), bf16),
            pltpu.VMEM((2, seq, kq), f32),
            pltpu.VMEM((seq, kv_), bf16),
        ],
        compiler_params=_cparams(("parallel", "arbitrary")),
        name="gla_lat" if has_h0 else "gla_ctx",
    )(*args)
    return res if emit_state else (res[0], None)


def _softplus(x):
    return jnp.maximum(x, 0.0) + jnp.log1p(jnp.exp(-jnp.abs(x)))


def _shift_rows(x, d, t_idx):
    n = x.shape[0]
    if d == 0:
        return x
    y = pltpu.roll(x, (-d) % n, 0)
    ok = (t_idx + d >= 0) & (t_idx + d < n)
    return jnp.where(ok, y, 0.0)


def _ssd_kernel(z_ref, x_ref, b_ref, c_ref, sm_ref, cwx_ref, cwb_ref, cwc_ref, cbx_ref, cbb_ref, cbc_ref,
                dtb_ref, a_ref, dsk_ref, *rest, seq, has_h0, emit_state):
    rest = list(rest)
    h0_ref = rest.pop(0) if has_h0 else None
    o_ref = rest.pop(0)
    st_ref = rest.pop(0) if emit_state else None
    y_ref, xs_ref, bm_ref, cm_ref = rest
    cs = SSD_CHUNK
    n_chunk = seq // cs
    hp = SSM_HPG * SSM_HEADDIM

    t_idx = lax.broadcasted_iota(jnp.int32, (seq, 1), 0)

    def conv_silu(src_ref, w_ref, bias_ref):
        xin = src_ref[...]
        acc = jnp.zeros_like(xin) + bias_ref[...]
        for j in range(SSM_CONV):
            acc = acc + w_ref[j:j + 1, :] * _shift_rows(xin, j - SSM_CONV // 2, t_idx)
        return _silu(acc)

    xs_ref[...] = conv_silu(x_ref, cwx_ref, cbx_ref)
    bm_ref[...] = conv_silu(b_ref, cwb_ref, cbb_ref)
    cm_ref[...] = conv_silu(c_ref, cwc_ref, cbc_ref)

    row = lax.broadcasted_iota(jnp.int32, (cs, cs), 0)
    col = lax.broadcasted_iota(jnp.int32, (cs, cs), 1)
    masks = (col <= row, col >= row)
    tris = tuple(jnp.where(m, 1.0, 0.0).astype(bf16) for m in masks)
    lane = lax.broadcasted_iota(jnp.int32, (1, SMALL_W), 1)

    dt_lane = (lane >= SMALL_DT) & (lane < SMALL_DT + 2 * SSM_HPG)
    a_row = a_ref[...]
    dt_bias = dtb_ref[...]

    def lane_of(d, j):
        return SMALL_DT + d * SSM_HPG + j

    def pick(arr, ln):
        return arr[:, ln:ln + 1]

    y_ref[...] = jnp.zeros_like(y_ref)
    states = []
    for d in range(2):
        if has_h0:
            st = h0_ref[d].reshape(hp, SSM_STATE)
        else:
            st = jnp.zeros((hp, SSM_STATE), f32)
        order = range(n_chunk) if d == 0 else range(n_chunk - 1, -1, -1)
        for cidx in order:
            rows = slice(cidx * cs, (cidx + 1) * cs)
            dt = jnp.where(dt_lane, _softplus(sm_ref[rows, :] + dt_bias), 0.0)
            dta = dt * a_row
            cum = _split3_dot(tris[d], dta)
            cum_t = cum.T
            tot = cum[0:1, :] if d == 1 else cum[cs - 1:cs, :]
            xs = xs_ref[rows, :]
            bm = bm_ref[rows, :].astype(bf16)
            cm = cm_ref[rows, :].astype(bf16)
            cb = lax.dot_general(cm, bm, NT, preferred_element_type=f32)
            inter = lax.dot_general(cm, st.astype(bf16), NT, preferred_element_type=f32)
            y_parts, xw_parts, dec_parts = [], [], []
            for j in range(SSM_HPG):
                ln = lane_of(d, j)
                cum_c = pick(cum, ln)
                dt_c = pick(dt, ln)
                tot_c = pick(tot, ln)
                cum_r = cum_t[ln:ln + 1, :]
                seg = jnp.exp(jnp.where(masks[d], cum_c - cum_r, -jnp.inf))
                xh = xs[:, j * SSM_HEADDIM:(j + 1) * SSM_HEADDIM]
                w = (cb * seg).astype(bf16)
                y_h = jnp.dot(w, (xh * dt_c).astype(bf16), preferred_element_type=f32)
                y_parts.append(y_h + jnp.exp(cum_c) * inter[:, j * SSM_HEADDIM:(j + 1) * SSM_HEADDIM])
                xw_parts.append(xh * (dt_c * jnp.exp(tot_c - cum_c)))
                dec_parts.append(jnp.broadcast_to(jnp.exp(tot_c), (SSM_HEADDIM, SSM_STATE)))
            y_ref[rows, :] += jnp.concatenate(y_parts, axis=1)
            xw = jnp.concatenate(xw_parts, axis=1).astype(bf16)
            upd = lax.dot_general(xw, bm, TN, preferred_element_type=f32)
            st = st * jnp.concatenate(dec_parts, axis=0) + upd
        states.append(st)

    y = y_ref[...] + dsk_ref[...] * xs_ref[...]
    o_ref[...] = y * _silu(z_ref[...])
    if emit_state:
        st_ref[0, 0] = states[0].reshape(SSM_HPG, SSM_HEADDIM, SSM_STATE)
        st_ref[0, 1] = states[1].reshape(SSM_HPG, SSM_HEADDIM, SSM_STATE)


def _split3_dot(m, x):
    hi = x.astype(bf16)
    r1 = x - hi.astype(f32)
    mid = r1.astype(bf16)
    lo = (r1 - mid.astype(f32)).astype(bf16)
    return (jnp.dot(m, hi, preferred_element_type=f32) + jnp.dot(m, mid, preferred_element_type=f32)
            + jnp.dot(m, lo, preferred_element_type=f32))


def _ssd(proj, row_blk0, n_seq, seq, conv_w, conv_b, dt_bias_row, a_row, d_skip, h0):
    has_h0 = h0 is not None
    emit_state = not has_h0
    hp = SSM_HPG * SSM_HEADDIM
    ns = SSM_STATE
    in_specs = [
        pl.BlockSpec((seq, hp), lambda b, g: (row_blk0 + b, COL_SZ // hp + g)),
        pl.BlockSpec((seq, hp), lambda b, g: (row_blk0 + b, COL_SX // hp + g)),
        pl.BlockSpec((seq, ns), lambda b, g: (row_blk0 + b, COL_SB // ns + g)),
        pl.BlockSpec((seq, ns), lambda b, g: (row_blk0 + b, COL_SC // ns + g)),
        pl.BlockSpec((seq, SMALL_W), lambda b, g: (row_blk0 + b, COL_SMALL // SMALL_W + g)),
        pl.BlockSpec((8, hp), lambda b, g: (0, g)),
        pl.BlockSpec((8, ns), lambda b, g: (0, SSM_INNER // ns + g)),
        pl.BlockSpec((8, ns), lambda b, g: (0, (SSM_INNER + SSM_BC) // ns + g)),
        pl.BlockSpec((1, hp), lambda b, g: (0, g)),
        pl.BlockSpec((1, ns), lambda b, g: (0, SSM_INNER // ns + g)),
        pl.BlockSpec((1, ns), lambda b, g: (0, (SSM_INNER + SSM_BC) // ns + g)),
        pl.BlockSpec((1, SMALL_W), lambda b, g: (0, g)),
        pl.BlockSpec((1, SMALL_W), lambda b, g: (0, g)),
        pl.BlockSpec((1, hp), lambda b, g: (0, g)),
    ]
    args = [proj, proj, proj, proj, proj, conv_w, conv_w, conv_w, conv_b, conv_b, conv_b,
            dt_bias_row, a_row, d_skip]
    if has_h0:
        in_specs.append(pl.BlockSpec((None, 2, SSM_HPG, SSM_HEADDIM, ns), lambda b, g: (b, 0, g, 0, 0)))
        args.append(h0)
    out_specs = [pl.BlockSpec((seq, hp), lambda b, g: (b, g))]
    out_shape = [jax.ShapeDtypeStruct((n_seq * seq, SSM_INNER), f32)]
    if emit_state:
        out_specs.append(pl.BlockSpec((1, 2, SSM_HPG, SSM_HEADDIM, ns), lambda b, g: (b, 0, g, 0, 0)))
        out_shape.append(jax.ShapeDtypeStruct((n_seq, 2, SSM_HEADS, SSM_HEADDIM, ns), f32))
    res = pl.pallas_call(
        functools.partial(_ssd_kernel, seq=seq, has_h0=has_h0, emit_state=emit_state),
        grid=(n_seq, SSM_GROUPS),
        in_specs=in_specs,
        out_specs=out_specs,
        out_shape=out_shape,
        scratch_shapes=[
            pltpu.VMEM((seq, hp), f32),
            pltpu.VMEM((seq, hp), f32),
            pltpu.VMEM((seq, ns), f32),
            pltpu.VMEM((seq, ns), f32),
        ],
        compiler_params=_cparams(("parallel", "arbitrary")),
        name="ssd_lat" if has_h0 else "ssd_ctx",
    )(*args)
    return res if emit_state else (res[0], None)


def _attn_ctx_kernel(q_ref, k_ref, v_ref, sink_ref, o_ref):
    k = k_ref[...].astype(bf16)
    v = v_ref[...].astype(bf16)
    kvh = pl.program_id(1)
    for g in range(Q_PER_KV):
        q = q_ref[:, g * HEAD_DIM:(g + 1) * HEAD_DIM].astype(bf16)
        s = lax.dot_general(q, k, NT, preferred_element_type=f32) * (HEAD_DIM ** -0.5)
        sink = sink_ref[pl.ds(kvh * Q_PER_KV + g, 1), 0:1]
        m = jnp.maximum(jnp.max(s, axis=-1, keepdims=True), sink)
        p = jnp.exp(s - m)
        den = jnp.sum(p, axis=-1, keepdims=True) + jnp.exp(sink - m)
        o = jnp.dot(p.astype(bf16), v, preferred_element_type=f32) / den
        o_ref[:, g * HEAD_DIM:(g + 1) * HEAD_DIM] = o.astype(o_ref.dtype)


def _attn_ctx(proj, sink8, n_seq=BATCH):
    qw = Q_PER_KV * HEAD_DIM
    return pl.pallas_call(
        _attn_ctx_kernel,
        grid=(n_seq, KV_HEADS),
        in_specs=[
            pl.BlockSpec((SEQ, qw), lambda b, h: (b, COL_AQ // qw + h)),
            pl.BlockSpec((SEQ, HEAD_DIM), lambda b, h: (b, COL_AK // HEAD_DIM + h)),
            pl.BlockSpec((SEQ, HEAD_DIM), lambda b, h: (b, COL_AV // HEAD_DIM + h)),
            pl.BlockSpec((ATTN_HEADS, 128), lambda b, h: (0, 0)),
        ],
        out_specs=pl.BlockSpec((SEQ, qw), lambda b, h: (b, h)),
        out_shape=jax.ShapeDtypeStruct((n_seq * SEQ, ATTN_Q), bf16),
        compiler_params=_cparams(("parallel", "arbitrary")),
        name="attn_ctx",
    )(proj, proj, proj, sink8)


def _rope(x, cos, sin_signed):
    quarter = HEAD_DIM // 4
    lane = lax.broadcasted_iota(jnp.int32, x.shape, 1)
    first = (lane % (2 * quarter)) < quarter
    partner = jnp.where(first, pltpu.roll(x, HEAD_DIM - quarter, 1), pltpu.roll(x, quarter, 1))
    return x * cos + partner * sin_signed


def _attn_lat_kernel(q_ref, k_ref, v_ref, kc_ref, vc_ref, cos_ref, sin_ref, sink_ref, o_ref, kr_ref):
    kvh = pl.program_id(1)
    cos = cos_ref[...]
    sin = sin_ref[...]
    kr_ref[...] = _rope(k_ref[...], cos, sin).astype(bf16)
    kc = kc_ref[...].astype(bf16)
    vc = vc_ref[...].astype(bf16)
    blk = WINDOW
    n_blk = DEC_SEQ // blk
    scale = HEAD_DIM ** -0.5
    for i in range(n_blk):
        lo = max(i - 1, 0) * blk
        hi = min(i + 2, n_blk) * blk
        kw = kr_ref[lo:hi, :]
        vw = v_ref[lo:hi, :].astype(bf16)
        qpos = i * blk + lax.broadcasted_iota(jnp.int32, (blk, hi - lo), 0)
        kpos = lo + lax.broadcasted_iota(jnp.int32, (blk, hi - lo), 1)
        win = jnp.abs(qpos - kpos) <= WINDOW
        rows = slice(i * blk, (i + 1) * blk)
        for g in range(Q_PER_KV):
            cols = slice(g * HEAD_DIM, (g + 1) * HEAD_DIM)
            q = _rope(q_ref[rows, cols], cos[rows, :], sin[rows, :]).astype(bf16)
            s_c = lax.dot_general(q, kc, NT, preferred_element_type=f32) * scale
            s_w = lax.dot_general(q, kw, NT, preferred_element_type=f32) * scale
            s_w = jnp.where(win, s_w, -jnp.inf)
            sink = sink_ref[pl.ds(kvh * Q_PER_KV + g, 1), 0:1]
            m = jnp.maximum(jnp.maximum(jnp.max(s_c, axis=-1, keepdims=True),
                                        jnp.max(s_w, axis=-1, keepdims=True)), sink)
            p_c = jnp.exp(s_c - m)
            p_w = jnp.exp(s_w - m)
            den = (jnp.sum(p_c, axis=-1, keepdims=True) + jnp.sum(p_w, axis=-1, keepdims=True)
                   + jnp.exp(sink - m))
            o = (jnp.dot(p_c.astype(bf16), vc, preferred_element_type=f32)
                 + jnp.dot(p_w.astype(bf16), vw, preferred_element_type=f32)) / den
            o_ref[rows, cols] = o.astype(o_ref.dtype)


def _attn_lat(proj, cache_k, cache_v, l, cos, sin, sink8, rb0=N_PROMPT // DEC_SEQ, n_seq=DEC_BATCH):
    qw = Q_PER_KV * HEAD_DIM
    ck = cache_k.reshape(n_seq, DEPTH, PAST_LEN, ATTN_KV)
    cv = cache_v.reshape(n_seq, DEPTH, PAST_LEN, ATTN_KV)
    return pl.pallas_call(
        _attn_lat_kernel,
        grid=(n_seq, KV_HEADS),
        in_specs=[
            pl.BlockSpec((DEC_SEQ, qw), lambda b, h: (rb0 + b, COL_AQ // qw + h)),
            pl.BlockSpec((DEC_SEQ, HEAD_DIM), lambda b, h: (rb0 + b, COL_AK // HEAD_DIM + h)),
            pl.BlockSpec((DEC_SEQ, HEAD_DIM), lambda b, h: (rb0 + b, COL_AV // HEAD_DIM + h)),
            pl.BlockSpec((None, None, PAST_LEN, HEAD_DIM), lambda b, h: (b, l, 0, h)),
            pl.BlockSpec((None, None, PAST_LEN, HEAD_DIM), lambda b, h: (b, l, 0, h)),
            pl.BlockSpec((DEC_SEQ, HEAD_DIM), lambda b, h: (0, 0)),
            pl.BlockSpec((DEC_SEQ, HEAD_DIM), lambda b, h: (0, 0)),
            pl.BlockSpec((ATTN_HEADS, 128), lambda b, h: (0, 0)),
        ],
        out_specs=pl.BlockSpec((DEC_SEQ, qw), lambda b, h: (b, h)),
        out_shape=jax.ShapeDtypeStruct((n_seq * DEC_SEQ, ATTN_Q), bf16),
        scratch_shapes=[pltpu.VMEM((DEC_SEQ, HEAD_DIM), bf16)],
        compiler_params=_cparams(("parallel", "arbitrary")),
        name="attn_lat",
    )(proj, proj, proj, ck, cv, cos, sin, sink8)


def _rope_tables():
    quarter = HEAD_DIM // 4
    freqs = ROPE_THETA ** (-np.arange(quarter, dtype=np.float32) / quarter)
    t = np.arange(DEC_SEQ)
    cos = np.zeros((DEC_SEQ, HEAD_DIM), np.float32)
    sin = np.zeros((DEC_SEQ, HEAD_DIM), np.float32)
    for half, pos in enumerate((t // GRID_W, t % GRID_W)):
        ang = pos.astype(np.float32)[:, None] * freqs[None, :]
        base = half * 2 * quarter
        cos[:, base:base + quarter] = np.cos(ang)
        cos[:, base + quarter:base + 2 * quarter] = np.cos(ang)
        sin[:, base:base + quarter] = -np.sin(ang)
        sin[:, base + quarter:base + 2 * quarter] = np.sin(ang)
    return jnp.asarray(cos), jnp.asarray(sin)


def _merge_kernel(x_ref, gt_ref, og_ref, os_ref, oa_ref, b0_ref, b1_ref, b2_ref, sn_ref,
                  wg_ref, ws_ref, wa_ref, wo_ref, o_ref, osn_ref, acc_ref):
    j = pl.program_id(1)

    @pl.when(j == 0)
    def _():
        y = os_ref[...]
        ms = jnp.mean(y * y, axis=-1, keepdims=True)
        osn_ref[...] = (y * lax.rsqrt(ms + RMS_EPS) * sn_ref[...]).astype(bf16)
        acc_ref[...] = jnp.zeros_like(acc_ref)

    m = (jax.nn.sigmoid(b0_ref[...]) * jnp.dot(og_ref[...], wg_ref[...], preferred_element_type=f32)
         + jax.nn.sigmoid(b1_ref[...]) * jnp.dot(osn_ref[...], ws_ref[...], preferred_element_type=f32)
         + jax.nn.sigmoid(b2_ref[...]) * jnp.dot(oa_ref[...], wa_ref[...], preferred_element_type=f32))
    acc_ref[...] += jnp.dot(m.astype(bf16), wo_ref[...], preferred_element_type=f32)

    @pl.when(j == pl.num_programs(1) - 1)
    def _():
        o_ref[...] = x_ref[...] + gt_ref[0] * acc_ref[...]


def _merge(x, mod3, l, proj, o_gla, o_ssm, o_att, ssm_norm, w_g, w_s, w_a, w_o):
    n_tok = x.shape[0]
    nj = D_MODEL // TJ

    def br_spec(k):
        return pl.BlockSpec((TM, TJ), lambda i, j: (i, (COL_BR + k * D_MODEL) // TJ + j))

    def wbr_spec():
        return pl.BlockSpec((None, GLA_VAL, TJ), lambda i, j: (l, 0, j))

    return pl.pallas_call(
        _merge_kernel,
        grid=(n_tok // TM, nj),
        in_specs=[
            pl.BlockSpec((TM, D_MODEL), lambda i, j: (i, 0)),
            pl.BlockSpec((1, 1, D_MODEL), lambda i, j: (_mod_row(i, TM), 0, 5)),
            pl.BlockSpec((TM, GLA_VAL), lambda i, j: (i, 0)),
            pl.BlockSpec((TM, SSM_INNER), lambda i, j: (i, 0)),
            pl.BlockSpec((TM, ATTN_Q), lambda i, j: (i, 0)),
            br_spec(0), br_spec(1), br_spec(2),
            pl.BlockSpec((None, 1, SSM_INNER), lambda i, j: (l, 0, 0)),
            wbr_spec(), wbr_spec(), wbr_spec(),
            pl.BlockSpec((None, TJ, D_MODEL), lambda i, j: (l, j, 0)),
        ],
        out_specs=pl.BlockSpec((TM, D_MODEL), lambda i, j: (i, 0)),
        out_shape=jax.ShapeDtypeStruct((n_tok, D_MODEL), f32),
        scratch_shapes=[pltpu.VMEM((TM, SSM_INNER), bf16), pltpu.VMEM((TM, D_MODEL), f32)],
        compiler_params=_cparams(("parallel", "arbitrary")),
        name="merge",
    )(x, mod3, o_gla, o_ssm, o_att, proj, proj, proj, ssm_norm, w_g, w_s, w_a, w_o)


def _permute_w_in(w_in):
    o = np.cumsum((0, GLA_QK, GLA_QK, GLA_VAL, GLA_VAL, 2 * GLA_RANK, SSM_INNER, CONV_CH, 2 * SSM_HEADS,
                   ATTN_Q, ATTN_KV, ATTN_KV, 3 * D_MODEL))
    seg = lambda a: w_in[:, :, int(o[a]):int(o[a + 1])]
    zeros = lambda n: jnp.zeros((DEPTH, D_MODEL, n), w_in.dtype)
    parts = [seg(0), seg(1), seg(2), seg(3), seg(5), seg(6), seg(8), seg(9), seg(10)]
    dt0 = int(o[7])
    for g in range(SSM_GROUPS):
        parts.append(seg(4))
        for d in range(2):
            lo = dt0 + d * SSM_HEADS + g * SSM_HPG
            parts.append(w_in[:, :, lo:lo + SSM_HPG])
        parts.append(zeros(SMALL_W - 2 * GLA_RANK - 2 * SSM_HPG))
    parts.append(seg(11))
    return jnp.concatenate(parts, axis=-1).astype(bf16)


def kernel(x_prompt, x_sample, c, cache_k, cache_v, state_gla, state_ssm, c_ctx, w_mod, b_mod, ffn1_norm,
           ffn1_w_gate, ffn1_w_up, ffn1_w_down, mix_norm, w_in, gla_w_up, gla_b_up, gla_norm, ssm_conv_w,
           ssm_conv_b, ssm_dt_bias, ssm_a_log, ssm_d, ssm_norm, attn_sink, w_br_gla, w_br_ssm, w_br_attn,
           w_out, ffn2_norm, ffn2_w_gate, ffn2_w_up, ffn2_w_down, final_norm):
    x = jnp.concatenate([x_prompt.reshape(N_PROMPT, D_MODEL), x_sample.reshape(N_SAMPLE, D_MODEL)], axis=0)
    cvec = jnp.concatenate([c_ctx[None], c, jnp.zeros((MOD_ROWS - 1 - DEC_BATCH, D_MODEL), f32)], axis=0)
    mod = _modulation(cvec, w_mod, b_mod)

    w_in_p = _permute_w_in(w_in)
    f1g, f1u, f1d = ffn1_w_gate.astype(bf16), ffn1_w_up.astype(bf16), ffn1_w_down.astype(bf16)
    f2g, f2u, f2d = ffn2_w_gate.astype(bf16), ffn2_w_up.astype(bf16), ffn2_w_down.astype(bf16)
    wbg, wbs, wba, wo = (w_br_gla.astype(bf16), w_br_ssm.astype(bf16), w_br_attn.astype(bf16),
                         w_out.astype(bf16))

    wz = jnp.zeros((DEPTH, 2, SMALL_W, GLA_QK), f32)
    for d in range(2):
        wz = wz.at[:, d, d * GLA_RANK:(d + 1) * GLA_RANK, :].set(gla_w_up[:, d])
    wz = wz.astype(bf16)
    bz = gla_b_up.reshape(DEPTH, 2, 1, GLA_QK)

    def dt_lanes(v):
        vg = v.reshape(DEPTH, 2, SSM_GROUPS, SSM_HPG).transpose(0, 2, 1, 3).reshape(DEPTH, SSM_GROUPS, 2 * SSM_HPG)
        row = jnp.zeros((DEPTH, SSM_GROUPS, SMALL_W), f32)
        row = row.at[:, :, SMALL_DT:SMALL_DT + 2 * SSM_HPG].set(vg)
        return row.reshape(DEPTH, 1, SSM_GROUPS * SMALL_W)

    dtb_rows = dt_lanes(ssm_dt_bias)
    a_rows = dt_lanes(-jnp.exp(ssm_a_log))
    d_skip = jnp.repeat(ssm_d, SSM_HEADDIM, axis=-1).reshape(DEPTH, 1, SSM_INNER)
    conv_w = jnp.concatenate([ssm_conv_w, jnp.zeros((DEPTH, 8 - SSM_CONV, CONV_CH), f32)], axis=1)
    conv_b = ssm_conv_b.reshape(DEPTH, 1, CONV_CH)
    sink8 = jnp.broadcast_to(attn_sink[:, :, None], (DEPTH, ATTN_HEADS, 128))
    cos, sin = _rope_tables()

    n1, nm, n2 = (w.reshape(DEPTH, 1, D_MODEL) for w in (ffn1_norm, mix_norm, ffn2_norm))
    sn = ssm_norm.reshape(DEPTH, 1, SSM_INNER)

    new_k, new_v, new_gla, new_ssm = [], [], [], []
    for l in range(DEPTH):
        mod3 = mod[l].reshape(MOD_ROWS, 1, N_MOD * D_MODEL)
        x = _ffn(x, mod3, l, 0, n1, f1g, f1u, f1d)
        proj = _inproj(x, mod3, l, nm, w_in_p)

        og_c, st_g = _gla(proj, 0, BATCH, SEQ, wz[l], bz[l], gla_norm[l][None], None)
        og_s, _ = _gla(proj, N_PROMPT // DEC_SEQ, DEC_BATCH, DEC_SEQ, wz[l], bz[l], gla_norm[l][None],
                       state_gla[:, l])
        os_c, st_s = _ssd(proj, 0, BATCH, SEQ, conv_w[l], conv_b[l], dtb_rows[l], a_rows[l], d_skip[l], None)
        os_s, _ = _ssd(proj, N_PROMPT // DEC_SEQ, DEC_BATCH, DEC_SEQ, conv_w[l], conv_b[l], dtb_rows[l],
                       a_rows[l], d_skip[l], state_ssm[:, l])
        oa_c = _attn_ctx(proj, sink8[l])
        oa_s = _attn_lat(proj, cache_k, cache_v, l, cos, sin, sink8[l])

        o_gla = jnp.concatenate([og_c, og_s], axis=0)
        o_ssm = jnp.concatenate([os_c, os_s], axis=0)
        o_att = jnp.concatenate([oa_c, oa_s], axis=0)
        x = _merge(x, mod3, l, proj, o_gla, o_ssm, o_att, sn, wbg, wbs, wba, wo)
        x = _ffn(x, mod3, l, 2, n2, f2g, f2u, f2d, final_w=final_norm if l == DEPTH - 1 else None)

        new_k.append(proj[:N_PROMPT, COL_AK:COL_AK + ATTN_KV].reshape(BATCH, SEQ, KV_HEADS, HEAD_DIM))
        new_v.append(proj[:N_PROMPT, COL_AV:COL_AV + ATTN_KV].reshape(BATCH, SEQ, KV_HEADS, HEAD_DIM))
        new_gla.append(st_g)
        new_ssm.append(st_s)

    y_prompt = x[:N_PROMPT].reshape(BATCH, SEQ, D_MODEL)
    y_sample = x[N_PROMPT:].reshape(DEC_BATCH, DEC_SEQ, D_MODEL)
    return (y_prompt, y_sample, jnp.stack(new_k, axis=1), jnp.stack(new_v, axis=1),
            jnp.stack(new_gla, axis=1), jnp.stack(new_ssm, axis=1))
```

```python
import functools
import math

import numpy as np
import jax
import jax.numpy as jnp
from jax import lax
from jax.experimental import pallas as pl
from jax.experimental.pallas import tpu as pltpu

f32 = jnp.float32
bf16 = jnp.bfloat16

D_MODEL = 2048
BATCH = 32
SEQ = 256
DEPTH = 2
DEC_BATCH = 2
DEC_SEQ = 1024
PAST_LEN = 512
GRID_W = 64
RMS_EPS = 1e-6
N_MOD = 9
D_FF = 5632
GLA_HEADS = 4
GLA_DK = 128
GLA_DV = 256
GLA_RANK = 16
GLA_GATE_NORM = 16.0
GLA_CHUNK = 16
SSM_HEADS = 16
SSM_HEADDIM = 64
SSM_GROUPS = 4
SSM_HPG = SSM_HEADS // SSM_GROUPS
SSM_STATE = 128
SSM_CONV = 5
SSM_INNER = SSM_HEADS * SSM_HEADDIM
SSM_BC = SSM_GROUPS * SSM_STATE
CONV_CH = SSM_INNER + 2 * SSM_BC
ATTN_HEADS = 8
KV_HEADS = 2
Q_PER_KV = ATTN_HEADS // KV_HEADS
HEAD_DIM = 128
WINDOW = 128
ROPE_THETA = 10000.0
GLA_QK = GLA_HEADS * GLA_DK
GLA_VAL = GLA_HEADS * GLA_DV
ATTN_Q = ATTN_HEADS * HEAD_DIM
ATTN_KV = KV_HEADS * HEAD_DIM

N_PROMPT = BATCH * SEQ
N_SAMPLE = DEC_BATCH * DEC_SEQ
N_TOK = N_PROMPT + N_SAMPLE
MOD_ROWS = 8

COL_GQ = 0
COL_GK = COL_GQ + GLA_QK
COL_GV = COL_GK + GLA_QK
COL_GR = COL_GV + GLA_VAL
COL_SZ = COL_GR + GLA_VAL
COL_SX = COL_SZ + SSM_INNER
COL_SB = COL_SX + SSM_INNER
COL_SC = COL_SB + SSM_BC
COL_AQ = COL_SC + SSM_BC
COL_AK = COL_AQ + ATTN_Q
COL_AV = COL_AK + ATTN_KV
COL_SMALL = COL_AV + ATTN_KV
SMALL_W = 128
SMALL_DT = 2 * GLA_RANK
COL_BR = 8192
D_INP = COL_BR + 3 * D_MODEL

TM = 512
TF = 512
TN_IN = 1024
TN_MOD = 1024
TJ = 256
GLA_SB = 256
GLA_SCAN_UNROLL = 16
SSD_CHUNK = 128
VMEM_LIMIT = 56 * 1024 * 1024

NT = (((1,), (1,)), ((), ()))
TN = (((0,), (0,)), ((), ()))


def _silu(x):
    return x * jax.nn.sigmoid(x)


def _mod_row(i, tm):
    n_p = N_PROMPT // tm
    per = DEC_SEQ // tm
    return jnp.where(i < n_p, 0, 1 + (i - n_p) // per)


def _cparams(sem):
    return pltpu.CompilerParams(dimension_semantics=sem, vmem_limit_bytes=VMEM_LIMIT)


def _mod_kernel(c_ref, w_ref, b_ref, o_ref):
    s = _silu(c_ref[...]).astype(bf16)
    o_ref[0] = jnp.dot(s, w_ref[0].astype(bf16), preferred_element_type=f32) + b_ref[0]


def _modulation(cvec, w_mod, b_mod):
    n = N_MOD * D_MODEL
    return pl.pallas_call(
        _mod_kernel,
        grid=(DEPTH, n // TN_MOD),
        in_specs=[
            pl.BlockSpec((MOD_ROWS, D_MODEL), lambda l, j: (0, 0)),
            pl.BlockSpec((1, D_MODEL, TN_MOD), lambda l, j: (l, 0, j)),
            pl.BlockSpec((1, 1, TN_MOD), lambda l, j: (l, 0, j)),
        ],
        out_specs=pl.BlockSpec((1, MOD_ROWS, TN_MOD), lambda l, j: (l, 0, j)),
        out_shape=jax.ShapeDtypeStruct((DEPTH, MOD_ROWS, n), f32),
        compiler_params=_cparams(("arbitrary", "arbitrary")),
        name="modulation",
    )(cvec, w_mod, b_mod.reshape(DEPTH, 1, n))


def _norm_mod(x, nw, sh, sc):
    ms = jnp.mean(x * x, axis=-1, keepdims=True)
    h = x * lax.rsqrt(ms + RMS_EPS) * nw
    return h * (1.0 + sc) + sh


def _pick_stream(i, tm, ctx_val, lat_val):
    return jnp.where(i < N_PROMPT // tm, ctx_val, lat_val)


def _ctx_blk(i, tm):
    return jnp.minimum(i, N_PROMPT // tm - 1)


def _lat_blk(i, tm):
    return jnp.maximum(i - N_PROMPT // tm, 0)


def _ffn_kernel(*refs, split_in, final):
    refs = list(refs)
    x_refs = [refs.pop(0) for _ in range(2 if split_in else 1)]
    sh_ref, sc_ref, gt_ref, nw_ref, wg_ref, wu_ref, wd_ref = refs[:7]
    refs = refs[7:]
    fw_ref = refs.pop(0) if final else None
    o_refs = [refs.pop(0) for _ in range(2 if final else 1)]
    h_ref, acc_ref = refs
    i = pl.program_id(0)
    j = pl.program_id(1)

    def read_x():
        if split_in:
            return _pick_stream(i, TM, x_refs[0][...], x_refs[1][...])
        return x_refs[0][...]

    @pl.when(j == 0)
    def _():
        h_ref[...] = _norm_mod(read_x(), nw_ref[...], sh_ref[0], sc_ref[0]).astype(bf16)
        acc_ref[...] = jnp.zeros_like(acc_ref)

    h = h_ref[...]
    g = jnp.dot(h, wg_ref[...], preferred_element_type=f32)
    u = jnp.dot(h, wu_ref[...], preferred_element_type=f32)
    a = (_silu(g) * u).astype(bf16)
    acc_ref[...] += jnp.dot(a, wd_ref[...], preferred_element_type=f32)

    last = j == pl.num_programs(1) - 1

    def result():
        y = read_x() + 0.5 * gt_ref[0] * acc_ref[...]
        if final:
            ms = jnp.mean(y * y, axis=-1, keepdims=True)
            y = y * lax.rsqrt(ms + RMS_EPS) * fw_ref[...]
        return y

    if final:
        @pl.when(last & (i < N_PROMPT // TM))
        def _():
            o_refs[0][...] = result()

        @pl.when(last & (i >= N_PROMPT // TM))
        def _():
            o_refs[1][...] = result()
    else:
        @pl.when(last)
        def _():
            o_refs[0][...] = result()


def _ffn(x, mod3, l, slot, nw, wg, wu, wd, final_w=None):
    split_in = isinstance(x, tuple)
    final = final_w is not None

    def mod_spec(k):
        return pl.BlockSpec((1, 1, D_MODEL), lambda i, j: (_mod_row(i, TM), 0, 3 * slot + k))

    if split_in:
        in_specs = [pl.BlockSpec((TM, D_MODEL), lambda i, j: (_ctx_blk(i, TM), 0)),
                    pl.BlockSpec((TM, D_MODEL), lambda i, j: (_lat_blk(i, TM), 0))]
        args = list(x)
    else:
        in_specs = [pl.BlockSpec((TM, D_MODEL), lambda i, j: (i, 0))]
        args = [x]
    in_specs += [
        mod_spec(0), mod_spec(1), mod_spec(2),
        pl.BlockSpec((None, 1, D_MODEL), lambda i, j: (l, 0, 0)),
        pl.BlockSpec((None, D_MODEL, TF), lambda i, j: (l, 0, j)),
        pl.BlockSpec((None, D_MODEL, TF), lambda i, j: (l, 0, j)),
        pl.BlockSpec((None, TF, D_MODEL), lambda i, j: (l, j, 0)),
    ]
    args += [mod3, mod3, mod3, nw, wg, wu, wd]
    if final:
        in_specs.append(pl.BlockSpec((1, D_MODEL), lambda i, j: (0, 0)))
        args.append(final_w.reshape(1, D_MODEL))
        out_specs = [pl.BlockSpec((TM, D_MODEL), lambda i, j: (_ctx_blk(i, TM), 0)),
                     pl.BlockSpec((TM, D_MODEL), lambda i, j: (_lat_blk(i, TM), 0))]
        out_shape = [jax.ShapeDtypeStruct((N_PROMPT, D_MODEL), f32), jax.ShapeDtypeStruct((N_SAMPLE, D_MODEL), f32)]
    else:
        out_specs = pl.BlockSpec((TM, D_MODEL), lambda i, j: (i, 0))
        out_shape = jax.ShapeDtypeStruct((N_TOK, D_MODEL), f32)
    return pl.pallas_call(
        functools.partial(_ffn_kernel, split_in=split_in, final=final),
        grid=(N_TOK // TM, D_FF // TF),
        in_specs=in_specs,
        out_specs=out_specs,
        out_shape=out_shape,
        scratch_shapes=[pltpu.VMEM((TM, D_MODEL), bf16), pltpu.VMEM((TM, D_MODEL), f32)],
        compiler_params=_cparams(("arbitrary", "arbitrary")),
        name="ffn_final" if final else ("ffn_first" if split_in else "ffn"),
    )(*args)


def _inproj_kernel(x_ref, sh_ref, sc_ref, nw_ref, w_ref, o_ref, h_ref):
    @pl.when(pl.program_id(1) == 0)
    def _():
        h_ref[...] = _norm_mod(x_ref[...], nw_ref[...], sh_ref[0], sc_ref[0]).astype(bf16)

    o_ref[...] = jnp.dot(h_ref[...], w_ref[...], preferred_element_type=f32)


def _inproj(x, mod3, l, nw, w):
    n_tok = N_TOK

    def mod_spec(k):
        return pl.BlockSpec((1, 1, D_MODEL), lambda i, j: (_mod_row(i, TM), 0, 3 + k))

    return pl.pallas_call(
        _inproj_kernel,
        grid=(n_tok // TM, D_INP // TN_IN),
        in_specs=[
            pl.BlockSpec((TM, D_MODEL), lambda i, j: (i, 0)),
            mod_spec(0), mod_spec(1),
            pl.BlockSpec((None, 1, D_MODEL), lambda i, j: (l, 0, 0)),
            pl.BlockSpec((None, D_MODEL, TN_IN), lambda i, j: (l, 0, j)),
        ],
        out_specs=pl.BlockSpec((TM, TN_IN), lambda i, j: (i, j)),
        out_shape=jax.ShapeDtypeStruct((n_tok, D_INP), f32),
        scratch_shapes=[pltpu.VMEM((TM, D_MODEL), bf16)],
        compiler_params=_cparams(("parallel", "arbitrary")),
        name="inproj",
    )(x, mod3, mod3, nw, w)


def _split_dot(m, x):
    hi = x.astype(bf16)
    lo = (x - hi.astype(f32)).astype(bf16)
    return (jnp.dot(m, hi, preferred_element_type=f32) + jnp.dot(m, lo, preferred_element_type=f32))


def _log_sigmoid(x):
    return jnp.minimum(x, 0.0) - jnp.log1p(jnp.exp(-jnp.abs(x)))


def _gla_kernel(q_ref, k_ref, v_ref, r_ref, sm_ref, wz_ref, bz_ref, gn_ref, *rest, seq, has_h0, emit_state):
    rest = list(rest)
    h0_ref = rest.pop(0) if has_h0 else None
    o_ref = rest.pop(0)
    st_ref = rest.pop(0) if emit_state else None
    oin_ref, qe_ref, ke_ref, dec_ref, vb_ref = rest
    n_chunk = seq // GLA_CHUNK
    sb = GLA_SB

    row = lax.broadcasted_iota(jnp.int32, (sb, sb), 0)
    col = lax.broadcasted_iota(jnp.int32, (sb, sb), 1)
    same = (row // GLA_CHUNK) == (col // GLA_CHUNK)
    masks = (same & (col <= row), same & (col >= row))
    tris = tuple(jnp.where(m, 1.0, 0.0).astype(bf16) for m in masks)
    blk = jnp.where(same, 1.0, 0.0).astype(bf16)

    for s in range(seq // sb):
        rows = slice(s * sb, (s + 1) * sb)
        sm = sm_ref[rows, :].astype(bf16)
        q = q_ref[rows, :] * (GLA_DK ** -0.5)
        k = k_ref[rows, :]
        v = v_ref[rows, :].astype(bf16)
        vb_ref[rows, :] = v
        for d in range(2):
            gz = jnp.dot(sm, wz_ref[d], preferred_element_type=f32) + bz_ref[d]
            la = _log_sigmoid(gz) * (1.0 / GLA_GATE_NORM)
            cum = _split_dot(tris[d], la)
            tot = _split_dot(blk, la)
            qe = q * jnp.exp(cum)
            kinv = k * jnp.exp(-cum)
            kend = k * jnp.exp(tot - cum)
            a = lax.dot_general(qe.astype(bf16), kinv.astype(bf16), NT, preferred_element_type=f32)
            a = jnp.where(masks[d], a, 0.0).astype(bf16)
            oin_ref[d, rows, :] = jnp.dot(a, v, preferred_element_type=f32)
            qe_ref[d, rows, :] = qe.astype(bf16)
            ke_ref[d, rows, :] = kend.astype(bf16)
            dec_ref[d, rows, :] = jnp.exp(tot)

    if has_h0:
        s_init = (h0_ref[0].T, h0_ref[1].T)
    else:
        s_init = (jnp.zeros((GLA_DV, GLA_DK), f32),) * 2

    unroll = min(n_chunk, GLA_SCAN_UNROLL)
    n_outer = n_chunk // unroll

    def body(o, carry):
        sts = list(carry)
        for u in range(unroll):
            for d in range(2):
                c = o * unroll + u
                i0 = (c if d == 0 else n_chunk - 1 - c) * GLA_CHUNK
                if n_outer > 1:
                    i0 = pl.multiple_of(i0, GLA_CHUNK)
                rows = pl.ds(i0, GLA_CHUNK)
                inter = lax.dot_general(qe_ref[d, rows, :], sts[d].astype(bf16), NT, preferred_element_type=f32)
                oin_ref[d, rows, :] += inter
                upd = lax.dot_general(vb_ref[rows, :], ke_ref[d, rows, :], TN, preferred_element_type=f32)
                sts[d] = sts[d] * dec_ref[d, pl.ds(i0, 1), :] + upd
        return tuple(sts)

    s_fin = body(0, s_init) if n_outer == 1 else lax.fori_loop(0, n_outer, body, s_init)

    o = oin_ref[0] + oin_ref[1]
    ms = jnp.mean(o * o, axis=-1, keepdims=True)
    o = o * lax.rsqrt(ms + RMS_EPS) * gn_ref[...]
    o_ref[...] = (o * _silu(r_ref[...])).astype(o_ref.dtype)
    if emit_state:
        st_ref[0, 0] = s_fin[0].T
        st_ref[0, 1] = s_fin[1].T


def _gla(proj, row_blk0, n_seq, seq, wz, bz, gn, h0):
    has_h0 = h0 is not None
    emit_state = not has_h0
    kq, kv_ = GLA_DK, GLA_DV
    in_specs = [
        pl.BlockSpec((seq, kq), lambda b, h: (row_blk0 + b, COL_GQ // kq + h)),
        pl.BlockSpec((seq, kq), lambda b, h: (row_blk0 + b, COL_GK // kq + h)),
        pl.BlockSpec((seq, kv_), lambda b, h: (row_blk0 + b, COL_GV // kv_ + h)),
        pl.BlockSpec((seq, kv_), lambda b, h: (row_blk0 + b, COL_GR // kv_ + h)),
        pl.BlockSpec((seq, SMALL_W), lambda b, h: (row_blk0 + b, COL_SMALL // SMALL_W)),
        pl.BlockSpec((2, SMALL_W, kq), lambda b, h: (0, 0, h)),
        pl.BlockSpec((2, 1, kq), lambda b, h: (0, 0, h)),
        pl.BlockSpec((1, kv_), lambda b, h: (0, 0)),
    ]
    args = [proj, proj, proj, proj, proj, wz, bz, gn]
    if has_h0:
        in_specs.append(pl.BlockSpec((None, 2, None, kq, kv_), lambda b, h: (b, 0, h, 0, 0)))
        args.append(h0)
    out_specs = [pl.BlockSpec((seq, kv_), lambda b, h: (b, h))]
    out_shape = [jax.ShapeDtypeStruct((n_seq * seq, GLA_VAL), bf16)]
    if emit_state:
        out_specs.append(pl.BlockSpec((1, 2, None, kq, kv_), lambda b, h: (b, 0, h, 0, 0)))
        out_shape.append(jax.ShapeDtypeStruct((n_seq, 2, GLA_HEADS, kq, kv_), f32))
    res = pl.pallas_call(
        functools.partial(_gla_kernel, seq=seq, has_h0=has_h0, emit_state=emit_state),
        grid=(n_seq, GLA_HEADS),
        in_specs=in_specs,
        out_specs=out_specs,
        out_shape=out_shape,
        scratch_shapes=[
            pltpu.VMEM((2, seq, kv_), f32),
            pltpu.VMEM((2, seq, kq), bf16),
            pltpu.VMEM((2, seq, kq), bf16),
            pltpu.VMEM((2, seq, kq), f32),
            pltpu.VMEM((seq, kv_), bf16),
        ],
        compiler_params=_cparams(("parallel", "arbitrary")),
        name="gla_lat" if has_h0 else "gla_ctx",
    )(*args)
    return res if emit_state else (res[0], None)


def _softplus(x):
    return jnp.maximum(x, 0.0) + jnp.log1p(jnp.exp(-jnp.abs(x)))


def _split3(x):
    hi = x.astype(bf16)
    r1 = x - hi.astype(f32)
    mid = r1.astype(bf16)
    lo = (r1 - mid.astype(f32)).astype(bf16)
    return hi, mid, lo


def _split3_dot(m, x):
    return sum(jnp.dot(m, t, preferred_element_type=f32) for t in _split3(x))


def _dot_sel(x, sel):
    return sum(jnp.dot(t, sel, preferred_element_type=f32) for t in _split3(x))


def _shift_rows(x, d, t_idx):
    n = x.shape[0]
    if d == 0:
        return x
    y = pltpu.roll(x, (-d) % n, 0)
    ok = (t_idx + d >= 0) & (t_idx + d < n)
    return jnp.where(ok, y, 0.0)


def _ssd_kernel(z_ref, x_ref, b_ref, c_ref, sm_ref, cwx_ref, cwb_ref, cwc_ref, cbx_ref, cbb_ref, cbc_ref,
                dtb_ref, a_ref, dsk_ref, *rest, seq, has_h0, emit_state):
    rest = list(rest)
    h0_ref = rest.pop(0) if has_h0 else None
    o_ref = rest.pop(0)
    st_ref = rest.pop(0) if emit_state else None
    y_ref, xs_ref, bm_ref, cm_ref = rest
    cs = SSD_CHUNK
    n_chunk = seq // cs
    hp = SSM_HPG * SSM_HEADDIM

    t_idx = lax.broadcasted_iota(jnp.int32, (seq, 1), 0)

    def conv_silu(src_ref, w_ref, bias_ref):
        xin = src_ref[...]
        acc = jnp.zeros_like(xin) + bias_ref[...]
        for j in range(SSM_CONV):
            acc = acc + w_ref[j:j + 1, :] * _shift_rows(xin, j - SSM_CONV // 2, t_idx)
        return _silu(acc)

    xs_ref[...] = conv_silu(x_ref, cwx_ref, cbx_ref)
    bm_ref[...] = conv_silu(b_ref, cwb_ref, cbb_ref)
    cm_ref[...] = conv_silu(c_ref, cwc_ref, cbc_ref)

    row = lax.broadcasted_iota(jnp.int32, (cs, cs), 0)
    col = lax.broadcasted_iota(jnp.int32, (cs, cs), 1)
    masks = (col <= row, col >= row)
    tris = tuple(jnp.where(m, 1.0, 0.0).astype(bf16) for m in masks)
    lane = lax.broadcasted_iota(jnp.int32, (1, SMALL_W), 1)

    dt_lane = (lane >= SMALL_DT) & (lane < SMALL_DT + 2 * SSM_HPG)
    a_row = a_ref[...]
    dt_bias = dtb_ref[...]

    def selectors(d):
        r = lax.broadcasted_iota(jnp.int32, (SMALL_W, SSM_HPG * cs), 0)
        c = lax.broadcasted_iota(jnp.int32, (SMALL_W, SSM_HPG * cs), 1)
        bcast = jnp.where(r == SMALL_DT + d * SSM_HPG + c // cs, 1.0, 0.0).astype(bf16)
        r = lax.broadcasted_iota(jnp.int32, (SMALL_W, hp), 0)
        c = lax.broadcasted_iota(jnp.int32, (SMALL_W, hp), 1)
        expand = jnp.where(r == SMALL_DT + d * SSM_HPG + c // SSM_HEADDIM, 1.0, 0.0).astype(bf16)
        return bcast, expand

    head_of_lane = lax.broadcasted_iota(jnp.int32, (cs, hp), 1) // SSM_HEADDIM
    bm_t = [bm_ref[c * cs:(c + 1) * cs, :].T.astype(bf16) for c in range(n_chunk)]

    y_ref[...] = jnp.zeros_like(y_ref)
    states = []
    for d in range(2):
        bcast, expand = selectors(d)
        if has_h0:
            st = h0_ref[d].reshape(hp, SSM_STATE).T
        else:
            st = jnp.zeros((SSM_STATE, hp), f32)
        order = range(n_chunk) if d == 0 else range(n_chunk - 1, -1, -1)
        for cidx in order:
            rows = slice(cidx * cs, (cidx + 1) * cs)
            dt = jnp.where(dt_lane, _softplus(sm_ref[rows, :] + dt_bias), 0.0)
            cum = _split3_dot(tris[d], dt * a_row)
            cum_t = cum.T
            cum_b = _dot_sel(cum, bcast)
            cum_e = _dot_sel(cum, expand)
            dt_e = _dot_sel(dt, expand)
            tot_e = cum_e[0:1, :] if d == 1 else cum_e[cs - 1:cs, :]
            xs = xs_ref[rows, :]
            cm = cm_ref[rows, :].astype(bf16)
            cb = lax.dot_general(cm, bm_ref[rows, :].astype(bf16), NT, preferred_element_type=f32)
            w_parts = []
            for j in range(SSM_HPG):
                ln = SMALL_DT + d * SSM_HPG + j
                seg = jnp.exp(jnp.where(masks[d], cum_b[:, j * cs:(j + 1) * cs] - cum_t[ln:ln + 1, :], -jnp.inf))
                w_parts.append((cb * seg).astype(bf16))
            w = jnp.concatenate(w_parts, axis=1)
            xd = xs * dt_e
            xd_bd = jnp.concatenate([jnp.where(head_of_lane == j, xd, 0.0).astype(bf16)
                                     for j in range(SSM_HPG)], axis=0)
            inter = jnp.dot(cm, st.astype(bf16), preferred_element_type=f32)
            y_ref[rows, :] += jnp.dot(w, xd_bd, preferred_element_type=f32) + jnp.exp(cum_e) * inter
            xw = (xd * jnp.exp(tot_e - cum_e)).astype(bf16)
            st = st * jnp.exp(tot_e) + jnp.dot(bm_t[cidx], xw, preferred_element_type=f32)
        states.append(st)

    y = y_ref[...] + dsk_ref[...] * xs_ref[...]
    o_ref[...] = y * _silu(z_ref[...])
    if emit_state:
        st_ref[0, 0] = states[0].T.reshape(SSM_HPG, SSM_HEADDIM, SSM_STATE)
        st_ref[0, 1] = states[1].T.reshape(SSM_HPG, SSM_HEADDIM, SSM_STATE)


def _ssd(proj, row_blk0, n_seq, seq, conv_w, conv_b, dt_bias_row, a_row, d_skip, h0):
    has_h0 = h0 is not None
    emit_state = not has_h0
    hp = SSM_HPG * SSM_HEADDIM
    ns = SSM_STATE
    in_specs = [
        pl.BlockSpec((seq, hp), lambda b, g: (row_blk0 + b, COL_SZ // hp + g)),
        pl.BlockSpec((seq, hp), lambda b, g: (row_blk0 + b, COL_SX // hp + g)),
        pl.BlockSpec((seq, ns), lambda b, g: (row_blk0 + b, COL_SB // ns + g)),
        pl.BlockSpec((seq, ns), lambda b, g: (row_blk0 + b, COL_SC // ns + g)),
        pl.BlockSpec((seq, SMALL_W), lambda b, g: (row_blk0 + b, COL_SMALL // SMALL_W + g)),
        pl.BlockSpec((8, hp), lambda b, g: (0, g)),
        pl.BlockSpec((8, ns), lambda b, g: (0, SSM_INNER // ns + g)),
        pl.BlockSpec((8, ns), lambda b, g: (0, (SSM_INNER + SSM_BC) // ns + g)),
        pl.BlockSpec((1, hp), lambda b, g: (0, g)),
        pl.BlockSpec((1, ns), lambda b, g: (0, SSM_INNER // ns + g)),
        pl.BlockSpec((1, ns), lambda b, g: (0, (SSM_INNER + SSM_BC) // ns + g)),
        pl.BlockSpec((1, SMALL_W), lambda b, g: (0, g)),
        pl.BlockSpec((1, SMALL_W), lambda b, g: (0, g)),
        pl.BlockSpec((1, hp), lambda b, g: (0, g)),
    ]
    args = [proj, proj, proj, proj, proj, conv_w, conv_w, conv_w, conv_b, conv_b, conv_b,
            dt_bias_row, a_row, d_skip]
    if has_h0:
        in_specs.append(pl.BlockSpec((None, 2, SSM_HPG, SSM_HEADDIM, ns), lambda b, g: (b, 0, g, 0, 0)))
        args.append(h0)
    out_specs = [pl.BlockSpec((seq, hp), lambda b, g: (b, g))]
    out_shape = [jax.ShapeDtypeStruct((n_seq * seq, SSM_INNER), f32)]
    if emit_state:
        out_specs.append(pl.BlockSpec((1, 2, SSM_HPG, SSM_HEADDIM, ns), lambda b, g: (b, 0, g, 0, 0)))
        out_shape.append(jax.ShapeDtypeStruct((n_seq, 2, SSM_HEADS, SSM_HEADDIM, ns), f32))
    res = pl.pallas_call(
        functools.partial(_ssd_kernel, seq=seq, has_h0=has_h0, emit_state=emit_state),
        grid=(n_seq, SSM_GROUPS),
        in_specs=in_specs,
        out_specs=out_specs,
        out_shape=out_shape,
        scratch_shapes=[
            pltpu.VMEM((seq, hp), f32),
            pltpu.VMEM((seq, hp), f32),
            pltpu.VMEM((seq, ns), f32),
            pltpu.VMEM((seq, ns), f32),
        ],
        compiler_params=_cparams(("parallel", "arbitrary")),
        name="ssd_lat" if has_h0 else "ssd_ctx",
    )(*args)
    return res if emit_state else (res[0], None)


def _attn_ctx_kernel(q_ref, k_ref, v_ref, sink_ref, o_ref):
    k = k_ref[...].astype(bf16)
    v = v_ref[...].astype(bf16)
    kvh = pl.program_id(1)
    for g in range(Q_PER_KV):
        q = q_ref[:, g * HEAD_DIM:(g + 1) * HEAD_DIM].astype(bf16)
        s = lax.dot_general(q, k, NT, preferred_element_type=f32) * (HEAD_DIM ** -0.5)
        sink = sink_ref[pl.ds(kvh * Q_PER_KV + g, 1), 0:1]
        m = jnp.maximum(jnp.max(s, axis=-1, keepdims=True), sink)
        p = jnp.exp(s - m)
        den = jnp.sum(p, axis=-1, keepdims=True) + jnp.exp(sink - m)
        o = jnp.dot(p.astype(bf16), v, preferred_element_type=f32) / den
        o_ref[:, g * HEAD_DIM:(g + 1) * HEAD_DIM] = o.astype(o_ref.dtype)


def _attn_ctx(proj, sink8, n_seq=BATCH):
    qw = Q_PER_KV * HEAD_DIM
    return pl.pallas_call(
        _attn_ctx_kernel,
        grid=(n_seq, KV_HEADS),
        in_specs=[
            pl.BlockSpec((SEQ, qw), lambda b, h: (b, COL_AQ // qw + h)),
            pl.BlockSpec((SEQ, HEAD_DIM), lambda b, h: (b, COL_AK // HEAD_DIM + h)),
            pl.BlockSpec((SEQ, HEAD_DIM), lambda b, h: (b, COL_AV // HEAD_DIM + h)),
            pl.BlockSpec((ATTN_HEADS, 128), lambda b, h: (0, 0)),
        ],
        out_specs=pl.BlockSpec((SEQ, qw), lambda b, h: (b, h)),
        out_shape=jax.ShapeDtypeStruct((n_seq * SEQ, ATTN_Q), bf16),
        compiler_params=_cparams(("parallel", "arbitrary")),
        name="attn_ctx",
    )(proj, proj, proj, sink8)


def _rope(x, cos, sin_signed):
    quarter = HEAD_DIM // 4
    lane = lax.broadcasted_iota(jnp.int32, x.shape, 1)
    first = (lane % (2 * quarter)) < quarter
    partner = jnp.where(first, pltpu.roll(x, HEAD_DIM - quarter, 1), pltpu.roll(x, quarter, 1))
    return x * cos + partner * sin_signed


def _attn_lat_kernel(q_ref, k_ref, v_ref, kc_ref, vc_ref, cos_ref, sin_ref, sink_ref, o_ref, kr_ref):
    kvh = pl.program_id(1)
    cos = cos_ref[...]
    sin = sin_ref[...]
    kr_ref[...] = _rope(k_ref[...], cos, sin).astype(bf16)
    kc = kc_ref[...].astype(bf16)
    vc = vc_ref[...].astype(bf16)
    blk = WINDOW
    n_blk = DEC_SEQ // blk
    scale = HEAD_DIM ** -0.5
    for i in range(n_blk):
        lo = max(i - 1, 0) * blk
        hi = min(i + 2, n_blk) * blk
        kw = kr_ref[lo:hi, :]
        vw = v_ref[lo:hi, :].astype(bf16)
        qpos = i * blk + lax.broadcasted_iota(jnp.int32, (blk, hi - lo), 0)
        kpos = lo + lax.broadcasted_iota(jnp.int32, (blk, hi - lo), 1)
        win = jnp.abs(qpos - kpos) <= WINDOW
        rows = slice(i * blk, (i + 1) * blk)
        for g in range(Q_PER_KV):
            cols = slice(g * HEAD_DIM, (g + 1) * HEAD_DIM)
            q = _rope(q_ref[rows, cols], cos[rows, :], sin[rows, :]).astype(bf16)
            s_c = lax.dot_general(q, kc, NT, preferred_element_type=f32) * scale
            s_w = lax.dot_general(q, kw, NT, preferred_element_type=f32) * scale
            s_w = jnp.where(win, s_w, -jnp.inf)
            sink = sink_ref[pl.ds(kvh * Q_PER_KV + g, 1), 0:1]
            m = jnp.maximum(jnp.maximum(jnp.max(s_c, axis=-1, keepdims=True),
                                        jnp.max(s_w, axis=-1, keepdims=True)), sink)
            p_c = jnp.exp(s_c - m)
            p_w = jnp.exp(s_w - m)
            den = (jnp.sum(p_c, axis=-1, keepdims=True) + jnp.sum(p_w, axis=-1, keepdims=True)
                   + jnp.exp(sink - m))
            o = (jnp.dot(p_c.astype(bf16), vc, preferred_element_type=f32)
                 + jnp.dot(p_w.astype(bf16), vw, preferred_element_type=f32)) / den
            o_ref[rows, cols] = o.astype(o_ref.dtype)


def _attn_lat(proj, cache_k, cache_v, l, cos, sin, sink8, rb0=N_PROMPT // DEC_SEQ, n_seq=DEC_BATCH):
    qw = Q_PER_KV * HEAD_DIM
    ck = cache_k.reshape(n_seq, DEPTH, PAST_LEN, ATTN_KV)
    cv = cache_v.reshape(n_seq, DEPTH, PAST_LEN, ATTN_KV)
    return pl.pallas_call(
        _attn_lat_kernel,
        grid=(n_seq, KV_HEADS),
        in_specs=[
            pl.BlockSpec((DEC_SEQ, qw), lambda b, h: (rb0 + b, COL_AQ // qw + h)),
            pl.BlockSpec((DEC_SEQ, HEAD_DIM), lambda b, h: (rb0 + b, COL_AK // HEAD_DIM + h)),
            pl.BlockSpec((DEC_SEQ, HEAD_DIM), lambda b, h: (rb0 + b, COL_AV // HEAD_DIM + h)),
            pl.BlockSpec((None, None, PAST_LEN, HEAD_DIM), lambda b, h: (b, l, 0, h)),
            pl.BlockSpec((None, None, PAST_LEN, HEAD_DIM), lambda b, h: (b, l, 0, h)),
            pl.BlockSpec((DEC_SEQ, HEAD_DIM), lambda b, h: (0, 0)),
            pl.BlockSpec((DEC_SEQ, HEAD_DIM), lambda b, h: (0, 0)),
            pl.BlockSpec((ATTN_HEADS, 128), lambda b, h: (0, 0)),
        ],
        out_specs=pl.BlockSpec((DEC_SEQ, qw), lambda b, h: (b, h)),
        out_shape=jax.ShapeDtypeStruct((n_seq * DEC_SEQ, ATTN_Q), bf16),
        scratch_shapes=[pltpu.VMEM((DEC_SEQ, HEAD_DIM), bf16)],
        compiler_params=_cparams(("parallel", "arbitrary")),
        name="attn_lat",
    )(proj, proj, proj, ck, cv, cos, sin, sink8)


def _rope_tables():
    quarter = HEAD_DIM // 4
    freqs = ROPE_THETA ** (-np.arange(quarter, dtype=np.float32) / quarter)
    t = np.arange(DEC_SEQ)
    cos = np.zeros((DEC_SEQ, HEAD_DIM), np.float32)
    sin = np.zeros((DEC_SEQ, HEAD_DIM), np.float32)
    for half, pos in enumerate((t // GRID_W, t % GRID_W)):
        ang = pos.astype(np.float32)[:, None] * freqs[None, :]
        base = half * 2 * quarter
        cos[:, base:base + quarter] = np.cos(ang)
        cos[:, base + quarter:base + 2 * quarter] = np.cos(ang)
        sin[:, base:base + quarter] = -np.sin(ang)
        sin[:, base + quarter:base + 2 * quarter] = np.sin(ang)
    return jnp.asarray(cos), jnp.asarray(sin)


def _merge_kernel(x_ref, gt_ref, ogc_ref, ogl_ref, osc_ref, osl_ref, oac_ref, oal_ref, b0_ref, b1_ref, b2_ref,
                  sn_ref, wg_ref, ws_ref, wa_ref, wo_ref, o_ref, osn_ref):
    i = pl.program_id(0)
    j = pl.program_id(1)

    def step(og_ref, os_ref, oa_ref):
        @pl.when(j == 0)
        def _():
            y = os_ref[...]
            ms = jnp.mean(y * y, axis=-1, keepdims=True)
            osn_ref[...] = (y * lax.rsqrt(ms + RMS_EPS) * sn_ref[...]).astype(bf16)
            o_ref[...] = jnp.zeros_like(o_ref)

        m = (jax.nn.sigmoid(b0_ref[...]) * jnp.dot(og_ref[...], wg_ref[...], preferred_element_type=f32)
             + jax.nn.sigmoid(b1_ref[...]) * jnp.dot(osn_ref[...], ws_ref[...], preferred_element_type=f32)
             + jax.nn.sigmoid(b2_ref[...]) * jnp.dot(oa_ref[...], wa_ref[...], preferred_element_type=f32))
        o_ref[...] += jnp.dot(m.astype(bf16), wo_ref[...], preferred_element_type=f32)

    @pl.when(i < N_PROMPT // TM)
    def _():
        step(ogc_ref, osc_ref, oac_ref)

    @pl.when(i >= N_PROMPT // TM)
    def _():
        step(ogl_ref, osl_ref, oal_ref)

    @pl.when(j == pl.num_programs(1) - 1)
    def _():
        o_ref[...] = x_ref[...] + gt_ref[0] * o_ref[...]


def _merge(x, mod3, l, proj, o_gla, o_ssm, o_att, ssm_norm, w_g, w_s, w_a, w_o):
    n_tok = N_TOK
    nj = D_MODEL // TJ

    def pair_specs(width):
        return [pl.BlockSpec((TM, width), lambda i, j: (_ctx_blk(i, TM), 0)),
                pl.BlockSpec((TM, width), lambda i, j: (_lat_blk(i, TM), 0))]

    def br_spec(k):
        return pl.BlockSpec((TM, TJ), lambda i, j: (i, (COL_BR + k * D_MODEL) // TJ + j))

    def wbr_spec():
        return pl.BlockSpec((None, GLA_VAL, TJ), lambda i, j: (l, 0, j))

    return pl.pallas_call(
        _merge_kernel,
        grid=(n_tok // TM, nj),
        in_specs=[
            pl.BlockSpec((TM, D_MODEL), lambda i, j: (i, 0)),
            pl.BlockSpec((1, 1, D_MODEL), lambda i, j: (_mod_row(i, TM), 0, 5)),
            *pair_specs(GLA_VAL), *pair_specs(SSM_INNER), *pair_specs(ATTN_Q),
            br_spec(0), br_spec(1), br_spec(2),
            pl.BlockSpec((None, 1, SSM_INNER), lambda i, j: (l, 0, 0)),
            wbr_spec(), wbr_spec(), wbr_spec(),
            pl.BlockSpec((None, TJ, D_MODEL), lambda i, j: (l, j, 0)),
        ],
        out_specs=pl.BlockSpec((TM, D_MODEL), lambda i, j: (i, 0)),
        out_shape=jax.ShapeDtypeStruct((n_tok, D_MODEL), f32),
        scratch_shapes=[pltpu.VMEM((TM, SSM_INNER), bf16)],
        compiler_params=_cparams(("arbitrary", "arbitrary")),
        name="merge",
    )(x, mod3, *o_gla, *o_ssm, *o_att, proj, proj, proj, ssm_norm, w_g, w_s, w_a, w_o)


def _permute_w_in(w_in):
    o = np.cumsum((0, GLA_QK, GLA_QK, GLA_VAL, GLA_VAL, 2 * GLA_RANK, SSM_INNER, CONV_CH, 2 * SSM_HEADS,
                   ATTN_Q, ATTN_KV, ATTN_KV, 3 * D_MODEL))
    seg = lambda a: w_in[:, :, int(o[a]):int(o[a + 1])]
    zeros = lambda n: jnp.zeros((DEPTH, D_MODEL, n), w_in.dtype)
    parts = [seg(0), seg(1), seg(2), seg(3), seg(5), seg(6), seg(8), seg(9), seg(10)]
    dt0 = int(o[7])
    for g in range(SSM_GROUPS):
        parts.append(seg(4))
        for d in range(2):
            lo = dt0 + d * SSM_HEADS + g * SSM_HPG
            parts.append(w_in[:, :, lo:lo + SSM_HPG])
        parts.append(zeros(SMALL_W - 2 * GLA_RANK - 2 * SSM_HPG))
    parts.append(seg(11))
    return jnp.concatenate(parts, axis=-1).astype(bf16)


def kernel(x_prompt, x_sample, c, cache_k, cache_v, state_gla, state_ssm, c_ctx, w_mod, b_mod, ffn1_norm,
           ffn1_w_gate, ffn1_w_up, ffn1_w_down, mix_norm, w_in, gla_w_up, gla_b_up, gla_norm, ssm_conv_w,
           ssm_conv_b, ssm_dt_bias, ssm_a_log, ssm_d, ssm_norm, attn_sink, w_br_gla, w_br_ssm, w_br_attn,
           w_out, ffn2_norm, ffn2_w_gate, ffn2_w_up, ffn2_w_down, final_norm):
    x = (x_prompt.reshape(N_PROMPT, D_MODEL), x_sample.reshape(N_SAMPLE, D_MODEL))
    cvec = jnp.concatenate([c_ctx[None], c, jnp.zeros((MOD_ROWS - 1 - DEC_BATCH, D_MODEL), f32)], axis=0)
    mod = _modulation(cvec, w_mod, b_mod)

    w_in_p = _permute_w_in(w_in)
    f1g, f1u, f1d = ffn1_w_gate.astype(bf16), ffn1_w_up.astype(bf16), ffn1_w_down.astype(bf16)
    f2g, f2u, f2d = ffn2_w_gate.astype(bf16), ffn2_w_up.astype(bf16), ffn2_w_down.astype(bf16)
    wbg, wbs, wba, wo = (w_br_gla.astype(bf16), w_br_ssm.astype(bf16), w_br_attn.astype(bf16),
                         w_out.astype(bf16))

    wz = jnp.zeros((DEPTH, 2, SMALL_W, GLA_QK), f32)
    for d in range(2):
        wz = wz.at[:, d, d * GLA_RANK:(d + 1) * GLA_RANK, :].set(gla_w_up[:, d])
    wz = wz.astype(bf16)
    bz = gla_b_up.reshape(DEPTH, 2, 1, GLA_QK)

    def dt_lanes(v):
        vg = v.reshape(DEPTH, 2, SSM_GROUPS, SSM_HPG).transpose(0, 2, 1, 3).reshape(DEPTH, SSM_GROUPS, 2 * SSM_HPG)
        row = jnp.zeros((DEPTH, SSM_GROUPS, SMALL_W), f32)
        row = row.at[:, :, SMALL_DT:SMALL_DT + 2 * SSM_HPG].set(vg)
        return row.reshape(DEPTH, 1, SSM_GROUPS * SMALL_W)

    dtb_rows = dt_lanes(ssm_dt_bias)
    a_rows = dt_lanes(-jnp.exp(ssm_a_log))
    d_skip = jnp.repeat(ssm_d, SSM_HEADDIM, axis=-1).reshape(DEPTH, 1, SSM_INNER)
    conv_w = jnp.concatenate([ssm_conv_w, jnp.zeros((DEPTH, 8 - SSM_CONV, CONV_CH), f32)], axis=1)
    conv_b = ssm_conv_b.reshape(DEPTH, 1, CONV_CH)
    sink8 = jnp.broadcast_to(attn_sink[:, :, None], (DEPTH, ATTN_HEADS, 128))
    cos, sin = _rope_tables()

    n1, nm, n2 = (w.reshape(DEPTH, 1, D_MODEL) for w in (ffn1_norm, mix_norm, ffn2_norm))
    sn = ssm_norm.reshape(DEPTH, 1, SSM_INNER)

    new_k, new_v, new_gla, new_ssm = [], [], [], []
    for l in range(DEPTH):
        mod3 = mod[l].reshape(MOD_ROWS, 1, N_MOD * D_MODEL)
        x = _ffn(x, mod3, l, 0, n1, f1g, f1u, f1d)
        proj = _inproj(x, mod3, l, nm, w_in_p)

        og_c, st_g = _gla(proj, 0, BATCH, SEQ, wz[l], bz[l], gla_norm[l][None], None)
        og_s, _ = _gla(proj, N_PROMPT // DEC_SEQ, DEC_BATCH, DEC_SEQ, wz[l], bz[l], gla_norm[l][None],
                       state_gla[:, l])
        os_c, st_s = _ssd(proj, 0, BATCH, SEQ, conv_w[l], conv_b[l], dtb_rows[l], a_rows[l], d_skip[l], None)
        os_s, _ = _ssd(proj, N_PROMPT // DEC_SEQ, DEC_BATCH, DEC_SEQ, conv_w[l], conv_b[l], dtb_rows[l],
                       a_rows[l], d_skip[l], state_ssm[:, l])
        oa_c = _attn_ctx(proj, sink8[l])
        oa_s = _attn_lat(proj, cache_k, cache_v, l, cos, sin, sink8[l])

        x = _merge(x, mod3, l, proj, (og_c, og_s), (os_c, os_s), (oa_c, oa_s), sn, wbg, wbs, wba, wo)
        x = _ffn(x, mod3, l, 2, n2, f2g, f2u, f2d, final_w=final_norm if l == DEPTH - 1 else None)

        new_k.append(proj[:N_PROMPT, COL_AK:COL_AK + ATTN_KV].reshape(BATCH, SEQ, KV_HEADS, HEAD_DIM))
        new_v.append(proj[:N_PROMPT, COL_AV:COL_AV + ATTN_KV].reshape(BATCH, SEQ, KV_HEADS, HEAD_DIM))
        new_gla.append(st_g)
        new_ssm.append(st_s)

    y_prompt = x[0].reshape(BATCH, SEQ, D_MODEL)
    y_sample = x[1].reshape(DEC_BATCH, DEC_SEQ, D_MODEL)
    return (y_prompt, y_sample, jnp.stack(new_k, axis=1), jnp.stack(new_v, axis=1),
            jnp.stack(new_gla, axis=1), jnp.stack(new_ssm, axis=1))
```

```python
import functools
import math

import numpy as np
import jax
import jax.numpy as jnp
from jax import lax
from jax.experimental import pallas as pl
from jax.experimental.pallas import tpu as pltpu

f32 = jnp.float32
bf16 = jnp.bfloat16

D_MODEL = 2048
BATCH = 32
SEQ = 256
DEPTH = 2
DEC_BATCH = 2
DEC_SEQ = 1024
PAST_LEN = 512
GRID_W = 64
RMS_EPS = 1e-6
N_MOD = 9
D_FF = 5632
GLA_HEADS = 4
GLA_DK = 128
GLA_DV = 256
GLA_RANK = 16
GLA_GATE_NORM = 16.0
GLA_CHUNK = 16
SSM_HEADS = 16
SSM_HEADDIM = 64
SSM_GROUPS = 4
SSM_HPG = SSM_HEADS // SSM_GROUPS
SSM_STATE = 128
SSM_CONV = 5
SSM_INNER = SSM_HEADS * SSM_HEADDIM
SSM_BC = SSM_GROUPS * SSM_STATE
CONV_CH = SSM_INNER + 2 * SSM_BC
ATTN_HEADS = 8
KV_HEADS = 2
Q_PER_KV = ATTN_HEADS // KV_HEADS
HEAD_DIM = 128
WINDOW = 128
ROPE_THETA = 10000.0
GLA_QK = GLA_HEADS * GLA_DK
GLA_VAL = GLA_HEADS * GLA_DV
ATTN_Q = ATTN_HEADS * HEAD_DIM
ATTN_KV = KV_HEADS * HEAD_DIM

N_PROMPT = BATCH * SEQ
N_SAMPLE = DEC_BATCH * DEC_SEQ
N_TOK = N_PROMPT + N_SAMPLE
MOD_ROWS = 8

COL_GQ = 0
COL_GK = COL_GQ + GLA_QK
COL_GV = COL_GK + GLA_QK
COL_GR = COL_GV + GLA_VAL
COL_SZ = COL_GR + GLA_VAL
COL_SX = COL_SZ + SSM_INNER
COL_SB = COL_SX + SSM_INNER
COL_SC = COL_SB + SSM_BC
COL_AQ = COL_SC + SSM_BC
COL_AK = COL_AQ + ATTN_Q
COL_AV = COL_AK + ATTN_KV
COL_SMALL = COL_AV + ATTN_KV
SMALL_W = 128
SMALL_DT = 2 * GLA_RANK
COL_BR = 8192
D_INP = COL_BR + 3 * D_MODEL

TM = 512
TF = 512
TN_IN = 1024
AUX_TILE = COL_AK // TN_IN
AUX_SMALL = (COL_SMALL - COL_AK) // SMALL_W
TN_MOD = 1024
TJ = 512
GLA_SB = 256
GLA_TILE = 128
SSD_CHUNK = 128
VMEM_LIMIT = 56 * 1024 * 1024

NT = (((1,), (1,)), ((), ()))
TN = (((0,), (0,)), ((), ()))


def _silu(x):
    return x * jax.nn.sigmoid(x)


def _mod_row(i, tm):
    n_p = N_PROMPT // tm
    per = DEC_SEQ // tm
    return jnp.where(i < n_p, 0, 1 + (i - n_p) // per)


def _cparams(sem):
    return pltpu.CompilerParams(dimension_semantics=sem, vmem_limit_bytes=VMEM_LIMIT)


def _mod_kernel(c_ref, w_ref, b_ref, o_ref):
    s = _silu(c_ref[...]).astype(bf16)
    o_ref[0] = jnp.dot(s, w_ref[0].astype(bf16), preferred_element_type=f32) + b_ref[0]


def _modulation(cvec, w_mod, b_mod):
    n = N_MOD * D_MODEL
    return pl.pallas_call(
        _mod_kernel,
        grid=(DEPTH, n // TN_MOD),
        in_specs=[
            pl.BlockSpec((MOD_ROWS, D_MODEL), lambda l, j: (0, 0)),
            pl.BlockSpec((1, D_MODEL, TN_MOD), lambda l, j: (l, 0, j)),
            pl.BlockSpec((1, 1, TN_MOD), lambda l, j: (l, 0, j)),
        ],
        out_specs=pl.BlockSpec((1, MOD_ROWS, TN_MOD), lambda l, j: (l, 0, j)),
        out_shape=jax.ShapeDtypeStruct((DEPTH, MOD_ROWS, n), f32),
        compiler_params=_cparams(("arbitrary", "arbitrary")),
        name="modulation",
    )(cvec, w_mod, b_mod.reshape(DEPTH, 1, n))


def _norm_mod(x, nw, sh, sc):
    ms = jnp.mean(x * x, axis=-1, keepdims=True)
    h = x * lax.rsqrt(ms + RMS_EPS) * nw
    return h * (1.0 + sc) + sh


def _pick_stream(i, tm, ctx_val, lat_val):
    return jnp.where(i < N_PROMPT // tm, ctx_val, lat_val)


def _ctx_blk(i, tm):
    return jnp.minimum(i, N_PROMPT // tm - 1)


def _lat_blk(i, tm):
    return jnp.maximum(i - N_PROMPT // tm, 0)


def _ffn_kernel(*refs, split_in, final):
    refs = list(refs)
    x_refs = [refs.pop(0) for _ in range(2 if split_in else 1)]
    sh_ref, sc_ref, gt_ref, nw_ref, wg_ref, wu_ref, wd_ref = refs[:7]
    refs = refs[7:]
    fw_ref = refs.pop(0) if final else None
    o_refs = [refs.pop(0) for _ in range(2 if final else 1)]
    h_ref, acc_ref = refs
    i = pl.program_id(0)
    j = pl.program_id(1)

    def read_x():
        if split_in:
            return _pick_stream(i, TM, x_refs[0][...], x_refs[1][...])
        return x_refs[0][...]

    @pl.when(j == 0)
    def _():
        h_ref[...] = _norm_mod(read_x(), nw_ref[...], sh_ref[0], sc_ref[0]).astype(bf16)
        acc_ref[...] = jnp.zeros_like(acc_ref)

    h = h_ref[...]
    g = jnp.dot(h, wg_ref[...], preferred_element_type=f32)
    u = jnp.dot(h, wu_ref[...], preferred_element_type=f32)
    a = (_silu(g) * u).astype(bf16)
    acc_ref[...] += jnp.dot(a, wd_ref[...], preferred_element_type=f32)

    last = j == pl.num_programs(1) - 1

    def result():
        y = read_x() + 0.5 * gt_ref[0] * acc_ref[...]
        if final:
            ms = jnp.mean(y * y, axis=-1, keepdims=True)
            y = y * lax.rsqrt(ms + RMS_EPS) * fw_ref[...]
        return y

    if final:
        @pl.when(last & (i < N_PROMPT // TM))
        def _():
            o_refs[0][...] = result()

        @pl.when(last & (i >= N_PROMPT // TM))
        def _():
            o_refs[1][...] = result()
    else:
        @pl.when(last)
        def _():
            o_refs[0][...] = result()


def _ffn(x, mod3, l, slot, nw, wg, wu, wd, final_w=None):
    split_in = isinstance(x, tuple)
    final = final_w is not None

    def mod_spec(k):
        return pl.BlockSpec((1, 1, D_MODEL), lambda i, j: (_mod_row(i, TM), 0, 3 * slot + k))

    if split_in:
        in_specs = [pl.BlockSpec((TM, D_MODEL), lambda i, j: (_ctx_blk(i, TM), 0)),
                    pl.BlockSpec((TM, D_MODEL), lambda i, j: (_lat_blk(i, TM), 0))]
        args = list(x)
    else:
        in_specs = [pl.BlockSpec((TM, D_MODEL), lambda i, j: (i, 0))]
        args = [x]
    in_specs += [
        mod_spec(0), mod_spec(1), mod_spec(2),
        pl.BlockSpec((None, 1, D_MODEL), lambda i, j: (l, 0, 0)),
        pl.BlockSpec((None, D_MODEL, TF), lambda i, j: (l, 0, j)),
        pl.BlockSpec((None, D_MODEL, TF), lambda i, j: (l, 0, j)),
        pl.BlockSpec((None, TF, D_MODEL), lambda i, j: (l, j, 0)),
    ]
    args += [mod3, mod3, mod3, nw, wg, wu, wd]
    if final:
        in_specs.append(pl.BlockSpec((1, D_MODEL), lambda i, j: (0, 0)))
        args.append(final_w.reshape(1, D_MODEL))
        out_specs = [pl.BlockSpec((TM, D_MODEL), lambda i, j: (_ctx_blk(i, TM), 0)),
                     pl.BlockSpec((TM, D_MODEL), lambda i, j: (_lat_blk(i, TM), 0))]
        out_shape = [jax.ShapeDtypeStruct((N_PROMPT, D_MODEL), f32), jax.ShapeDtypeStruct((N_SAMPLE, D_MODEL), f32)]
    else:
        out_specs = pl.BlockSpec((TM, D_MODEL), lambda i, j: (i, 0))
        out_shape = jax.ShapeDtypeStruct((N_TOK, D_MODEL), f32)
    return pl.pallas_call(
        functools.partial(_ffn_kernel, split_in=split_in, final=final),
        grid=(N_TOK // TM, D_FF // TF),
        in_specs=in_specs,
        out_specs=out_specs,
        out_shape=out_shape,
        scratch_shapes=[pltpu.VMEM((TM, D_MODEL), bf16), pltpu.VMEM((TM, D_MODEL), f32)],
        compiler_params=_cparams(("arbitrary", "arbitrary")),
        name="ffn_final" if final else ("ffn_first" if split_in else "ffn"),
    )(*args)


def _inproj_kernel(x_ref, sh_ref, sc_ref, nw_ref, w_ref, o_ref, os_ref, h_ref):
    j = pl.program_id(1)

    @pl.when(j == 0)
    def _():
        h_ref[...] = _norm_mod(x_ref[...], nw_ref[...], sh_ref[0], sc_ref[0]).astype(bf16)

    acc = jnp.dot(h_ref[...], w_ref[...], preferred_element_type=f32)

    @pl.when(j < COL_BR // TN_IN)
    def _():
        o_ref[...] = acc.astype(bf16)

    @pl.when(j >= COL_BR // TN_IN)
    def _():
        o_ref[...] = jax.nn.sigmoid(acc).astype(bf16)

    @pl.when(j == AUX_TILE)
    def _():
        os_ref[...] = acc


def _inproj(x, mod3, l, nw, w):
    n_tok = N_TOK

    def mod_spec(k):
        return pl.BlockSpec((1, 1, D_MODEL), lambda i, j: (_mod_row(i, TM), 0, 3 + k))

    return pl.pallas_call(
        _inproj_kernel,
        grid=(n_tok // TM, D_INP // TN_IN),
        in_specs=[
            pl.BlockSpec((TM, D_MODEL), lambda i, j: (i, 0)),
            mod_spec(0), mod_spec(1),
            pl.BlockSpec((None, 1, D_MODEL), lambda i, j: (l, 0, 0)),
            pl.BlockSpec((None, D_MODEL, TN_IN), lambda i, j: (l, 0, j)),
        ],
        out_specs=[pl.BlockSpec((TM, TN_IN), lambda i, j: (i, j)),
                   pl.BlockSpec((TM, TN_IN), lambda i, j: (i, 0))],
        out_shape=[jax.ShapeDtypeStruct((n_tok, D_INP), bf16), jax.ShapeDtypeStruct((n_tok, TN_IN), f32)],
        scratch_shapes=[pltpu.VMEM((TM, D_MODEL), bf16)],
        compiler_params=_cparams(("arbitrary", "arbitrary")),
        name="inproj",
    )(x, mod3, mod3, nw, w)


def _split_dot(m, x):
    hi = x.astype(bf16)
    lo = (x - hi.astype(f32)).astype(bf16)
    return (jnp.dot(m, hi, preferred_element_type=f32) + jnp.dot(m, lo, preferred_element_type=f32))


def _log_sigmoid(x):
    return jnp.minimum(x, 0.0) - jnp.log1p(jnp.exp(-jnp.abs(x)))


def _gla_kernel(q_ref, k_ref, v_ref, r_ref, sm_ref, wz_ref, bz_ref, gn_ref, *rest, seq, has_h0, emit_state):
    rest = list(rest)
    h0_ref = rest.pop(0) if has_h0 else None
    o_ref = rest.pop(0)
    st_ref = rest.pop(0) if emit_state else None
    oin_ref, qe_ref, ke_ref, dec_ref, vt_ref = rest
    sb = GLA_SB
    tile = GLA_TILE
    per_tile = tile // GLA_CHUNK
    n_tile = seq // tile

    row = lax.broadcasted_iota(jnp.int32, (sb, sb), 0)
    col = lax.broadcasted_iota(jnp.int32, (sb, sb), 1)
    same = (row // GLA_CHUNK) == (col // GLA_CHUNK)
    masks = (same & (col <= row), same & (col >= row))
    tris = tuple(jnp.where(m, 1.0, 0.0).astype(bf16) for m in masks)
    blk = jnp.where(same, 1.0, 0.0).astype(bf16)

    for s in range(seq // sb):
        rows = slice(s * sb, (s + 1) * sb)
        sm = sm_ref[rows, :].astype(bf16)
        q = q_ref[rows, :].astype(f32) * (GLA_DK ** -0.5)
        k = k_ref[rows, :].astype(f32)
        v = v_ref[rows, :]
        v_t = v.astype(f32).T.astype(bf16)
        for t in range(sb // tile):
            vt_ref[s * (sb // tile) + t] = v_t[:, t * tile:(t + 1) * tile]
        for d in range(2):
            gz = jnp.dot(sm, wz_ref[d], preferred_element_type=f32) + bz_ref[d]
            la = _log_sigmoid(gz) * (1.0 / GLA_GATE_NORM)
            cum = _split_dot(tris[d], la)
            tot = _split_dot(blk, la)
            qe = q * jnp.exp(cum)
            kinv = k * jnp.exp(-cum)
            kend = k * jnp.exp(tot - cum)
            a = lax.dot_general(qe.astype(bf16), kinv.astype(bf16), NT, preferred_element_type=f32)
            a = jnp.where(masks[d], a, 0.0).astype(bf16)
            oin_ref[d, rows, :] = jnp.dot(a, v, preferred_element_type=f32)
            qe_ref[d, rows, :] = qe.astype(bf16)
            ke_ref[d, rows, :] = kend.astype(bf16)
            dec_ref[d, rows, :] = jnp.exp(tot)

    if has_h0:
        s_init = (h0_ref[0].T, h0_ref[1].T)
    else:
        s_init = (jnp.zeros((GLA_DV, GLA_DK), f32),) * 2

    def scan_tile(o, carry):
        sts = list(carry)
        for u in range(per_tile):
            for d in range(2):
                t_idx = o if d == 0 else n_tile - 1 - o
                pos = u if d == 0 else per_tile - 1 - u
                i0 = t_idx * tile + pos * GLA_CHUNK
                if not isinstance(i0, int):
                    i0 = pl.multiple_of(i0, GLA_CHUNK)
                rows = pl.ds(i0, GLA_CHUNK)
                inter = lax.dot_general(qe_ref[d, rows, :], sts[d].astype(bf16), NT, preferred_element_type=f32)
                oin_ref[d, rows, :] += inter
                pieces = [ke_ref[d, rows, :]]
                if pos:
                    pieces.insert(0, jnp.zeros((pos * GLA_CHUNK, GLA_DK), bf16))
                if pos < per_tile - 1:
                    pieces.append(jnp.zeros(((per_tile - 1 - pos) * GLA_CHUNK, GLA_DK), bf16))
                upd = jnp.dot(vt_ref[t_idx], jnp.concatenate(pieces, axis=0), preferred_element_type=f32)
                sts[d] = sts[d] * dec_ref[d, pl.ds(i0, 1), :] + upd
        return tuple(sts)

    if n_tile <= 2:
        s_fin = s_init
        for o in range(n_tile):
            s_fin = scan_tile(o, s_fin)
    else:
        s_fin = lax.fori_loop(0, n_tile, scan_tile, s_init)

    o = oin_ref[0] + oin_ref[1]
    ms = jnp.mean(o * o, axis=-1, keepdims=True)
    o = o * lax.rsqrt(ms + RMS_EPS) * gn_ref[...]
    o_ref[...] = (o * _silu(r_ref[...].astype(f32))).astype(o_ref.dtype)
    if emit_state:
        st_ref[0, 0] = s_fin[0].T
        st_ref[0, 1] = s_fin[1].T


def _gla(proj, small, row_blk0, n_seq, seq, wz, bz, gn, h0):
    has_h0 = h0 is not None
    emit_state = not has_h0
    kq, kv_ = GLA_DK, GLA_DV
    in_specs = [
        pl.BlockSpec((seq, kq), lambda b, h: (row_blk0 + b, COL_GQ // kq + h)),
        pl.BlockSpec((seq, kq), lambda b, h: (row_blk0 + b, COL_GK // kq + h)),
        pl.BlockSpec((seq, kv_), lambda b, h: (row_blk0 + b, COL_GV // kv_ + h)),
        pl.BlockSpec((seq, kv_), lambda b, h: (row_blk0 + b, COL_GR // kv_ + h)),
        pl.BlockSpec((seq, SMALL_W), lambda b, h: (row_blk0 + b, AUX_SMALL)),
        pl.BlockSpec((2, SMALL_W, kq), lambda b, h: (0, 0, h)),
        pl.BlockSpec((2, 1, kq), lambda b, h: (0, 0, h)),
        pl.BlockSpec((1, kv_), lambda b, h: (0, 0)),
    ]
    args = [proj, proj, proj, proj, small, wz, bz, gn]
    if has_h0:
        in_specs.append(pl.BlockSpec((None, 2, None, kq, kv_), lambda b, h: (b, 0, h, 0, 0)))
        args.append(h0)
    out_specs = [pl.BlockSpec((seq, kv_), lambda b, h: (b, h))]
    out_shape = [jax.ShapeDtypeStruct((n_seq * seq, GLA_VAL), bf16)]
    if emit_state:
        out_specs.append(pl.BlockSpec((1, 2, None, kq, kv_), lambda b, h: (b, 0, h, 0, 0)))
        out_shape.append(jax.ShapeDtypeStruct((n_seq, 2, GLA_HEADS, kq, kv_), f32))
    res = pl.pallas_call(
        functools.partial(_gla_kernel, seq=seq, has_h0=has_h0, emit_state=emit_state),
        grid=(n_seq, GLA_HEADS),
        in_specs=in_specs,
        out_specs=out_specs,
        out_shape=out_shape,
        scratch_shapes=[
            pltpu.VMEM((2, seq, kv_), f32),
            pltpu.VMEM((2, seq, kq), bf16),
            pltpu.VMEM((2, seq, kq), bf16),
            pltpu.VMEM((2, seq, kq), f32),
            pltpu.VMEM((seq // GLA_TILE, kv_, GLA_TILE), bf16),
        ],
        compiler_params=_cparams(("parallel", "arbitrary")),
        name="gla_lat" if has_h0 else "gla_ctx",
    )(*args)
    return res if emit_state else (res[0], None)


def _softplus(x):
    return jnp.maximum(x, 0.0) + jnp.log1p(jnp.exp(-jnp.abs(x)))


def _split3(x):
    hi = x.astype(bf16)
    r1 = x - hi.astype(f32)
    mid = r1.astype(bf16)
    lo = (r1 - mid.astype(f32)).astype(bf16)
    return hi, mid, lo


def _split3_dot(m, x):
    return sum(jnp.dot(m, t, preferred_element_type=f32) for t in _split3(x))


def _dot_sel(x, sel):
    return sum(jnp.dot(t, sel, preferred_element_type=f32) for t in _split3(x))


def _shift_rows(x, d, t_idx):
    n = x.shape[0]
    if d == 0:
        return x
    y = pltpu.roll(x, (-d) % n, 0)
    ok = (t_idx + d >= 0) & (t_idx + d < n)
    return jnp.where(ok, y, 0.0)


def _ssd_kernel(z_ref, x_ref, b_ref, c_ref, sm_ref, cwx_ref, cwb_ref, cwc_ref, cbx_ref, cbb_ref, cbc_ref,
                dtb_ref, a_ref, dsk_ref, *rest, seq, has_h0, emit_state):
    rest = list(rest)
    h0_ref = rest.pop(0) if has_h0 else None
    o_ref = rest.pop(0)
    st_ref = rest.pop(0) if emit_state else None
    y_ref, xs_ref, bm_ref, cm_ref = rest
    cs = SSD_CHUNK
    n_chunk = seq // cs
    hp = SSM_HPG * SSM_HEADDIM

    t_idx = lax.broadcasted_iota(jnp.int32, (seq, 1), 0)

    def conv_silu(src_ref, w_ref, bias_ref):
        xin = src_ref[...].astype(f32)
        acc = jnp.zeros_like(xin) + bias_ref[...]
        for j in range(SSM_CONV):
            acc = acc + w_ref[j:j + 1, :] * _shift_rows(xin, j - SSM_CONV // 2, t_idx)
        return _silu(acc)

    xs_ref[...] = conv_silu(x_ref, cwx_ref, cbx_ref)
    bm_ref[...] = conv_silu(b_ref, cwb_ref, cbb_ref)
    cm_ref[...] = conv_silu(c_ref, cwc_ref, cbc_ref)

    row = lax.broadcasted_iota(jnp.int32, (cs, cs), 0)
    col = lax.broadcasted_iota(jnp.int32, (cs, cs), 1)
    masks = (col <= row, col >= row)
    tris = tuple(jnp.where(m, 1.0, 0.0).astype(bf16) for m in masks)
    lane = lax.broadcasted_iota(jnp.int32, (1, SMALL_W), 1)

    dt_lane = (lane >= SMALL_DT) & (lane < SMALL_DT + 2 * SSM_HPG)
    a_row = a_ref[...]
    dt_bias = dtb_ref[...]

    def selectors(d):
        r = lax.broadcasted_iota(jnp.int32, (SMALL_W, SSM_HPG * cs), 0)
        c = lax.broadcasted_iota(jnp.int32, (SMALL_W, SSM_HPG * cs), 1)
        bcast = jnp.where(r == SMALL_DT + d * SSM_HPG + c // cs, 1.0, 0.0).astype(bf16)
        r = lax.broadcasted_iota(jnp.int32, (SMALL_W, hp), 0)
        c = lax.broadcasted_iota(jnp.int32, (SMALL_W, hp), 1)
        expand = jnp.where(r == SMALL_DT + d * SSM_HPG + c // SSM_HEADDIM, 1.0, 0.0).astype(bf16)
        return bcast, expand

    head_of_lane = lax.broadcasted_iota(jnp.int32, (cs, hp), 1) // SSM_HEADDIM
    bm_t = [bm_ref[c * cs:(c + 1) * cs, :].T.astype(bf16) for c in range(n_chunk)]

    y_ref[...] = jnp.zeros_like(y_ref)
    states = []
    for d in range(2):
        bcast, expand = selectors(d)
        if has_h0:
            st = h0_ref[d].reshape(hp, SSM_STATE).T
        else:
            st = jnp.zeros((SSM_STATE, hp), f32)
        order = range(n_chunk) if d == 0 else range(n_chunk - 1, -1, -1)
        for cidx in order:
            rows = slice(cidx * cs, (cidx + 1) * cs)
            dt = jnp.where(dt_lane, _softplus(sm_ref[rows, :] + dt_bias), 0.0)
            cum = _split3_dot(tris[d], dt * a_row)
            cum_t = cum.T
            cum_b = _dot_sel(cum, bcast)
            cum_e = _dot_sel(cum, expand)
            dt_e = _dot_sel(dt, expand)
            tot_e = cum_e[0:1, :] if d == 1 else cum_e[cs - 1:cs, :]
            xs = xs_ref[rows, :]
            cm = cm_ref[rows, :].astype(bf16)
            cb = lax.dot_general(cm, bm_ref[rows, :].astype(bf16), NT, preferred_element_type=f32)
            w_parts = []
            for j in range(SSM_HPG):
                ln = SMALL_DT + d * SSM_HPG + j
                seg = jnp.exp(jnp.where(masks[d], cum_b[:, j * cs:(j + 1) * cs] - cum_t[ln:ln + 1, :], -jnp.inf))
                w_parts.append((cb * seg).astype(bf16))
            w = jnp.concatenate(w_parts, axis=1)
            xd = xs * dt_e
            xd_bd = jnp.concatenate([jnp.where(head_of_lane == j, xd, 0.0).astype(bf16)
                                     for j in range(SSM_HPG)], axis=0)
            inter = jnp.dot(cm, st.astype(bf16), preferred_element_type=f32)
            y_ref[rows, :] += jnp.dot(w, xd_bd, preferred_element_type=f32) + jnp.exp(cum_e) * inter
            xw = (xd * jnp.exp(tot_e - cum_e)).astype(bf16)
            st = st * jnp.exp(tot_e) + jnp.dot(bm_t[cidx], xw, preferred_element_type=f32)
        states.append(st)

    y = y_ref[...] + dsk_ref[...] * xs_ref[...]
    o_ref[...] = y * _silu(z_ref[...].astype(f32))
    if emit_state:
        st_ref[0, 0] = states[0].T.reshape(SSM_HPG, SSM_HEADDIM, SSM_STATE)
        st_ref[0, 1] = states[1].T.reshape(SSM_HPG, SSM_HEADDIM, SSM_STATE)


def _ssd(proj, small, row_blk0, n_seq, seq, conv_w, conv_b, dt_bias_row, a_row, d_skip, h0):
    has_h0 = h0 is not None
    emit_state = not has_h0
    hp = SSM_HPG * SSM_HEADDIM
    ns = SSM_STATE
    in_specs = [
        pl.BlockSpec((seq, hp), lambda b, g: (row_blk0 + b, COL_SZ // hp + g)),
        pl.BlockSpec((seq, hp), lambda b, g: (row_blk0 + b, COL_SX // hp + g)),
        pl.BlockSpec((seq, ns), lambda b, g: (row_blk0 + b, COL_SB // ns + g)),
        pl.BlockSpec((seq, ns), lambda b, g: (row_blk0 + b, COL_SC // ns + g)),
        pl.BlockSpec((seq, SMALL_W), lambda b, g: (row_blk0 + b, AUX_SMALL + g)),
        pl.BlockSpec((8, hp), lambda b, g: (0, g)),
        pl.BlockSpec((8, ns), lambda b, g: (0, SSM_INNER // ns + g)),
        pl.BlockSpec((8, ns), lambda b, g: (0, (SSM_INNER + SSM_BC) // ns + g)),
        pl.BlockSpec((1, hp), lambda b, g: (0, g)),
        pl.BlockSpec((1, ns), lambda b, g: (0, SSM_INNER // ns + g)),
        pl.BlockSpec((1, ns), lambda b, g: (0, (SSM_INNER + SSM_BC) // ns + g)),
        pl.BlockSpec((1, SMALL_W), lambda b, g: (0, g)),
        pl.BlockSpec((1, SMALL_W), lambda b, g: (0, g)),
        pl.BlockSpec((1, hp), lambda b, g: (0, g)),
    ]
    args = [proj, proj, proj, proj, small, conv_w, conv_w, conv_w, conv_b, conv_b, conv_b,
            dt_bias_row, a_row, d_skip]
    if has_h0:
        in_specs.append(pl.BlockSpec((None, 2, SSM_HPG, SSM_HEADDIM, ns), lambda b, g: (b, 0, g, 0, 0)))
        args.append(h0)
    out_specs = [pl.BlockSpec((seq, hp), lambda b, g: (b, g))]
    out_shape = [jax.ShapeDtypeStruct((n_seq * seq, SSM_INNER), f32)]
    if emit_state:
        out_specs.append(pl.BlockSpec((1, 2, SSM_HPG, SSM_HEADDIM, ns), lambda b, g: (b, 0, g, 0, 0)))
        out_shape.append(jax.ShapeDtypeStruct((n_seq, 2, SSM_HEADS, SSM_HEADDIM, ns), f32))
    res = pl.pallas_call(
        functools.partial(_ssd_kernel, seq=seq, has_h0=has_h0, emit_state=emit_state),
        grid=(n_seq, SSM_GROUPS),
        in_specs=in_specs,
        out_specs=out_specs,
        out_shape=out_shape,
        scratch_shapes=[
            pltpu.VMEM((seq, hp), f32),
            pltpu.VMEM((seq, hp), f32),
            pltpu.VMEM((seq, ns), f32),
            pltpu.VMEM((seq, ns), f32),
        ],
        compiler_params=_cparams(("parallel", "arbitrary")),
        name="ssd_lat" if has_h0 else "ssd_ctx",
    )(*args)
    return res if emit_state else (res[0], None)


def _attn_ctx_kernel(q_ref, k_ref, v_ref, sink_ref, o_ref):
    k = k_ref[...].astype(bf16)
    v = v_ref[...].astype(bf16)
    kvh = pl.program_id(1)
    for g in range(Q_PER_KV):
        q = q_ref[:, g * HEAD_DIM:(g + 1) * HEAD_DIM].astype(bf16)
        s = lax.dot_general(q, k, NT, preferred_element_type=f32) * (HEAD_DIM ** -0.5)
        sink = sink_ref[pl.ds(kvh * Q_PER_KV + g, 1), 0:1]
        m = jnp.maximum(jnp.max(s, axis=-1, keepdims=True), sink)
        p = jnp.exp(s - m)
        den = jnp.sum(p, axis=-1, keepdims=True) + jnp.exp(sink - m)
        o = jnp.dot(p.astype(bf16), v, preferred_element_type=f32) / den
        o_ref[:, g * HEAD_DIM:(g + 1) * HEAD_DIM] = o.astype(o_ref.dtype)


def _attn_ctx(proj, sink8, n_seq=BATCH):
    qw = Q_PER_KV * HEAD_DIM
    return pl.pallas_call(
        _attn_ctx_kernel,
        grid=(n_seq, KV_HEADS),
        in_specs=[
            pl.BlockSpec((SEQ, qw), lambda b, h: (b, COL_AQ // qw + h)),
            pl.BlockSpec((SEQ, HEAD_DIM), lambda b, h: (b, COL_AK // HEAD_DIM + h)),
            pl.BlockSpec((SEQ, HEAD_DIM), lambda b, h: (b, COL_AV // HEAD_DIM + h)),
            pl.BlockSpec((ATTN_HEADS, 128), lambda b, h: (0, 0)),
        ],
        out_specs=pl.BlockSpec((SEQ, qw), lambda b, h: (b, h)),
        out_shape=jax.ShapeDtypeStruct((n_seq * SEQ, ATTN_Q), bf16),
        compiler_params=_cparams(("parallel", "arbitrary")),
        name="attn_ctx",
    )(proj, proj, proj, sink8)


def _rope(x, cos, sin_signed):
    quarter = HEAD_DIM // 4
    lane = lax.broadcasted_iota(jnp.int32, x.shape, 1)
    first = (lane % (2 * quarter)) < quarter
    partner = jnp.where(first, pltpu.roll(x, HEAD_DIM - quarter, 1), pltpu.roll(x, quarter, 1))
    return x * cos + partner * sin_signed


def _attn_lat_kernel(q_ref, k_ref, v_ref, kc_ref, vc_ref, cos_ref, sin_ref, sink_ref, o_ref, kr_ref):
    kvh = pl.program_id(1)
    cos = cos_ref[...]
    sin = sin_ref[...]
    kr_ref[...] = _rope(k_ref[...].astype(f32), cos, sin).astype(bf16)
    kc = kc_ref[...].astype(bf16)
    vc = vc_ref[...].astype(bf16)
    blk = WINDOW
    n_blk = DEC_SEQ // blk
    scale = HEAD_DIM ** -0.5
    for i in range(n_blk):
        lo = max(i - 1, 0) * blk
        hi = min(i + 2, n_blk) * blk
        kw = kr_ref[lo:hi, :]
        vw = v_ref[lo:hi, :].astype(bf16)
        qpos = i * blk + lax.broadcasted_iota(jnp.int32, (blk, hi - lo), 0)
        kpos = lo + lax.broadcasted_iota(jnp.int32, (blk, hi - lo), 1)
        win = jnp.abs(qpos - kpos) <= WINDOW
        rows = slice(i * blk, (i + 1) * blk)
        for g in range(Q_PER_KV):
            cols = slice(g * HEAD_DIM, (g + 1) * HEAD_DIM)
            q = _rope(q_ref[rows, cols].astype(f32), cos[rows, :], sin[rows, :]).astype(bf16)
            s_c = lax.dot_general(q, kc, NT, preferred_element_type=f32) * scale
            s_w = lax.dot_general(q, kw, NT, preferred_element_type=f32) * scale
            s_w = jnp.where(win, s_w, -jnp.inf)
            sink = sink_ref[pl.ds(kvh * Q_PER_KV + g, 1), 0:1]
            m = jnp.maximum(jnp.maximum(jnp.max(s_c, axis=-1, keepdims=True),
                                        jnp.max(s_w, axis=-1, keepdims=True)), sink)
            p_c = jnp.exp(s_c - m)
            p_w = jnp.exp(s_w - m)
            den = (jnp.sum(p_c, axis=-1, keepdims=True) + jnp.sum(p_w, axis=-1, keepdims=True)
                   + jnp.exp(sink - m))
            o = (jnp.dot(p_c.astype(bf16), vc, preferred_element_type=f32)
                 + jnp.dot(p_w.astype(bf16), vw, preferred_element_type=f32)) / den
            o_ref[rows, cols] = o.astype(o_ref.dtype)


def _attn_lat(proj, cache_k, cache_v, l, cos, sin, sink8, rb0=N_PROMPT // DEC_SEQ, n_seq=DEC_BATCH):
    qw = Q_PER_KV * HEAD_DIM
    ck = cache_k.reshape(n_seq, DEPTH, PAST_LEN, ATTN_KV)
    cv = cache_v.reshape(n_seq, DEPTH, PAST_LEN, ATTN_KV)
    return pl.pallas_call(
        _attn_lat_kernel,
        grid=(n_seq, KV_HEADS),
        in_specs=[
            pl.BlockSpec((DEC_SEQ, qw), lambda b, h: (rb0 + b, COL_AQ // qw + h)),
            pl.BlockSpec((DEC_SEQ, HEAD_DIM), lambda b, h: (rb0 + b, COL_AK // HEAD_DIM + h)),
            pl.BlockSpec((DEC_SEQ, HEAD_DIM), lambda b, h: (rb0 + b, COL_AV // HEAD_DIM + h)),
            pl.BlockSpec((None, None, PAST_LEN, HEAD_DIM), lambda b, h: (b, l, 0, h)),
            pl.BlockSpec((None, None, PAST_LEN, HEAD_DIM), lambda b, h: (b, l, 0, h)),
            pl.BlockSpec((DEC_SEQ, HEAD_DIM), lambda b, h: (0, 0)),
            pl.BlockSpec((DEC_SEQ, HEAD_DIM), lambda b, h: (0, 0)),
            pl.BlockSpec((ATTN_HEADS, 128), lambda b, h: (0, 0)),
        ],
        out_specs=pl.BlockSpec((DEC_SEQ, qw), lambda b, h: (b, h)),
        out_shape=jax.ShapeDtypeStruct((n_seq * DEC_SEQ, ATTN_Q), bf16),
        scratch_shapes=[pltpu.VMEM((DEC_SEQ, HEAD_DIM), bf16)],
        compiler_params=_cparams(("parallel", "arbitrary")),
        name="attn_lat",
    )(proj, proj, proj, ck, cv, cos, sin, sink8)


def _rope_tables():
    quarter = HEAD_DIM // 4
    freqs = ROPE_THETA ** (-np.arange(quarter, dtype=np.float32) / quarter)
    t = np.arange(DEC_SEQ)
    cos = np.zeros((DEC_SEQ, HEAD_DIM), np.float32)
    sin = np.zeros((DEC_SEQ, HEAD_DIM), np.float32)
    for half, pos in enumerate((t // GRID_W, t % GRID_W)):
        ang = pos.astype(np.float32)[:, None] * freqs[None, :]
        base = half * 2 * quarter
        cos[:, base:base + quarter] = np.cos(ang)
        cos[:, base + quarter:base + 2 * quarter] = np.cos(ang)
        sin[:, base:base + quarter] = -np.sin(ang)
        sin[:, base + quarter:base + 2 * quarter] = np.sin(ang)
    return jnp.asarray(cos), jnp.asarray(sin)


def _merge_kernel(x_ref, gt_ref, ogc_ref, ogl_ref, osc_ref, osl_ref, oac_ref, oal_ref, b0_ref, b1_ref, b2_ref,
                  sn_ref, wg_ref, ws_ref, wa_ref, wo_ref, o_ref, osn_ref):
    i = pl.program_id(0)
    j = pl.program_id(1)

    def step(og_ref, os_ref, oa_ref):
        @pl.when(j == 0)
        def _():
            y = os_ref[...]
            ms = jnp.mean(y * y, axis=-1, keepdims=True)
            osn_ref[...] = (y * lax.rsqrt(ms + RMS_EPS) * sn_ref[...]).astype(bf16)
            o_ref[...] = jnp.zeros_like(o_ref)

        m = (b0_ref[...].astype(f32) * jnp.dot(og_ref[...], wg_ref[...], preferred_element_type=f32)
             + b1_ref[...].astype(f32) * jnp.dot(osn_ref[...], ws_ref[...], preferred_element_type=f32)
             + b2_ref[...].astype(f32) * jnp.dot(oa_ref[...], wa_ref[...], preferred_element_type=f32))
        o_ref[...] += jnp.dot(m.astype(bf16), wo_ref[...], preferred_element_type=f32)

    @pl.when(i < N_PROMPT // TM)
    def _():
        step(ogc_ref, osc_ref, oac_ref)

    @pl.when(i >= N_PROMPT // TM)
    def _():
        step(ogl_ref, osl_ref, oal_ref)

    @pl.when(j == pl.num_programs(1) - 1)
    def _():
        o_ref[...] = x_ref[...] + gt_ref[0] * o_ref[...]


def _merge(x, mod3, l, proj, o_gla, o_ssm, o_att, ssm_norm, w_g, w_s, w_a, w_o):
    n_tok = N_TOK
    nj = D_MODEL // TJ

    def pair_specs(width):
        return [pl.BlockSpec((TM, width), lambda i, j: (_ctx_blk(i, TM), 0)),
                pl.BlockSpec((TM, width), lambda i, j: (_lat_blk(i, TM), 0))]

    def br_spec(k):
        return pl.BlockSpec((TM, TJ), lambda i, j: (i, (COL_BR + k * D_MODEL) // TJ + j))

    def wbr_spec():
        return pl.BlockSpec((None, GLA_VAL, TJ), lambda i, j: (l, 0, j))

    return pl.pallas_call(
        _merge_kernel,
        grid=(n_tok // TM, nj),
        in_specs=[
            pl.BlockSpec((TM, D_MODEL), lambda i, j: (i, 0)),
            pl.BlockSpec((1, 1, D_MODEL), lambda i, j: (_mod_row(i, TM), 0, 5)),
            *pair_specs(GLA_VAL), *pair_specs(SSM_INNER), *pair_specs(ATTN_Q),
            br_spec(0), br_spec(1), br_spec(2),
            pl.BlockSpec((None, 1, SSM_INNER), lambda i, j: (l, 0, 0)),
            wbr_spec(), wbr_spec(), wbr_spec(),
            pl.BlockSpec((None, TJ, D_MODEL), lambda i, j: (l, j, 0)),
        ],
        out_specs=pl.BlockSpec((TM, D_MODEL), lambda i, j: (i, 0)),
        out_shape=jax.ShapeDtypeStruct((n_tok, D_MODEL), f32),
        scratch_shapes=[pltpu.VMEM((TM, SSM_INNER), bf16)],
        compiler_params=_cparams(("arbitrary", "arbitrary")),
        name="merge",
    )(x, mod3, *o_gla, *o_ssm, *o_att, proj, proj, proj, ssm_norm, w_g, w_s, w_a, w_o)


_IN_SRC = np.cumsum((0, GLA_QK, GLA_QK, GLA_VAL, GLA_VAL, 2 * GLA_RANK, SSM_INNER, CONV_CH, 2 * SSM_HEADS,
                     ATTN_Q, ATTN_KV, ATTN_KV, 3 * D_MODEL))
SRC_GDOWN, SRC_SZ, SRC_DT, SRC_AQ, SRC_BR, SRC_END = (int(_IN_SRC[k]) for k in (4, 5, 7, 8, 11, 12))
LANE = 128
W_PREP_ROWS = 128


def _w_in_prep_kernel(w_ref, o_ref):
    def move(src, dst, width):
        lo = src // LANE * LANE
        hi = min(-(-(src + width) // LANE) * LANE, SRC_END)
        win = w_ref[0, :, lo:hi]
        o_ref[0, :, dst:dst + width] = win[:, src - lo:src - lo + width].astype(bf16)

    move(0, COL_GQ, SRC_GDOWN)
    move(SRC_SZ, COL_SZ, SRC_DT - SRC_SZ)
    move(SRC_AQ, COL_AQ, SRC_BR - SRC_AQ)
    move(SRC_BR, COL_BR, SRC_END - SRC_BR)

    t_gd = w_ref[0, :, SRC_GDOWN:SRC_GDOWN + LANE].astype(bf16)
    dt_lo = SRC_DT // LANE * LANE
    t_dt = w_ref[0, :, dt_lo:dt_lo + LANE].astype(bf16)
    r = lax.broadcasted_iota(jnp.int32, (LANE, LANE), 0)
    c = lax.broadcasted_iota(jnp.int32, (LANE, LANE), 1)
    sel_gd = jnp.where((r == c) & (c < 2 * GLA_RANK), 1.0, 0.0).astype(bf16)
    k = c - SMALL_DT
    for g in range(SSM_GROUPS):
        src_lane = (SRC_DT - dt_lo) + (k // SSM_HPG) * SSM_HEADS + g * SSM_HPG + k % SSM_HPG
        sel_dt = jnp.where((k >= 0) & (k < 2 * SSM_HPG) & (r == src_lane), 1.0, 0.0).astype(bf16)
        blk = (jnp.dot(t_gd, sel_gd, preferred_element_type=f32)
               + jnp.dot(t_dt, sel_dt, preferred_element_type=f32))
        o_ref[0, :, COL_SMALL + g * SMALL_W:COL_SMALL + (g + 1) * SMALL_W] = blk.astype(bf16)


def _w_in_prep(w_in):
    return pl.pallas_call(
        _w_in_prep_kernel,
        grid=(DEPTH, D_MODEL // W_PREP_ROWS),
        in_specs=[pl.BlockSpec((1, W_PREP_ROWS, SRC_END), lambda l, i: (l, i, 0))],
        out_specs=pl.BlockSpec((1, W_PREP_ROWS, D_INP), lambda l, i: (l, i, 0)),
        out_shape=jax.ShapeDtypeStruct((DEPTH, D_MODEL, D_INP), bf16),
        compiler_params=_cparams(("arbitrary", "arbitrary")),
        name="w_in_prep",
    )(w_in)


def kernel(x_prompt, x_sample, c, cache_k, cache_v, state_gla, state_ssm, c_ctx, w_mod, b_mod, ffn1_norm,
           ffn1_w_gate, ffn1_w_up, ffn1_w_down, mix_norm, w_in, gla_w_up, gla_b_up, gla_norm, ssm_conv_w,
           ssm_conv_b, ssm_dt_bias, ssm_a_log, ssm_d, ssm_norm, attn_sink, w_br_gla, w_br_ssm, w_br_attn,
           w_out, ffn2_norm, ffn2_w_gate, ffn2_w_up, ffn2_w_down, final_norm):
    x = (x_prompt.reshape(N_PROMPT, D_MODEL), x_sample.reshape(N_SAMPLE, D_MODEL))
    cvec = jnp.concatenate([c_ctx[None], c, jnp.zeros((MOD_ROWS - 1 - DEC_BATCH, D_MODEL), f32)], axis=0)
    mod = _modulation(cvec, w_mod, b_mod)

    w_in_p = _w_in_prep(w_in)
    f1g, f1u, f1d = ffn1_w_gate.astype(bf16), ffn1_w_up.astype(bf16), ffn1_w_down.astype(bf16)
    f2g, f2u, f2d = ffn2_w_gate.astype(bf16), ffn2_w_up.astype(bf16), ffn2_w_down.astype(bf16)
    wbg, wbs, wba, wo = (w_br_gla.astype(bf16), w_br_ssm.astype(bf16), w_br_attn.astype(bf16),
                         w_out.astype(bf16))

    wz = jnp.zeros((DEPTH, 2, SMALL_W, GLA_QK), f32)
    for d in range(2):
        wz = wz.at[:, d, d * GLA_RANK:(d + 1) * GLA_RANK, :].set(gla_w_up[:, d])
    wz = wz.astype(bf16)
    bz = gla_b_up.reshape(DEPTH, 2, 1, GLA_QK)

    def dt_lanes(v):
        vg = v.reshape(DEPTH, 2, SSM_GROUPS, SSM_HPG).transpose(0, 2, 1, 3).reshape(DEPTH, SSM_GROUPS, 2 * SSM_HPG)
        row = jnp.zeros((DEPTH, SSM_GROUPS, SMALL_W), f32)
        row = row.at[:, :, SMALL_DT:SMALL_DT + 2 * SSM_HPG].set(vg)
        return row.reshape(DEPTH, 1, SSM_GROUPS * SMALL_W)

    dtb_rows = dt_lanes(ssm_dt_bias)
    a_rows = dt_lanes(-jnp.exp(ssm_a_log))
    d_skip = jnp.repeat(ssm_d, SSM_HEADDIM, axis=-1).reshape(DEPTH, 1, SSM_INNER)
    conv_w = jnp.concatenate([ssm_conv_w, jnp.zeros((DEPTH, 8 - SSM_CONV, CONV_CH), f32)], axis=1)
    conv_b = ssm_conv_b.reshape(DEPTH, 1, CONV_CH)
    sink8 = jnp.broadcast_to(attn_sink[:, :, None], (DEPTH, ATTN_HEADS, 128))
    cos, sin = _rope_tables()

    n1, nm, n2 = (w.reshape(DEPTH, 1, D_MODEL) for w in (ffn1_norm, mix_norm, ffn2_norm))
    sn = ssm_norm.reshape(DEPTH, 1, SSM_INNER)

    new_k, new_v, new_gla, new_ssm = [], [], [], []
    for l in range(DEPTH):
        mod3 = mod[l].reshape(MOD_ROWS, 1, N_MOD * D_MODEL)
        x = _ffn(x, mod3, l, 0, n1, f1g, f1u, f1d)
        proj, aux = _inproj(x, mod3, l, nm, w_in_p)

        og_c, st_g = _gla(proj, aux, 0, BATCH, SEQ, wz[l], bz[l], gla_norm[l][None], None)
        og_s, _ = _gla(proj, aux, N_PROMPT // DEC_SEQ, DEC_BATCH, DEC_SEQ, wz[l], bz[l], gla_norm[l][None],
                       state_gla[:, l])
        os_c, st_s = _ssd(proj, aux, 0, BATCH, SEQ, conv_w[l], conv_b[l], dtb_rows[l], a_rows[l], d_skip[l], None)
        os_s, _ = _ssd(proj, aux, N_PROMPT // DEC_SEQ, DEC_BATCH, DEC_SEQ, conv_w[l], conv_b[l], dtb_rows[l],
                       a_rows[l], d_skip[l], state_ssm[:, l])
        oa_c = _attn_ctx(proj, sink8[l])
        oa_s = _attn_lat(proj, cache_k, cache_v, l, cos, sin, sink8[l])

        x = _merge(x, mod3, l, proj, (og_c, og_s), (os_c, os_s), (oa_c, oa_s), sn, wbg, wbs, wba, wo)
        x = _ffn(x, mod3, l, 2, n2, f2g, f2u, f2d, final_w=final_norm if l == DEPTH - 1 else None)

        new_k.append(aux[:N_PROMPT, :ATTN_KV].reshape(BATCH, SEQ, KV_HEADS, HEAD_DIM))
        new_v.append(aux[:N_PROMPT, ATTN_KV:2 * ATTN_KV].reshape(BATCH, SEQ, KV_HEADS, HEAD_DIM))
        new_gla.append(st_g)
        new_ssm.append(st_s)

    y_prompt = x[0].reshape(BATCH, SEQ, D_MODEL)
    y_sample = x[1].reshape(DEC_BATCH, DEC_SEQ, D_MODEL)
    return (y_prompt, y_sample, jnp.stack(new_k, axis=1), jnp.stack(new_v, axis=1),
            jnp.stack(new_gla, axis=1), jnp.stack(new_ssm, axis=1))
```

```python
import functools
import math

import numpy as np
import jax
import jax.numpy as jnp
from jax import lax
from jax.experimental import pallas as pl
from jax.experimental.pallas import tpu as pltpu

f32 = jnp.float32
bf16 = jnp.bfloat16

D_MODEL = 2048
BATCH = 32
SEQ = 256
DEPTH = 2
DEC_BATCH = 2
DEC_SEQ = 1024
PAST_LEN = 512
GRID_W = 64
RMS_EPS = 1e-6
N_MOD = 9
D_FF = 5632
GLA_HEADS = 4
GLA_DK = 128
GLA_DV = 256
GLA_RANK = 16
GLA_GATE_NORM = 16.0
GLA_CHUNK = 16
SSM_HEADS = 16
SSM_HEADDIM = 64
SSM_GROUPS = 4
SSM_HPG = SSM_HEADS // SSM_GROUPS
SSM_STATE = 128
SSM_CONV = 5
SSM_INNER = SSM_HEADS * SSM_HEADDIM
SSM_BC = SSM_GROUPS * SSM_STATE
CONV_CH = SSM_INNER + 2 * SSM_BC
ATTN_HEADS = 8
KV_HEADS = 2
Q_PER_KV = ATTN_HEADS // KV_HEADS
HEAD_DIM = 128
WINDOW = 128
ROPE_THETA = 10000.0
GLA_QK = GLA_HEADS * GLA_DK
GLA_VAL = GLA_HEADS * GLA_DV
ATTN_Q = ATTN_HEADS * HEAD_DIM
ATTN_KV = KV_HEADS * HEAD_DIM

N_PROMPT = BATCH * SEQ
N_SAMPLE = DEC_BATCH * DEC_SEQ
N_TOK = N_PROMPT + N_SAMPLE
MOD_ROWS = 8

COL_GQ = 0
COL_GK = COL_GQ + GLA_QK
COL_GV = COL_GK + GLA_QK
COL_GR = COL_GV + GLA_VAL
COL_SZ = COL_GR + GLA_VAL
COL_SX = COL_SZ + SSM_INNER
COL_SB = COL_SX + SSM_INNER
COL_SC = COL_SB + SSM_BC
COL_AQ = COL_SC + SSM_BC
COL_AK = COL_AQ + ATTN_Q
COL_AV = COL_AK + ATTN_KV
COL_SMALL = COL_AV + ATTN_KV
SMALL_W = 128
SMALL_DT = 2 * GLA_RANK
COL_BR = 8192
D_INP = COL_BR + 3 * D_MODEL

TM = 512
TF = 512
TM_BIG = 1024
TF_BIG = 256
TN_IN = 1024
AUX_TILE = COL_AK // TN_IN
AUX_SMALL = (COL_SMALL - COL_AK) // SMALL_W
TN_MOD = 1024
TJ = 512
GLA_SB = 256
GLA_TILE = 128
SSD_CHUNK = 128
VMEM_LIMIT = 58 * 1024 * 1024

NT = (((1,), (1,)), ((), ()))
TN = (((0,), (0,)), ((), ()))


def _silu(x):
    return x * jax.nn.sigmoid(x)


def _mod_row(i, tm):
    n_p = N_PROMPT // tm
    per = DEC_SEQ // tm
    return jnp.where(i < n_p, 0, 1 + (i - n_p) // per)


def _cparams(sem):
    return pltpu.CompilerParams(dimension_semantics=sem, vmem_limit_bytes=VMEM_LIMIT)


def _mod_kernel(c_ref, w_ref, b_ref, o_ref):
    s = _silu(c_ref[...]).astype(bf16)
    o_ref[0] = jnp.dot(s, w_ref[0].astype(bf16), preferred_element_type=f32) + b_ref[0]


def _modulation(cvec, w_mod, b_mod):
    n = N_MOD * D_MODEL
    return pl.pallas_call(
        _mod_kernel,
        grid=(DEPTH, n // TN_MOD),
        in_specs=[
            pl.BlockSpec((MOD_ROWS, D_MODEL), lambda l, j: (0, 0)),
            pl.BlockSpec((1, D_MODEL, TN_MOD), lambda l, j: (l, 0, j)),
            pl.BlockSpec((1, 1, TN_MOD), lambda l, j: (l, 0, j)),
        ],
        out_specs=pl.BlockSpec((1, MOD_ROWS, TN_MOD), lambda l, j: (l, 0, j)),
        out_shape=jax.ShapeDtypeStruct((DEPTH, MOD_ROWS, n), f32),
        compiler_params=_cparams(("arbitrary", "arbitrary")),
        name="modulation",
    )(cvec, w_mod, b_mod.reshape(DEPTH, 1, n))


def _norm_mod(x, nw, sh, sc):
    ms = jnp.mean(x * x, axis=-1, keepdims=True)
    h = x * lax.rsqrt(ms + RMS_EPS) * nw
    return h * (1.0 + sc) + sh


def _pick_stream(i, tm, ctx_val, lat_val):
    return jnp.where(i < N_PROMPT // tm, ctx_val, lat_val)


def _ctx_blk(i, tm):
    return jnp.minimum(i, N_PROMPT // tm - 1)


def _lat_blk(i, tm):
    return jnp.maximum(i - N_PROMPT // tm, 0)


def _ffn_kernel(*refs, split_in, final, tm):
    refs = list(refs)
    x_refs = [refs.pop(0) for _ in range(2 if split_in else 1)]
    sh_ref, sc_ref, gt_ref, nw_ref, wg_ref, wu_ref, wd_ref = refs[:7]
    refs = refs[7:]
    fw_ref = refs.pop(0) if final else None
    o_refs = [refs.pop(0) for _ in range(2 if final else 1)]
    h_ref = refs.pop(0)
    acc_ref = refs.pop(0) if final else o_refs[0]
    i = pl.program_id(0)
    j = pl.program_id(1)
    n_ctx = N_PROMPT // tm

    def read_x():
        if split_in:
            return _pick_stream(i, tm, x_refs[0][...], x_refs[1][...])
        return x_refs[0][...]

    @pl.when(j == 0)
    def _():
        h_ref[...] = _norm_mod(read_x(), nw_ref[...], sh_ref[0], sc_ref[0]).astype(bf16)
        acc_ref[...] = jnp.zeros_like(acc_ref)

    h = h_ref[...]
    g = jnp.dot(h, wg_ref[...].astype(bf16), preferred_element_type=f32)
    u = jnp.dot(h, wu_ref[...].astype(bf16), preferred_element_type=f32)
    a = (_silu(g) * u).astype(bf16)
    acc_ref[...] += jnp.dot(a, wd_ref[...].astype(bf16), preferred_element_type=f32)

    last = j == pl.num_programs(1) - 1

    def result():
        y = read_x() + 0.5 * gt_ref[0] * acc_ref[...]
        if final:
            ms = jnp.mean(y * y, axis=-1, keepdims=True)
            y = y * lax.rsqrt(ms + RMS_EPS) * fw_ref[...]
        return y

    if final:
        @pl.when(last & (i < n_ctx))
        def _():
            o_refs[0][...] = result()

        @pl.when(last & (i >= n_ctx))
        def _():
            o_refs[1][...] = result()
    else:
        @pl.when(last)
        def _():
            o_refs[0][...] = result()


def _ffn(x, mod3, l, slot, nw, wg, wu, wd, wl, tm, tf, final_w=None):
    split_in = isinstance(x, tuple)
    final = final_w is not None

    def mod_spec(k):
        return pl.BlockSpec((1, 1, D_MODEL), lambda i, j: (_mod_row(i, tm), 0, 3 * slot + k))

    if split_in:
        in_specs = [pl.BlockSpec((tm, D_MODEL), lambda i, j: (_ctx_blk(i, tm), 0)),
                    pl.BlockSpec((tm, D_MODEL), lambda i, j: (_lat_blk(i, tm), 0))]
        args = list(x)
    else:
        in_specs = [pl.BlockSpec((tm, D_MODEL), lambda i, j: (i, 0))]
        args = [x]
    in_specs += [
        mod_spec(0), mod_spec(1), mod_spec(2),
        pl.BlockSpec((None, 1, D_MODEL), lambda i, j: (l, 0, 0)),
        pl.BlockSpec((None, D_MODEL, tf), lambda i, j: (wl, 0, j)),
        pl.BlockSpec((None, D_MODEL, tf), lambda i, j: (wl, 0, j)),
        pl.BlockSpec((None, tf, D_MODEL), lambda i, j: (wl, j, 0)),
    ]
    args += [mod3, mod3, mod3, nw, wg, wu, wd]
    scratch = [pltpu.VMEM((tm, D_MODEL), bf16)]
    if final:
        in_specs.append(pl.BlockSpec((1, D_MODEL), lambda i, j: (0, 0)))
        args.append(final_w.reshape(1, D_MODEL))
        out_specs = [pl.BlockSpec((tm, D_MODEL), lambda i, j: (_ctx_blk(i, tm), 0)),
                     pl.BlockSpec((tm, D_MODEL), lambda i, j: (_lat_blk(i, tm), 0))]
        out_shape = [jax.ShapeDtypeStruct((N_PROMPT, D_MODEL), f32), jax.ShapeDtypeStruct((N_SAMPLE, D_MODEL), f32)]
        scratch.append(pltpu.VMEM((tm, D_MODEL), f32))
    else:
        out_specs = pl.BlockSpec((tm, D_MODEL), lambda i, j: (i, 0))
        out_shape = jax.ShapeDtypeStruct((N_TOK, D_MODEL), f32)
    return pl.pallas_call(
        functools.partial(_ffn_kernel, split_in=split_in, final=final, tm=tm),
        grid=(N_TOK // tm, D_FF // tf),
        in_specs=in_specs,
        out_specs=out_specs,
        out_shape=out_shape,
        scratch_shapes=scratch,
        compiler_params=_cparams(("arbitrary", "arbitrary")),
        name="ffn_final" if final else ("ffn_first" if split_in else "ffn"),
    )(*args)


def _inproj_kernel(x_ref, sh_ref, sc_ref, nw_ref, w_ref, o_ref, os_ref, h_ref):
    j = pl.program_id(1)

    @pl.when(j == 0)
    def _():
        h_ref[...] = _norm_mod(x_ref[...], nw_ref[...], sh_ref[0], sc_ref[0]).astype(bf16)

    acc = jnp.dot(h_ref[...], w_ref[...], preferred_element_type=f32)

    @pl.when(j < COL_BR // TN_IN)
    def _():
        o_ref[...] = acc.astype(bf16)

    @pl.when(j >= COL_BR // TN_IN)
    def _():
        o_ref[...] = jax.nn.sigmoid(acc).astype(bf16)

    @pl.when(j == AUX_TILE)
    def _():
        os_ref[...] = acc


def _inproj(x, mod3, l, nw, w):
    n_tok = N_TOK
    tm = TM_BIG

    def mod_spec(k):
        return pl.BlockSpec((1, 1, D_MODEL), lambda i, j: (_mod_row(i, tm), 0, 3 + k))

    return pl.pallas_call(
        _inproj_kernel,
        grid=(n_tok // tm, D_INP // TN_IN),
        in_specs=[
            pl.BlockSpec((tm, D_MODEL), lambda i, j: (i, 0)),
            mod_spec(0), mod_spec(1),
            pl.BlockSpec((None, 1, D_MODEL), lambda i, j: (l, 0, 0)),
            pl.BlockSpec((None, D_MODEL, TN_IN), lambda i, j: (l, 0, j)),
        ],
        out_specs=[pl.BlockSpec((tm, TN_IN), lambda i, j: (i, j)),
                   pl.BlockSpec((tm, TN_IN), lambda i, j: (i, 0))],
        out_shape=[jax.ShapeDtypeStruct((n_tok, D_INP), bf16), jax.ShapeDtypeStruct((n_tok, TN_IN), f32)],
        scratch_shapes=[pltpu.VMEM((tm, D_MODEL), bf16)],
        compiler_params=_cparams(("arbitrary", "arbitrary")),
        name="inproj",
    )(x, mod3, mod3, nw, w)


def _split_dot(m, x):
    hi = x.astype(bf16)
    lo = (x - hi.astype(f32)).astype(bf16)
    return (jnp.dot(m, hi, preferred_element_type=f32) + jnp.dot(m, lo, preferred_element_type=f32))


def _log_sigmoid(x):
    return jnp.minimum(x, 0.0) - jnp.log1p(jnp.exp(-jnp.abs(x)))


def _gla_kernel(q_ref, k_ref, v_ref, r_ref, sm_ref, wz_ref, bz_ref, gn_ref, *rest, seq, has_h0, emit_state):
    rest = list(rest)
    h0_ref = rest.pop(0) if has_h0 else None
    o_ref = rest.pop(0)
    st_ref = rest.pop(0) if emit_state else None
    oin_ref, qe_ref, ke_ref, dec_ref, vt_ref = rest
    sb = GLA_SB
    tile = GLA_TILE
    per_tile = tile // GLA_CHUNK
    n_tile = seq // tile

    row = lax.broadcasted_iota(jnp.int32, (sb, sb), 0)
    col = lax.broadcasted_iota(jnp.int32, (sb, sb), 1)
    same = (row // GLA_CHUNK) == (col // GLA_CHUNK)
    masks = (same & (col <= row), same & (col >= row))
    tris = tuple(jnp.where(m, 1.0, 0.0).astype(bf16) for m in masks)
    blk = jnp.where(same, 1.0, 0.0).astype(bf16)

    for s in range(seq // sb):
        rows = slice(s * sb, (s + 1) * sb)
        sm = sm_ref[rows, :].astype(bf16)
        q = q_ref[rows, :].astype(f32) * (GLA_DK ** -0.5)
        k = k_ref[rows, :].astype(f32)
        v = v_ref[rows, :]
        v_t = v.astype(f32).T.astype(bf16)
        for t in range(sb // tile):
            vt_ref[s * (sb // tile) + t] = v_t[:, t * tile:(t + 1) * tile]
        for d in range(2):
            gz = jnp.dot(sm, wz_ref[d], preferred_element_type=f32) + bz_ref[d]
            la = _log_sigmoid(gz) * (1.0 / GLA_GATE_NORM)
            cum = _split_dot(tris[d], la)
            tot = _split_dot(blk, la)
            qe = q * jnp.exp(cum)
            kinv = k * jnp.exp(-cum)
            kend = k * jnp.exp(tot - cum)
            a = lax.dot_general(qe.astype(bf16), kinv.astype(bf16), NT, preferred_element_type=f32)
            a = jnp.where(masks[d], a, 0.0).astype(bf16)
            oin_ref[d, rows, :] = jnp.dot(a, v, preferred_element_type=f32)
            qe_ref[d, rows, :] = qe.astype(bf16)
            ke_ref[d, rows, :] = kend.astype(bf16)
            dec_ref[d, rows, :] = jnp.exp(tot)

    if has_h0:
        s_init = (h0_ref[0].T, h0_ref[1].T)
    else:
        s_init = (jnp.zeros((GLA_DV, GLA_DK), f32),) * 2

    def scan_tile(o, carry):
        sts = list(carry)
        for u in range(per_tile):
            for d in range(2):
                t_idx = o if d == 0 else n_tile - 1 - o
                pos = u if d == 0 else per_tile - 1 - u
                i0 = t_idx * tile + pos * GLA_CHUNK
                if not isinstance(i0, int):
                    i0 = pl.multiple_of(i0, GLA_CHUNK)
                rows = pl.ds(i0, GLA_CHUNK)
                inter = lax.dot_general(qe_ref[d, rows, :], sts[d].astype(bf16), NT, preferred_element_type=f32)
                oin_ref[d, rows, :] += inter
                pieces = [ke_ref[d, rows, :]]
                if pos:
                    pieces.insert(0, jnp.zeros((pos * GLA_CHUNK, GLA_DK), bf16))
                if pos < per_tile - 1:
                    pieces.append(jnp.zeros(((per_tile - 1 - pos) * GLA_CHUNK, GLA_DK), bf16))
                upd = jnp.dot(vt_ref[t_idx], jnp.concatenate(pieces, axis=0), preferred_element_type=f32)
                sts[d] = sts[d] * dec_ref[d, pl.ds(i0, 1), :] + upd
        return tuple(sts)

    if n_tile <= 2:
        s_fin = s_init
        for o in range(n_tile):
            s_fin = scan_tile(o, s_fin)
    else:
        s_fin = lax.fori_loop(0, n_tile, scan_tile, s_init)

    o = oin_ref[0] + oin_ref[1]
    ms = jnp.mean(o * o, axis=-1, keepdims=True)
    o = o * lax.rsqrt(ms + RMS_EPS) * gn_ref[...]
    o_ref[...] = (o * _silu(r_ref[...].astype(f32))).astype(o_ref.dtype)
    if emit_state:
        st_ref[0, 0] = s_fin[0].T
        st_ref[0, 1] = s_fin[1].T


def _gla(proj, small, row_blk0, n_seq, seq, wz, bz, gn, h0):
    has_h0 = h0 is not None
    emit_state = not has_h0
    kq, kv_ = GLA_DK, GLA_DV
    in_specs = [
        pl.BlockSpec((seq, kq), lambda b, h: (row_blk0 + b, COL_GQ // kq + h)),
        pl.BlockSpec((seq, kq), lambda b, h: (row_blk0 + b, COL_GK // kq + h)),
        pl.BlockSpec((seq, kv_), lambda b, h: (row_blk0 + b, COL_GV // kv_ + h)),
        pl.BlockSpec((seq, kv_), lambda b, h: (row_blk0 + b, COL_GR // kv_ + h)),
        pl.BlockSpec((seq, SMALL_W), lambda b, h: (row_blk0 + b, AUX_SMALL)),
        pl.BlockSpec((2, SMALL_W, kq), lambda b, h: (0, 0, h)),
        pl.BlockSpec((2, 1, kq), lambda b, h: (0, 0, h)),
        pl.BlockSpec((1, kv_), lambda b, h: (0, 0)),
    ]
    args = [proj, proj, proj, proj, small, wz, bz, gn]
    if has_h0:
        in_specs.append(pl.BlockSpec((None, 2, None, kq, kv_), lambda b, h: (b, 0, h, 0, 0)))
        args.append(h0)
    out_specs = [pl.BlockSpec((seq, kv_), lambda b, h: (b, h))]
    out_shape = [jax.ShapeDtypeStruct((n_seq * seq, GLA_VAL), bf16)]
    if emit_state:
        out_specs.append(pl.BlockSpec((1, 2, None, kq, kv_), lambda b, h: (b, 0, h, 0, 0)))
        out_shape.append(jax.ShapeDtypeStruct((n_seq, 2, GLA_HEADS, kq, kv_), f32))
    res = pl.pallas_call(
        functools.partial(_gla_kernel, seq=seq, has_h0=has_h0, emit_state=emit_state),
        grid=(n_seq, GLA_HEADS),
        in_specs=in_specs,
        out_specs=out_specs,
        out_shape=out_shape,
        scratch_shapes=[
            pltpu.VMEM((2, seq, kv_), f32),
            pltpu.VMEM((2, seq, kq), bf16),
            pltpu.VMEM((2, seq, kq), bf16),
            pltpu.VMEM((2, seq, kq), f32),
            pltpu.VMEM((seq // GLA_TILE, kv_, GLA_TILE), bf16),
        ],
        compiler_params=_cparams(("parallel", "arbitrary")),
        name="gla_lat" if has_h0 else "gla_ctx",
    )(*args)
    return res if emit_state else (res[0], None)


def _softplus(x):
    return jnp.maximum(x, 0.0) + jnp.log1p(jnp.exp(-jnp.abs(x)))


def _split3(x):
    hi = x.astype(bf16)
    r1 = x - hi.astype(f32)
    mid = r1.astype(bf16)
    lo = (r1 - mid.astype(f32)).astype(bf16)
    return hi, mid, lo


def _split3_dot(m, x):
    return sum(jnp.dot(m, t, preferred_element_type=f32) for t in _split3(x))


def _dot_sel(x, sel):
    return sum(jnp.dot(t, sel, preferred_element_type=f32) for t in _split3(x))


def _shift_rows(x, d, t_idx):
    n = x.shape[0]
    if d == 0:
        return x
    y = pltpu.roll(x, (-d) % n, 0)
    ok = (t_idx + d >= 0) & (t_idx + d < n)
    return jnp.where(ok, y, 0.0)


def _ssd_kernel(z_ref, x_ref, b_ref, c_ref, sm_ref, cwx_ref, cwb_ref, cwc_ref, cbx_ref, cbb_ref, cbc_ref,
                dtb_ref, a_ref, dsk_ref, *rest, seq, has_h0, emit_state):
    rest = list(rest)
    h0_ref = rest.pop(0) if has_h0 else None
    o_ref = rest.pop(0)
    st_ref = rest.pop(0) if emit_state else None
    y_ref, xs_ref, bm_ref, cm_ref = rest
    cs = SSD_CHUNK
    n_chunk = seq // cs
    hp = SSM_HPG * SSM_HEADDIM

    t_idx = lax.broadcasted_iota(jnp.int32, (seq, 1), 0)

    def conv_silu(src_ref, w_ref, bias_ref):
        xin = src_ref[...].astype(f32)
        acc = jnp.zeros_like(xin) + bias_ref[...]
        for j in range(SSM_CONV):
            acc = acc + w_ref[j:j + 1, :] * _shift_rows(xin, j - SSM_CONV // 2, t_idx)
        return _silu(acc)

    xs_ref[...] = conv_silu(x_ref, cwx_ref, cbx_ref)
    bm_ref[...] = conv_silu(b_ref, cwb_ref, cbb_ref)
    cm_ref[...] = conv_silu(c_ref, cwc_ref, cbc_ref)

    row = lax.broadcasted_iota(jnp.int32, (cs, cs), 0)
    col = lax.broadcasted_iota(jnp.int32, (cs, cs), 1)
    masks = (col <= row, col >= row)
    tris = tuple(jnp.where(m, 1.0, 0.0).astype(bf16) for m in masks)
    lane = lax.broadcasted_iota(jnp.int32, (1, SMALL_W), 1)

    dt_lane = (lane >= SMALL_DT) & (lane < SMALL_DT + 2 * SSM_HPG)
    a_row = a_ref[...]
    dt_bias = dtb_ref[...]

    def selectors(d):
        r = lax.broadcasted_iota(jnp.int32, (SMALL_W, SSM_HPG * cs), 0)
        c = lax.broadcasted_iota(jnp.int32, (SMALL_W, SSM_HPG * cs), 1)
        bcast = jnp.where(r == SMALL_DT + d * SSM_HPG + c // cs, 1.0, 0.0).astype(bf16)
        r = lax.broadcasted_iota(jnp.int32, (SMALL_W, hp), 0)
        c = lax.broadcasted_iota(jnp.int32, (SMALL_W, hp), 1)
        expand = jnp.where(r == SMALL_DT + d * SSM_HPG + c // SSM_HEADDIM, 1.0, 0.0).astype(bf16)
        return bcast, expand

    head_of_lane = lax.broadcasted_iota(jnp.int32, (cs, hp), 1) // SSM_HEADDIM
    bm_t = [bm_ref[c * cs:(c + 1) * cs, :].T.astype(bf16) for c in range(n_chunk)]

    y_ref[...] = jnp.zeros_like(y_ref)
    states = []
    for d in range(2):
        bcast, expand = selectors(d)
        if has_h0:
            st = h0_ref[d].reshape(hp, SSM_STATE).T
        else:
            st = jnp.zeros((SSM_STATE, hp), f32)
        order = range(n_chunk) if d == 0 else range(n_chunk - 1, -1, -1)
        for cidx in order:
            rows = slice(cidx * cs, (cidx + 1) * cs)
            dt = jnp.where(dt_lane, _softplus(sm_ref[rows, :] + dt_bias), 0.0)
            cum = _split3_dot(tris[d], dt * a_row)
            cum_t = cum.T
            cum_b = _dot_sel(cum, bcast)
            cum_e = _dot_sel(cum, expand)
            dt_e = _dot_sel(dt, expand)
            tot_e = cum_e[0:1, :] if d == 1 else cum_e[cs - 1:cs, :]
            xs = xs_ref[rows, :]
            cm = cm_ref[rows, :].astype(bf16)
            cb = lax.dot_general(cm, bm_ref[rows, :].astype(bf16), NT, preferred_element_type=f32)
            w_parts = []
            for j in range(SSM_HPG):
                ln = SMALL_DT + d * SSM_HPG + j
                seg = jnp.exp(jnp.where(masks[d], cum_b[:, j * cs:(j + 1) * cs] - cum_t[ln:ln + 1, :], -jnp.inf))
                w_parts.append((cb * seg).astype(bf16))
            w = jnp.concatenate(w_parts, axis=1)
            xd = xs * dt_e
            xd_bd = jnp.concatenate([jnp.where(head_of_lane == j, xd, 0.0).astype(bf16)
                                     for j in range(SSM_HPG)], axis=0)
            inter = jnp.dot(cm, st.astype(bf16), preferred_element_type=f32)
            y_ref[rows, :] += jnp.dot(w, xd_bd, preferred_element_type=f32) + jnp.exp(cum_e) * inter
            xw = (xd * jnp.exp(tot_e - cum_e)).astype(bf16)
            st = st * jnp.exp(tot_e) + jnp.dot(bm_t[cidx], xw, preferred_element_type=f32)
        states.append(st)

    y = y_ref[...] + dsk_ref[...] * xs_ref[...]
    o_ref[...] = y * _silu(z_ref[...].astype(f32))
    if emit_state:
        st_ref[0, 0] = states[0].T.reshape(SSM_HPG, SSM_HEADDIM, SSM_STATE)
        st_ref[0, 1] = states[1].T.reshape(SSM_HPG, SSM_HEADDIM, SSM_STATE)


def _ssd(proj, small, row_blk0, n_seq, seq, conv_w, conv_b, dt_bias_row, a_row, d_skip, h0):
    has_h0 = h0 is not None
    emit_state = not has_h0
    hp = SSM_HPG * SSM_HEADDIM
    ns = SSM_STATE
    in_specs = [
        pl.BlockSpec((seq, hp), lambda b, g: (row_blk0 + b, COL_SZ // hp + g)),
        pl.BlockSpec((seq, hp), lambda b, g: (row_blk0 + b, COL_SX // hp + g)),
        pl.BlockSpec((seq, ns), lambda b, g: (row_blk0 + b, COL_SB // ns + g)),
        pl.BlockSpec((seq, ns), lambda b, g: (row_blk0 + b, COL_SC // ns + g)),
        pl.BlockSpec((seq, SMALL_W), lambda b, g: (row_blk0 + b, AUX_SMALL + g)),
        pl.BlockSpec((8, hp), lambda b, g: (0, g)),
        pl.BlockSpec((8, ns), lambda b, g: (0, SSM_INNER // ns + g)),
        pl.BlockSpec((8, ns), lambda b, g: (0, (SSM_INNER + SSM_BC) // ns + g)),
        pl.BlockSpec((1, hp), lambda b, g: (0, g)),
        pl.BlockSpec((1, ns), lambda b, g: (0, SSM_INNER // ns + g)),
        pl.BlockSpec((1, ns), lambda b, g: (0, (SSM_INNER + SSM_BC) // ns + g)),
        pl.BlockSpec((1, SMALL_W), lambda b, g: (0, g)),
        pl.BlockSpec((1, SMALL_W), lambda b, g: (0, g)),
        pl.BlockSpec((1, hp), lambda b, g: (0, g)),
    ]
    args = [proj, proj, proj, proj, small, conv_w, conv_w, conv_w, conv_b, conv_b, conv_b,
            dt_bias_row, a_row, d_skip]
    if has_h0:
        in_specs.append(pl.BlockSpec((None, 2, SSM_HPG, SSM_HEADDIM, ns), lambda b, g: (b, 0, g, 0, 0)))
        args.append(h0)
    out_specs = [pl.BlockSpec((seq, hp), lambda b, g: (b, g))]
    out_shape = [jax.ShapeDtypeStruct((n_seq * seq, SSM_INNER), f32)]
    if emit_state:
        out_specs.append(pl.BlockSpec((1, 2, SSM_HPG, SSM_HEADDIM, ns), lambda b, g: (b, 0, g, 0, 0)))
        out_shape.append(jax.ShapeDtypeStruct((n_seq, 2, SSM_HEADS, SSM_HEADDIM, ns), f32))
    res = pl.pallas_call(
        functools.partial(_ssd_kernel, seq=seq, has_h0=has_h0, emit_state=emit_state),
        grid=(n_seq, SSM_GROUPS),
        in_specs=in_specs,
        out_specs=out_specs,
        out_shape=out_shape,
        scratch_shapes=[
            pltpu.VMEM((seq, hp), f32),
            pltpu.VMEM((seq, hp), f32),
            pltpu.VMEM((seq, ns), f32),
            pltpu.VMEM((seq, ns), f32),
        ],
        compiler_params=_cparams(("parallel", "arbitrary")),
        name="ssd_lat" if has_h0 else "ssd_ctx",
    )(*args)
    return res if emit_state else (res[0], None)


def _attn_ctx_kernel(q_ref, k_ref, v_ref, sink_ref, o_ref):
    for head in range(ATTN_HEADS):
        kv = slice(head // Q_PER_KV * HEAD_DIM, (head // Q_PER_KV + 1) * HEAD_DIM)
        cols = slice(head * HEAD_DIM, (head + 1) * HEAD_DIM)
        s = lax.dot_general(q_ref[:, cols], k_ref[:, kv], NT, preferred_element_type=f32) * (HEAD_DIM ** -0.5)
        sink = sink_ref[head:head + 1, 0:1]
        m = jnp.maximum(jnp.max(s, axis=-1, keepdims=True), sink)
        p = jnp.exp(s - m)
        den = jnp.sum(p, axis=-1, keepdims=True) + jnp.exp(sink - m)
        o = jnp.dot(p.astype(bf16), v_ref[:, kv], preferred_element_type=f32) / den
        o_ref[:, cols] = o.astype(o_ref.dtype)


def _attn_ctx(proj, sink8, n_seq=BATCH):
    return pl.pallas_call(
        _attn_ctx_kernel,
        grid=(n_seq,),
        in_specs=[
            pl.BlockSpec((SEQ, ATTN_Q), lambda b: (b, COL_AQ // ATTN_Q)),
            pl.BlockSpec((SEQ, ATTN_KV), lambda b: (b, COL_AK // ATTN_KV)),
            pl.BlockSpec((SEQ, ATTN_KV), lambda b: (b, COL_AV // ATTN_KV)),
            pl.BlockSpec((ATTN_HEADS, 128), lambda b: (0, 0)),
        ],
        out_specs=pl.BlockSpec((SEQ, ATTN_Q), lambda b: (b, 0)),
        out_shape=jax.ShapeDtypeStruct((n_seq * SEQ, ATTN_Q), bf16),
        compiler_params=_cparams(("arbitrary",)),
        name="attn_ctx",
    )(proj, proj, proj, sink8)


def _rope(x, cos, sin_signed):
    quarter = HEAD_DIM // 4
    lane = lax.broadcasted_iota(jnp.int32, x.shape, 1)
    first = (lane % (2 * quarter)) < quarter
    partner = jnp.where(first, pltpu.roll(x, HEAD_DIM - quarter, 1), pltpu.roll(x, quarter, 1))
    return x * cos + partner * sin_signed


def _attn_lat_kernel(q_ref, k_ref, v_ref, kc_ref, vc_ref, cos_ref, sin_ref, sink_ref, o_ref, kr_ref):
    kvh = pl.program_id(1)
    cos = cos_ref[...]
    sin = sin_ref[...]
    kr_ref[...] = _rope(k_ref[...].astype(f32), cos, sin).astype(bf16)
    kc = kc_ref[...].astype(bf16)
    vc = vc_ref[...].astype(bf16)
    blk = WINDOW
    n_blk = DEC_SEQ // blk
    scale = HEAD_DIM ** -0.5
    for i in range(n_blk):
        lo = max(i - 1, 0) * blk
        hi = min(i + 2, n_blk) * blk
        kw = kr_ref[lo:hi, :]
        vw = v_ref[lo:hi, :].astype(bf16)
        qpos = i * blk + lax.broadcasted_iota(jnp.int32, (blk, hi - lo), 0)
        kpos = lo + lax.broadcasted_iota(jnp.int32, (blk, hi - lo), 1)
        win = jnp.abs(qpos - kpos) <= WINDOW
        rows = slice(i * blk, (i + 1) * blk)
        for g in range(Q_PER_KV):
            cols = slice(g * HEAD_DIM, (g + 1) * HEAD_DIM)
            q = _rope(q_ref[rows, cols].astype(f32), cos[rows, :], sin[rows, :]).astype(bf16)
            s_c = lax.dot_general(q, kc, NT, preferred_element_type=f32) * scale
            s_w = lax.dot_general(q, kw, NT, preferred_element_type=f32) * scale
            s_w = jnp.where(win, s_w, -jnp.inf)
            sink = sink_ref[pl.ds(kvh * Q_PER_KV + g, 1), 0:1]
            m = jnp.maximum(jnp.maximum(jnp.max(s_c, axis=-1, keepdims=True),
                                        jnp.max(s_w, axis=-1, keepdims=True)), sink)
            p_c = jnp.exp(s_c - m)
            p_w = jnp.exp(s_w - m)
            den = (jnp.sum(p_c, axis=-1, keepdims=True) + jnp.sum(p_w, axis=-1, keepdims=True)
                   + jnp.exp(sink - m))
            o = (jnp.dot(p_c.astype(bf16), vc, preferred_element_type=f32)
                 + jnp.dot(p_w.astype(bf16), vw, preferred_element_type=f32)) / den
            o_ref[rows, cols] = o.astype(o_ref.dtype)


def _attn_lat(proj, cache_k, cache_v, l, cos, sin, sink8, rb0=N_PROMPT // DEC_SEQ, n_seq=DEC_BATCH):
    qw = Q_PER_KV * HEAD_DIM
    ck = cache_k.reshape(n_seq, DEPTH, PAST_LEN, ATTN_KV)
    cv = cache_v.reshape(n_seq, DEPTH, PAST_LEN, ATTN_KV)
    return pl.pallas_call(
        _attn_lat_kernel,
        grid=(n_seq, KV_HEADS),
        in_specs=[
            pl.BlockSpec((DEC_SEQ, qw), lambda b, h: (rb0 + b, COL_AQ // qw + h)),
            pl.BlockSpec((DEC_SEQ, HEAD_DIM), lambda b, h: (rb0 + b, COL_AK // HEAD_DIM + h)),
            pl.BlockSpec((DEC_SEQ, HEAD_DIM), lambda b, h: (rb0 + b, COL_AV // HEAD_DIM + h)),
            pl.BlockSpec((None, None, PAST_LEN, HEAD_DIM), lambda b, h: (b, l, 0, h)),
            pl.BlockSpec((None, None, PAST_LEN, HEAD_DIM), lambda b, h: (b, l, 0, h)),
            pl.BlockSpec((DEC_SEQ, HEAD_DIM), lambda b, h: (0, 0)),
            pl.BlockSpec((DEC_SEQ, HEAD_DIM), lambda b, h: (0, 0)),
            pl.BlockSpec((ATTN_HEADS, 128), lambda b, h: (0, 0)),
        ],
        out_specs=pl.BlockSpec((DEC_SEQ, qw), lambda b, h: (b, h)),
        out_shape=jax.ShapeDtypeStruct((n_seq * DEC_SEQ, ATTN_Q), bf16),
        scratch_shapes=[pltpu.VMEM((DEC_SEQ, HEAD_DIM), bf16)],
        compiler_params=_cparams(("parallel", "arbitrary")),
        name="attn_lat",
    )(proj, proj, proj, ck, cv, cos, sin, sink8)


def _rope_tables():
    quarter = HEAD_DIM // 4
    freqs = ROPE_THETA ** (-np.arange(quarter, dtype=np.float32) / quarter)
    t = np.arange(DEC_SEQ)
    cos = np.zeros((DEC_SEQ, HEAD_DIM), np.float32)
    sin = np.zeros((DEC_SEQ, HEAD_DIM), np.float32)
    for half, pos in enumerate((t // GRID_W, t % GRID_W)):
        ang = pos.astype(np.float32)[:, None] * freqs[None, :]
        base = half * 2 * quarter
        cos[:, base:base + quarter] = np.cos(ang)
        cos[:, base + quarter:base + 2 * quarter] = np.cos(ang)
        sin[:, base:base + quarter] = -np.sin(ang)
        sin[:, base + quarter:base + 2 * quarter] = np.sin(ang)
    return jnp.asarray(cos), jnp.asarray(sin)


def _merge_kernel(x_ref, gt_ref, ogc_ref, ogl_ref, osc_ref, osl_ref, oac_ref, oal_ref, b0_ref, b1_ref, b2_ref,
                  sn_ref, wg_ref, ws_ref, wa_ref, wo_ref, o_ref, osn_ref):
    i = pl.program_id(0)
    j = pl.program_id(1)

    def step(og_ref, os_ref, oa_ref):
        @pl.when(j == 0)
        def _():
            y = os_ref[...]
            ms = jnp.mean(y * y, axis=-1, keepdims=True)
            osn_ref[...] = (y * lax.rsqrt(ms + RMS_EPS) * sn_ref[...]).astype(bf16)
            o_ref[...] = jnp.zeros_like(o_ref)

        m = (b0_ref[...].astype(f32) * jnp.dot(og_ref[...], wg_ref[...], preferred_element_type=f32)
             + b1_ref[...].astype(f32) * jnp.dot(osn_ref[...], ws_ref[...], preferred_element_type=f32)
             + b2_ref[...].astype(f32) * jnp.dot(oa_ref[...], wa_ref[...], preferred_element_type=f32))
        o_ref[...] += jnp.dot(m.astype(bf16), wo_ref[...], preferred_element_type=f32)

    @pl.when(i < N_PROMPT // TM)
    def _():
        step(ogc_ref, osc_ref, oac_ref)

    @pl.when(i >= N_PROMPT // TM)
    def _():
        step(ogl_ref, osl_ref, oal_ref)

    @pl.when(j == pl.num_programs(1) - 1)
    def _():
        o_ref[...] = x_ref[...] + gt_ref[0] * o_ref[...]


def _merge(x, mod3, l, proj, o_gla, o_ssm, o_att, ssm_norm, w_g, w_s, w_a, w_o):
    n_tok = N_TOK
    nj = D_MODEL // TJ

    def pair_specs(width):
        return [pl.BlockSpec((TM, width), lambda i, j: (_ctx_blk(i, TM), 0)),
                pl.BlockSpec((TM, width), lambda i, j: (_lat_blk(i, TM), 0))]

    def br_spec(k):
        return pl.BlockSpec((TM, TJ), lambda i, j: (i, (COL_BR + k * D_MODEL) // TJ + j))

    def wbr_spec():
        return pl.BlockSpec((None, GLA_VAL, TJ), lambda i, j: (l, 0, j))

    return pl.pallas_call(
        _merge_kernel,
        grid=(n_tok // TM, nj),
        in_specs=[
            pl.BlockSpec((TM, D_MODEL), lambda i, j: (i, 0)),
            pl.BlockSpec((1, 1, D_MODEL), lambda i, j: (_mod_row(i, TM), 0, 5)),
            *pair_specs(GLA_VAL), *pair_specs(SSM_INNER), *pair_specs(ATTN_Q),
            br_spec(0), br_spec(1), br_spec(2),
            pl.BlockSpec((None, 1, SSM_INNER), lambda i, j: (l, 0, 0)),
            wbr_spec(), wbr_spec(), wbr_spec(),
            pl.BlockSpec((None, TJ, D_MODEL), lambda i, j: (l, j, 0)),
        ],
        out_specs=pl.BlockSpec((TM, D_MODEL), lambda i, j: (i, 0)),
        out_shape=jax.ShapeDtypeStruct((n_tok, D_MODEL), f32),
        scratch_shapes=[pltpu.VMEM((TM, SSM_INNER), bf16)],
        compiler_params=_cparams(("arbitrary", "arbitrary")),
        name="merge",
    )(x, mod3, *o_gla, *o_ssm, *o_att, proj, proj, proj, ssm_norm, w_g, w_s, w_a, w_o)


_IN_SRC = np.cumsum((0, GLA_QK, GLA_QK, GLA_VAL, GLA_VAL, 2 * GLA_RANK, SSM_INNER, CONV_CH, 2 * SSM_HEADS,
                     ATTN_Q, ATTN_KV, ATTN_KV, 3 * D_MODEL))
SRC_GDOWN, SRC_SZ, SRC_DT, SRC_AQ, SRC_BR, SRC_END = (int(_IN_SRC[k]) for k in (4, 5, 7, 8, 11, 12))
W_PREP_K = 256
W_PREP_CHUNK = 512


def _w_in_prep_kernel(wt_ref, o_ref):
    def move(src, dst, width):
        for c0 in range(0, width, W_PREP_CHUNK):
            w = min(W_PREP_CHUNK, width - c0)
            o_ref[0, :, dst + c0:dst + c0 + w] = wt_ref[0, src + c0:src + c0 + w, :].T.astype(bf16)

    move(0, COL_GQ, SRC_GDOWN)
    move(SRC_SZ, COL_SZ, SRC_DT - SRC_SZ)
    move(SRC_AQ, COL_AQ, SRC_BR - SRC_AQ)
    move(SRC_BR, COL_BR, SRC_END - SRC_BR)

    n_gd, n_dt = 2 * GLA_RANK, 2 * SSM_HEADS
    t_gd = wt_ref[0, SRC_GDOWN:SRC_GDOWN + n_gd, :].astype(bf16)
    t_dt = wt_ref[0, SRC_DT:SRC_DT + n_dt, :].astype(bf16)
    r = lax.broadcasted_iota(jnp.int32, (SMALL_W, n_gd), 0)
    c = lax.broadcasted_iota(jnp.int32, (SMALL_W, n_gd), 1)
    sel_gd = jnp.where(r == c, 1.0, 0.0).astype(bf16)
    k = r - SMALL_DT
    for g in range(SSM_GROUPS):
        src_row = (k // SSM_HPG) * SSM_HEADS + g * SSM_HPG + k % SSM_HPG
        sel_dt = jnp.where((k >= 0) & (k < 2 * SSM_HPG) & (c == src_row), 1.0, 0.0).astype(bf16)
        blk = (jnp.dot(sel_gd, t_gd, preferred_element_type=f32)
               + jnp.dot(sel_dt, t_dt, preferred_element_type=f32))
        o_ref[0, :, COL_SMALL + g * SMALL_W:COL_SMALL + (g + 1) * SMALL_W] = blk.T.astype(bf16)


def _w_in_prep(w_in):
    wt = jnp.swapaxes(w_in, 1, 2)
    return pl.pallas_call(
        _w_in_prep_kernel,
        grid=(DEPTH, D_MODEL // W_PREP_K),
        in_specs=[pl.BlockSpec((1, SRC_END, W_PREP_K), lambda l, i: (l, 0, i))],
        out_specs=pl.BlockSpec((1, W_PREP_K, D_INP), lambda l, i: (l, i, 0)),
        out_shape=jax.ShapeDtypeStruct((DEPTH, D_MODEL, D_INP), bf16),
        compiler_params=_cparams(("arbitrary", "arbitrary")),
        name="w_in_prep",
    )(wt)


def kernel(x_prompt, x_sample, c, cache_k, cache_v, state_gla, state_ssm, c_ctx, w_mod, b_mod, ffn1_norm,
           ffn1_w_gate, ffn1_w_up, ffn1_w_down, mix_norm, w_in, gla_w_up, gla_b_up, gla_norm, ssm_conv_w,
           ssm_conv_b, ssm_dt_bias, ssm_a_log, ssm_d, ssm_norm, attn_sink, w_br_gla, w_br_ssm, w_br_attn,
           w_out, ffn2_norm, ffn2_w_gate, ffn2_w_up, ffn2_w_down, final_norm):
    x = (x_prompt.reshape(N_PROMPT, D_MODEL), x_sample.reshape(N_SAMPLE, D_MODEL))
    cvec = jnp.concatenate([c_ctx[None], c, jnp.zeros((MOD_ROWS - 1 - DEC_BATCH, D_MODEL), f32)], axis=0)
    mod = _modulation(cvec, w_mod, b_mod)

    w_in_p = _w_in_prep(w_in)
    first_w = tuple(w[:1].astype(bf16) for w in (ffn1_w_gate, ffn1_w_up, ffn1_w_down))
    last_w = tuple(w[DEPTH - 1:].astype(bf16) for w in (ffn2_w_gate, ffn2_w_up, ffn2_w_down))
    wbg, wbs, wba, wo = (w_br_gla.astype(bf16), w_br_ssm.astype(bf16), w_br_attn.astype(bf16),
                         w_out.astype(bf16))

    wz = jnp.zeros((DEPTH, 2, SMALL_W, GLA_QK), f32)
    for d in range(2):
        wz = wz.at[:, d, d * GLA_RANK:(d + 1) * GLA_RANK, :].set(gla_w_up[:, d])
    wz = wz.astype(bf16)
    bz = gla_b_up.reshape(DEPTH, 2, 1, GLA_QK)

    def dt_lanes(v):
        vg = v.reshape(DEPTH, 2, SSM_GROUPS, SSM_HPG).transpose(0, 2, 1, 3).reshape(DEPTH, SSM_GROUPS, 2 * SSM_HPG)
        row = jnp.zeros((DEPTH, SSM_GROUPS, SMALL_W), f32)
        row = row.at[:, :, SMALL_DT:SMALL_DT + 2 * SSM_HPG].set(vg)
        return row.reshape(DEPTH, 1, SSM_GROUPS * SMALL_W)

    dtb_rows = dt_lanes(ssm_dt_bias)
    a_rows = dt_lanes(-jnp.exp(ssm_a_log))
    d_skip = jnp.repeat(ssm_d, SSM_HEADDIM, axis=-1).reshape(DEPTH, 1, SSM_INNER)
    conv_w = jnp.concatenate([ssm_conv_w, jnp.zeros((DEPTH, 8 - SSM_CONV, CONV_CH), f32)], axis=1)
    conv_b = ssm_conv_b.reshape(DEPTH, 1, CONV_CH)
    sink8 = jnp.broadcast_to(attn_sink[:, :, None], (DEPTH, ATTN_HEADS, 128))
    cos, sin = _rope_tables()

    n1, nm, n2 = (w.reshape(DEPTH, 1, D_MODEL) for w in (ffn1_norm, mix_norm, ffn2_norm))
    sn = ssm_norm.reshape(DEPTH, 1, SSM_INNER)

    new_k, new_v, new_gla, new_ssm = [], [], [], []
    for l in range(DEPTH):
        mod3 = mod[l].reshape(MOD_ROWS, 1, N_MOD * D_MODEL)
        if l == 0:
            x = _ffn(x, mod3, l, 0, n1, *first_w, 0, TM, TF)
        else:
            x = _ffn(x, mod3, l, 0, n1, ffn1_w_gate, ffn1_w_up, ffn1_w_down, l, TM_BIG, TF_BIG)
        proj, aux = _inproj(x, mod3, l, nm, w_in_p)

        og_c, st_g = _gla(proj, aux, 0, BATCH, SEQ, wz[l], bz[l], gla_norm[l][None], None)
        og_s, _ = _gla(proj, aux, N_PROMPT // DEC_SEQ, DEC_BATCH, DEC_SEQ, wz[l], bz[l], gla_norm[l][None],
                       state_gla[:, l])
        os_c, st_s = _ssd(proj, aux, 0, BATCH, SEQ, conv_w[l], conv_b[l], dtb_rows[l], a_rows[l], d_skip[l], None)
        os_s, _ = _ssd(proj, aux, N_PROMPT // DEC_SEQ, DEC_BATCH, DEC_SEQ, conv_w[l], conv_b[l], dtb_rows[l],
                       a_rows[l], d_skip[l], state_ssm[:, l])
        oa_c = _attn_ctx(proj, sink8[l])
        oa_s = _attn_lat(proj, cache_k, cache_v, l, cos, sin, sink8[l])

        x = _merge(x, mod3, l, proj, (og_c, og_s), (os_c, os_s), (oa_c, oa_s), sn, wbg, wbs, wba, wo)
        if l == DEPTH - 1:
            x = _ffn(x, mod3, l, 2, n2, *last_w, 0, TM, TF, final_w=final_norm)
        else:
            x = _ffn(x, mod3, l, 2, n2, ffn2_w_gate, ffn2_w_up, ffn2_w_down, l, TM_BIG, TF_BIG)

        new_k.append(aux[:N_PROMPT, :ATTN_KV].reshape(BATCH, SEQ, KV_HEADS, HEAD_DIM))
        new_v.append(aux[:N_PROMPT, ATTN_KV:2 * ATTN_KV].reshape(BATCH, SEQ, KV_HEADS, HEAD_DIM))
        new_gla.append(st_g)
        new_ssm.append(st_s)

    y_prompt = x[0].reshape(BATCH, SEQ, D_MODEL)
    y_sample = x[1].reshape(DEC_BATCH, DEC_SEQ, D_MODEL)
    return (y_prompt, y_sample, jnp.stack(new_k, axis=1), jnp.stack(new_v, axis=1),
            jnp.stack(new_gla, axis=1), jnp.stack(new_ssm, axis=1))
```

```python
import functools
import math
from typing import NamedTuple

import numpy as np
import jax
import jax.numpy as jnp
from jax import lax
from jax.experimental import pallas as pl
from jax.experimental.pallas import tpu as pltpu

f32 = jnp.float32
bf16 = jnp.bfloat16

D_MODEL = 2048
BATCH = 32
SEQ = 256
DEPTH = 2
DEC_BATCH = 2
DEC_SEQ = 1024
PAST_LEN = 512
GRID_W = 64
RMS_EPS = 1e-6
N_MOD = 9
D_FF = 5632
GLA_HEADS = 4
GLA_DK = 128
GLA_DV = 256
GLA_RANK = 16
GLA_GATE_NORM = 16.0
GLA_CHUNK = 16
SSM_HEADS = 16
SSM_HEADDIM = 64
SSM_GROUPS = 4
SSM_HPG = SSM_HEADS // SSM_GROUPS
SSM_STATE = 128
SSM_CONV = 5
SSM_INNER = SSM_HEADS * SSM_HEADDIM
SSM_BC = SSM_GROUPS * SSM_STATE
CONV_CH = SSM_INNER + 2 * SSM_BC
ATTN_HEADS = 8
KV_HEADS = 2
Q_PER_KV = ATTN_HEADS // KV_HEADS
HEAD_DIM = 128
WINDOW = 128
ROPE_THETA = 10000.0
GLA_QK = GLA_HEADS * GLA_DK
GLA_VAL = GLA_HEADS * GLA_DV
ATTN_Q = ATTN_HEADS * HEAD_DIM
ATTN_KV = KV_HEADS * HEAD_DIM

N_PROMPT = BATCH * SEQ
N_SAMPLE = DEC_BATCH * DEC_SEQ
MOD_ROWS = 8

COL_GQ = 0
COL_GK = COL_GQ + GLA_QK
COL_GV = COL_GK + GLA_QK
COL_GR = COL_GV + GLA_VAL
COL_SZ = COL_GR + GLA_VAL
COL_SX = COL_SZ + SSM_INNER
COL_SB = COL_SX + SSM_INNER
COL_SC = COL_SB + SSM_BC
COL_AQ = COL_SC + SSM_BC
COL_AK = COL_AQ + ATTN_Q
COL_AV = COL_AK + ATTN_KV
COL_SMALL = COL_AV + ATTN_KV
SMALL_W = 128
SMALL_DT = 2 * GLA_RANK
COL_BR = 8192
D_INP = COL_BR + 3 * D_MODEL

TM = 512
TM_BIG = 1024
TF_BIG = 256
TN_IN = 1024
AUX_TILE = COL_AK // TN_IN
AUX_SMALL = (COL_SMALL - COL_AK) // SMALL_W
TN_MOD = 1024
TJ = 512
GLA_SB = 256
GLA_TILE = 128
SSD_CHUNK = 128
VMEM_LIMIT = 58 * 1024 * 1024

NT = (((1,), (1,)), ((), ()))


def _silu(x):
    return x * jax.nn.sigmoid(x)


class _Stream(NamedTuple):
    n_tok: int
    row0: int
    per_row: int

    def mod_row(self, i, tm):
        return self.row0 + (i * tm) // self.per_row


CTX = _Stream(N_PROMPT, 0, N_PROMPT)
LAT = _Stream(N_SAMPLE, 1, DEC_SEQ)


def _cparams(sem):
    return pltpu.CompilerParams(dimension_semantics=sem, vmem_limit_bytes=VMEM_LIMIT)


def _mod_kernel(c_ref, w_ref, b_ref, o_ref):
    s = _silu(c_ref[...]).astype(bf16)
    o_ref[0] = jnp.dot(s, w_ref[0].astype(bf16), preferred_element_type=f32) + b_ref[0]


def _modulation(cvec, w_mod, b_mod):
    n = N_MOD * D_MODEL
    return pl.pallas_call(
        _mod_kernel,
        grid=(DEPTH, n // TN_MOD),
        in_specs=[
            pl.BlockSpec((MOD_ROWS, D_MODEL), lambda l, j: (0, 0)),
            pl.BlockSpec((1, D_MODEL, TN_MOD), lambda l, j: (l, 0, j)),
            pl.BlockSpec((1, 1, TN_MOD), lambda l, j: (l, 0, j)),
        ],
        out_specs=pl.BlockSpec((1, MOD_ROWS, TN_MOD), lambda l, j: (l, 0, j)),
        out_shape=jax.ShapeDtypeStruct((DEPTH, MOD_ROWS, n), f32),
        compiler_params=_cparams(("arbitrary", "arbitrary")),
        name="modulation",
    )(cvec, w_mod, b_mod.reshape(DEPTH, 1, n))


def _norm_mod(x, nw, sh, sc):
    ms = jnp.mean(x * x, axis=-1, keepdims=True)
    h = x * lax.rsqrt(ms + RMS_EPS) * nw
    return h * (1.0 + sc) + sh


def _ffn_kernel(x_ref, sh_ref, sc_ref, gt_ref, nw_ref, wg_ref, wu_ref, wd_ref, *rest, final):
    if final:
        fw_ref, o_ref, h_ref = rest
    else:
        o_ref, h_ref = rest
    j = pl.program_id(1)

    @pl.when(j == 0)
    def _():
        h_ref[...] = _norm_mod(x_ref[...], nw_ref[...], sh_ref[0], sc_ref[0]).astype(bf16)
        o_ref[...] = jnp.zeros_like(o_ref)

    h = h_ref[...]
    g = jnp.dot(h, wg_ref[...].astype(bf16), preferred_element_type=f32)
    u = jnp.dot(h, wu_ref[...].astype(bf16), preferred_element_type=f32)
    a = (_silu(g) * u).astype(bf16)
    o_ref[...] += jnp.dot(a, wd_ref[...].astype(bf16), preferred_element_type=f32)

    @pl.when(j == pl.num_programs(1) - 1)
    def _():
        y = x_ref[...] + 0.5 * gt_ref[0] * o_ref[...]
        if final:
            ms = jnp.mean(y * y, axis=-1, keepdims=True)
            y = y * lax.rsqrt(ms + RMS_EPS) * fw_ref[...]
        o_ref[...] = y


def _ffn(st, x, mod3, l, slot, nw, wg, wu, wd, final_w=None):
    final = final_w is not None
    tm, tf = TM_BIG, TF_BIG

    def mod_spec(k):
        return pl.BlockSpec((1, 1, D_MODEL), lambda i, j: (st.mod_row(i, tm), 0, 3 * slot + k))

    in_specs = [
        pl.BlockSpec((tm, D_MODEL), lambda i, j: (i, 0)),
        mod_spec(0), mod_spec(1), mod_spec(2),
        pl.BlockSpec((None, 1, D_MODEL), lambda i, j: (l, 0, 0)),
        pl.BlockSpec((None, D_MODEL, tf), lambda i, j: (l, 0, j)),
        pl.BlockSpec((None, D_MODEL, tf), lambda i, j: (l, 0, j)),
        pl.BlockSpec((None, tf, D_MODEL), lambda i, j: (l, j, 0)),
    ]
    args = [x, mod3, mod3, mod3, nw, wg, wu, wd]
    if final:
        in_specs.append(pl.BlockSpec((1, D_MODEL), lambda i, j: (0, 0)))
        args.append(final_w.reshape(1, D_MODEL))
    return pl.pallas_call(
        functools.partial(_ffn_kernel, final=final),
        grid=(st.n_tok // tm, D_FF // tf),
        in_specs=in_specs,
        out_specs=pl.BlockSpec((tm, D_MODEL), lambda i, j: (i, 0)),
        out_shape=jax.ShapeDtypeStruct((st.n_tok, D_MODEL), f32),
        scratch_shapes=[pltpu.VMEM((tm, D_MODEL), bf16)],
        compiler_params=_cparams(("arbitrary", "arbitrary")),
        name="ffn_final" if final else "ffn",
    )(*args)


def _inproj_kernel(x_ref, sh_ref, sc_ref, nw_ref, w_ref, o_ref, os_ref, h_ref):
    j = pl.program_id(1)

    @pl.when(j == 0)
    def _():
        h_ref[...] = _norm_mod(x_ref[...], nw_ref[...], sh_ref[0], sc_ref[0]).astype(bf16)

    acc = jnp.dot(h_ref[...], w_ref[...], preferred_element_type=f32)
    o_ref[...] = jnp.where(j >= COL_BR // TN_IN, jax.nn.sigmoid(acc), acc).astype(bf16)

    @pl.when(j == AUX_TILE)
    def _():
        os_ref[...] = jnp.dot(h_ref[...], w_ref[...], preferred_element_type=f32)


def _inproj(st, x, mod3, l, nw, w):
    tm = TM_BIG

    def mod_spec(k):
        return pl.BlockSpec((1, 1, D_MODEL), lambda i, j: (st.mod_row(i, tm), 0, 3 + k))

    return pl.pallas_call(
        _inproj_kernel,
        grid=(st.n_tok // tm, D_INP // TN_IN),
        in_specs=[
            pl.BlockSpec((tm, D_MODEL), lambda i, j: (i, 0)),
            mod_spec(0), mod_spec(1),
            pl.BlockSpec((None, 1, D_MODEL), lambda i, j: (l, 0, 0)),
            pl.BlockSpec((None, D_MODEL, TN_IN), lambda i, j: (l, 0, j)),
        ],
        out_specs=[pl.BlockSpec((tm, TN_IN), lambda i, j: (i, j)),
                   pl.BlockSpec((tm, TN_IN), lambda i, j: (i, 0))],
        out_shape=[jax.ShapeDtypeStruct((st.n_tok, D_INP), bf16), jax.ShapeDtypeStruct((st.n_tok, TN_IN), f32)],
        scratch_shapes=[pltpu.VMEM((tm, D_MODEL), bf16)],
        compiler_params=_cparams(("arbitrary", "arbitrary")),
        name="inproj",
    )(x, mod3, mod3, nw, w)


def _split_dot(m, x):
    hi = x.astype(bf16)
    lo = (x - hi.astype(f32)).astype(bf16)
    return (jnp.dot(m, hi, preferred_element_type=f32) + jnp.dot(m, lo, preferred_element_type=f32))


def _log_sigmoid(x):
    return jnp.minimum(x, 0.0) - jnp.log1p(jnp.exp(-jnp.abs(x)))


def _gla_kernel(q_ref, k_ref, v_ref, r_ref, sm_ref, wz_ref, bz_ref, gn_ref, *rest, seq, has_h0, emit_state):
    rest = list(rest)
    h0_ref = rest.pop(0) if has_h0 else None
    o_ref = rest.pop(0)
    st_ref = rest.pop(0) if emit_state else None
    oin_ref, qe_ref, ke_ref, dec_ref, vt_ref = rest
    sb = GLA_SB
    tile = GLA_TILE
    per_tile = tile // GLA_CHUNK
    n_tile = seq // tile

    row = lax.broadcasted_iota(jnp.int32, (sb, sb), 0)
    col = lax.broadcasted_iota(jnp.int32, (sb, sb), 1)
    same = (row // GLA_CHUNK) == (col // GLA_CHUNK)
    masks = (same & (col <= row), same & (col >= row))
    tris = tuple(jnp.where(m, 1.0, 0.0).astype(bf16) for m in masks)
    blk = jnp.where(same, 1.0, 0.0).astype(bf16)

    for s in range(seq // sb):
        rows = slice(s * sb, (s + 1) * sb)
        sm = sm_ref[rows, :].astype(bf16)
        q = q_ref[rows, :].astype(f32) * (GLA_DK ** -0.5)
        k = k_ref[rows, :].astype(f32)
        v = v_ref[rows, :]
        v_t = v.astype(f32).T.astype(bf16)
        for t in range(sb // tile):
            vt_ref[s * (sb // tile) + t] = v_t[:, t * tile:(t + 1) * tile]
        for d in range(2):
            gz = jnp.dot(sm, wz_ref[d], preferred_element_type=f32) + bz_ref[d]
            la = _log_sigmoid(gz) * (1.0 / GLA_GATE_NORM)
            cum = _split_dot(tris[d], la)
            tot = _split_dot(blk, la)
            qe = q * jnp.exp(cum)
            kinv = k * jnp.exp(-cum)
            kend = k * jnp.exp(tot - cum)
            a = lax.dot_general(qe.astype(bf16), kinv.astype(bf16), NT, preferred_element_type=f32)
            a = jnp.where(masks[d], a, 0.0).astype(bf16)
            oin_ref[d, rows, :] = jnp.dot(a, v, preferred_element_type=f32)
            qe_ref[d, rows, :] = qe.astype(bf16)
            ke_ref[d, rows, :] = kend.astype(bf16)
            dec_ref[d, rows, :] = jnp.exp(tot)

    if has_h0:
        s_init = (h0_ref[0].T, h0_ref[1].T)
    else:
        s_init = (jnp.zeros((GLA_DV, GLA_DK), f32),) * 2

    def scan_tile(o, carry):
        sts = list(carry)
        for u in range(per_tile):
            for d in range(2):
                t_idx = o if d == 0 else n_tile - 1 - o
                pos = u if d == 0 else per_tile - 1 - u
                i0 = t_idx * tile + pos * GLA_CHUNK
                if not isinstance(i0, int):
                    i0 = pl.multiple_of(i0, GLA_CHUNK)
                rows = pl.ds(i0, GLA_CHUNK)
                inter = lax.dot_general(qe_ref[d, rows, :], sts[d].astype(bf16), NT, preferred_element_type=f32)
                oin_ref[d, rows, :] += inter
                pieces = [ke_ref[d, rows, :]]
                if pos:
                    pieces.insert(0, jnp.zeros((pos * GLA_CHUNK, GLA_DK), bf16))
                if pos < per_tile - 1:
                    pieces.append(jnp.zeros(((per_tile - 1 - pos) * GLA_CHUNK, GLA_DK), bf16))
                upd = jnp.dot(vt_ref[t_idx], jnp.concatenate(pieces, axis=0), preferred_element_type=f32)
                sts[d] = sts[d] * dec_ref[d, pl.ds(i0, 1), :] + upd
        return tuple(sts)

    if n_tile <= 2:
        s_fin = s_init
        for o in range(n_tile):
            s_fin = scan_tile(o, s_fin)
    else:
        s_fin = lax.fori_loop(0, n_tile, scan_tile, s_init)

    o = oin_ref[0] + oin_ref[1]
    ms = jnp.mean(o * o, axis=-1, keepdims=True)
    o = o * lax.rsqrt(ms + RMS_EPS) * gn_ref[...]
    o_ref[...] = (o * _silu(r_ref[...].astype(f32))).astype(o_ref.dtype)
    if emit_state:
        st_ref[0, 0] = s_fin[0].T
        st_ref[0, 1] = s_fin[1].T


def _gla(proj, small, row_blk0, n_seq, seq, wz, bz, gn, h0):
    has_h0 = h0 is not None
    emit_state = not has_h0
    kq, kv_ = GLA_DK, GLA_DV
    in_specs = [
        pl.BlockSpec((seq, kq), lambda b, h: (row_blk0 + b, COL_GQ // kq + h)),
        pl.BlockSpec((seq, kq), lambda b, h: (row_blk0 + b, COL_GK // kq + h)),
        pl.BlockSpec((seq, kv_), lambda b, h: (row_blk0 + b, COL_GV // kv_ + h)),
        pl.BlockSpec((seq, kv_), lambda b, h: (row_blk0 + b, COL_GR // kv_ + h)),
        pl.BlockSpec((seq, SMALL_W), lambda b, h: (row_blk0 + b, AUX_SMALL)),
        pl.BlockSpec((2, SMALL_W, kq), lambda b, h: (0, 0, h)),
        pl.BlockSpec((2, 1, kq), lambda b, h: (0, 0, h)),
        pl.BlockSpec((1, kv_), lambda b, h: (0, 0)),
    ]
    args = [proj, proj, proj, proj, small, wz, bz, gn]
    if has_h0:
        in_specs.append(pl.BlockSpec((None, 2, None, kq, kv_), lambda b, h: (b, 0, h, 0, 0)))
        args.append(h0)
    out_specs = [pl.BlockSpec((seq, kv_), lambda b, h: (b, h))]
    out_shape = [jax.ShapeDtypeStruct((n_seq * seq, GLA_VAL), bf16)]
    if emit_state:
        out_specs.append(pl.BlockSpec((1, 2, None, kq, kv_), lambda b, h: (b, 0, h, 0, 0)))
        out_shape.append(jax.ShapeDtypeStruct((n_seq, 2, GLA_HEADS, kq, kv_), f32))
    res = pl.pallas_call(
        functools.partial(_gla_kernel, seq=seq, has_h0=has_h0, emit_state=emit_state),
        grid=(n_seq, GLA_HEADS),
        in_specs=in_specs,
        out_specs=out_specs,
        out_shape=out_shape,
        scratch_shapes=[
            pltpu.VMEM((2, seq, kv_), f32),
            pltpu.VMEM((2, seq, kq), bf16),
            pltpu.VMEM((2, seq, kq), bf16),
            pltpu.VMEM((2, seq, kq), f32),
            pltpu.VMEM((seq // GLA_TILE, kv_, GLA_TILE), bf16),
        ],
        compiler_params=_cparams(("parallel", "arbitrary")),
        name="gla_lat" if has_h0 else "gla_ctx",
    )(*args)
    return res if emit_state else (res[0], None)


def _softplus(x):
    return jnp.maximum(x, 0.0) + jnp.log1p(jnp.exp(-jnp.abs(x)))


def _split3(x):
    hi = x.astype(bf16)
    r1 = x - hi.astype(f32)
    mid = r1.astype(bf16)
    lo = (r1 - mid.astype(f32)).astype(bf16)
    return hi, mid, lo


def _split3_dot(m, x):
    return sum(jnp.dot(m, t, preferred_element_type=f32) for t in _split3(x))


def _dot_sel(x, sel):
    return sum(jnp.dot(t, sel, preferred_element_type=f32) for t in _split3(x))


def _shift_rows(x, d, t_idx):
    n = x.shape[0]
    if d == 0:
        return x
    y = pltpu.roll(x, (-d) % n, 0)
    ok = (t_idx + d >= 0) & (t_idx + d < n)
    return jnp.where(ok, y, 0.0)


def _ssd_kernel(z_ref, x_ref, b_ref, c_ref, sm_ref, cwx_ref, cwb_ref, cwc_ref, cbx_ref, cbb_ref, cbc_ref,
                dtb_ref, a_ref, dsk_ref, *rest, seq, has_h0, emit_state):
    rest = list(rest)
    h0_ref = rest.pop(0) if has_h0 else None
    o_ref = rest.pop(0)
    st_ref = rest.pop(0) if emit_state else None
    y_ref, xs_ref, bm_ref, cm_ref, dt_ref = rest
    cs = SSD_CHUNK
    n_chunk = seq // cs
    hp = SSM_HPG * SSM_HEADDIM

    t_idx = lax.broadcasted_iota(jnp.int32, (seq, 1), 0)

    def conv_silu(src_ref, w_ref, bias_ref):
        xin = src_ref[...].astype(f32)
        acc = jnp.zeros_like(xin) + bias_ref[...]
        for j in range(SSM_CONV):
            acc = acc + w_ref[j:j + 1, :] * _shift_rows(xin, j - SSM_CONV // 2, t_idx)
        return _silu(acc)

    xs_ref[...] = conv_silu(x_ref, cwx_ref, cbx_ref)
    bm_ref[...] = conv_silu(b_ref, cwb_ref, cbb_ref)
    cm_ref[...] = conv_silu(c_ref, cwc_ref, cbc_ref)

    row = lax.broadcasted_iota(jnp.int32, (cs, cs), 0)
    col = lax.broadcasted_iota(jnp.int32, (cs, cs), 1)
    masks = (col <= row, col >= row)
    tris = tuple(jnp.where(m, 1.0, 0.0).astype(bf16) for m in masks)
    lane = lax.broadcasted_iota(jnp.int32, (1, SMALL_W), 1)

    dt_lane = (lane >= SMALL_DT) & (lane < SMALL_DT + 2 * SSM_HPG)
    a_row = a_ref[...]
    dt_bias = dtb_ref[...]

    def selectors(d):
        r = lax.broadcasted_iota(jnp.int32, (SMALL_W, SSM_HPG * cs), 0)
        c = lax.broadcasted_iota(jnp.int32, (SMALL_W, SSM_HPG * cs), 1)
        bcast = jnp.where(r == SMALL_DT + d * SSM_HPG + c // cs, 1.0, 0.0).astype(bf16)
        r = lax.broadcasted_iota(jnp.int32, (SMALL_W, hp), 0)
        c = lax.broadcasted_iota(jnp.int32, (SMALL_W, hp), 1)
        expand = jnp.where(r == SMALL_DT + d * SSM_HPG + c // SSM_HEADDIM, 1.0, 0.0).astype(bf16)
        return bcast, expand

    head_of_lane = lax.broadcasted_iota(jnp.int32, (cs, hp), 1) // SSM_HEADDIM
    bm_t = [bm_ref[c * cs:(c + 1) * cs, :].T.astype(bf16) for c in range(n_chunk)]

    dt_ref[...] = jnp.where(dt_lane, _softplus(sm_ref[...] + dt_bias), 0.0)
    y_ref[...] = jnp.zeros_like(y_ref)
    states = []
    for d in range(2):
        bcast, expand = selectors(d)
        if has_h0:
            st = h0_ref[d].reshape(hp, SSM_STATE).T
        else:
            st = jnp.zeros((SSM_STATE, hp), f32)
        order = range(n_chunk) if d == 0 else range(n_chunk - 1, -1, -1)
        for cidx in order:
            rows = slice(cidx * cs, (cidx + 1) * cs)
            dt = dt_ref[rows, :]
            cum = _split3_dot(tris[d], dt * a_row)
            cum_t = cum.T
            cum_b = _dot_sel(cum, bcast)
            cum_e = _dot_sel(cum, expand)
            dt_e = _dot_sel(dt, expand)
            tot_e = cum_e[0:1, :] if d == 1 else cum_e[cs - 1:cs, :]
            xs = xs_ref[rows, :]
            cm = cm_ref[rows, :].astype(bf16)
            cb = lax.dot_general(cm, bm_ref[rows, :].astype(bf16), NT, preferred_element_type=f32)
            w_parts = []
            for j in range(SSM_HPG):
                ln = SMALL_DT + d * SSM_HPG + j
                seg = jnp.exp(jnp.where(masks[d], cum_b[:, j * cs:(j + 1) * cs] - cum_t[ln:ln + 1, :], -jnp.inf))
                w_parts.append((cb * seg).astype(bf16))
            w = jnp.concatenate(w_parts, axis=1)
            xd = xs * dt_e
            xd_bd = jnp.concatenate([jnp.where(head_of_lane == j, xd, 0.0).astype(bf16)
                                     for j in range(SSM_HPG)], axis=0)
            inter = jnp.dot(cm, st.astype(bf16), preferred_element_type=f32)
            y_ref[rows, :] += jnp.dot(w, xd_bd, preferred_element_type=f32) + jnp.exp(cum_e) * inter
            xw = (xd * jnp.exp(tot_e - cum_e)).astype(bf16)
            st = st * jnp.exp(tot_e) + jnp.dot(bm_t[cidx], xw, preferred_element_type=f32)
        states.append(st)

    y = y_ref[...] + dsk_ref[...] * xs_ref[...]
    o_ref[...] = y * _silu(z_ref[...].astype(f32))
    if emit_state:
        st_ref[0, 0] = states[0].T.reshape(SSM_HPG, SSM_HEADDIM, SSM_STATE)
        st_ref[0, 1] = states[1].T.reshape(SSM_HPG, SSM_HEADDIM, SSM_STATE)


def _ssd(proj, small, row_blk0, n_seq, seq, conv_w, conv_b, dt_bias_row, a_row, d_skip, h0):
    has_h0 = h0 is not None
    emit_state = not has_h0
    hp = SSM_HPG * SSM_HEADDIM
    ns = SSM_STATE
    in_specs = [
        pl.BlockSpec((seq, hp), lambda b, g: (row_blk0 + b, COL_SZ // hp + g)),
        pl.BlockSpec((seq, hp), lambda b, g: (row_blk0 + b, COL_SX // hp + g)),
        pl.BlockSpec((seq, ns), lambda b, g: (row_blk0 + b, COL_SB // ns + g)),
        pl.BlockSpec((seq, ns), lambda b, g: (row_blk0 + b, COL_SC // ns + g)),
        pl.BlockSpec((seq, SMALL_W), lambda b, g: (row_blk0 + b, AUX_SMALL + g)),
        pl.BlockSpec((8, hp), lambda b, g: (0, g)),
        pl.BlockSpec((8, ns), lambda b, g: (0, SSM_INNER // ns + g)),
        pl.BlockSpec((8, ns), lambda b, g: (0, (SSM_INNER + SSM_BC) // ns + g)),
        pl.BlockSpec((1, hp), lambda b, g: (0, g)),
        pl.BlockSpec((1, ns), lambda b, g: (0, SSM_INNER // ns + g)),
        pl.BlockSpec((1, ns), lambda b, g: (0, (SSM_INNER + SSM_BC) // ns + g)),
        pl.BlockSpec((1, SMALL_W), lambda b, g: (0, g)),
        pl.BlockSpec((1, SMALL_W), lambda b, g: (0, g)),
        pl.BlockSpec((1, hp), lambda b, g: (0, g)),
    ]
    args = [proj, proj, proj, proj, small, conv_w, conv_w, conv_w, conv_b, conv_b, conv_b,
            dt_bias_row, a_row, d_skip]
    if has_h0:
        in_specs.append(pl.BlockSpec((None, 2, SSM_HPG, SSM_HEADDIM, ns), lambda b, g: (b, 0, g, 0, 0)))
        args.append(h0)
    out_specs = [pl.BlockSpec((seq, hp), lambda b, g: (b, g))]
    out_shape = [jax.ShapeDtypeStruct((n_seq * seq, SSM_INNER), f32)]
    if emit_state:
        out_specs.append(pl.BlockSpec((1, 2, SSM_HPG, SSM_HEADDIM, ns), lambda b, g: (b, 0, g, 0, 0)))
        out_shape.append(jax.ShapeDtypeStruct((n_seq, 2, SSM_HEADS, SSM_HEADDIM, ns), f32))
    res = pl.pallas_call(
        functools.partial(_ssd_kernel, seq=seq, has_h0=has_h0, emit_state=emit_state),
        grid=(n_seq, SSM_GROUPS),
        in_specs=in_specs,
        out_specs=out_specs,
        out_shape=out_shape,
        scratch_shapes=[
            pltpu.VMEM((seq, hp), f32),
            pltpu.VMEM((seq, hp), f32),
            pltpu.VMEM((seq, ns), f32),
            pltpu.VMEM((seq, ns), f32),
            pltpu.VMEM((seq, SMALL_W), f32),
        ],
        compiler_params=_cparams(("parallel", "arbitrary")),
        name="ssd_lat" if has_h0 else "ssd_ctx",
    )(*args)
    return res if emit_state else (res[0], None)


def _attn_ctx_kernel(q_ref, k_ref, v_ref, sink_ref, o_ref):
    for head in range(ATTN_HEADS):
        kv = slice(head // Q_PER_KV * HEAD_DIM, (head // Q_PER_KV + 1) * HEAD_DIM)
        cols = slice(head * HEAD_DIM, (head + 1) * HEAD_DIM)
        s = lax.dot_general(q_ref[:, cols], k_ref[:, kv], NT, preferred_element_type=f32) * (HEAD_DIM ** -0.5)
        sink = sink_ref[head:head + 1, 0:1]
        m = jnp.maximum(jnp.max(s, axis=-1, keepdims=True), sink)
        p = jnp.exp(s - m)
        den = jnp.sum(p, axis=-1, keepdims=True) + jnp.exp(sink - m)
        o = jnp.dot(p.astype(bf16), v_ref[:, kv], preferred_element_type=f32) / den
        o_ref[:, cols] = o.astype(o_ref.dtype)


def _attn_ctx(proj, sink8, n_seq=BATCH):
    return pl.pallas_call(
        _attn_ctx_kernel,
        grid=(n_seq,),
        in_specs=[
            pl.BlockSpec((SEQ, ATTN_Q), lambda b: (b, COL_AQ // ATTN_Q)),
            pl.BlockSpec((SEQ, ATTN_KV), lambda b: (b, COL_AK // ATTN_KV)),
            pl.BlockSpec((SEQ, ATTN_KV), lambda b: (b, COL_AV // ATTN_KV)),
            pl.BlockSpec((ATTN_HEADS, 128), lambda b: (0, 0)),
        ],
        out_specs=pl.BlockSpec((SEQ, ATTN_Q), lambda b: (b, 0)),
        out_shape=jax.ShapeDtypeStruct((n_seq * SEQ, ATTN_Q), bf16),
        compiler_params=_cparams(("arbitrary",)),
        name="attn_ctx",
    )(proj, proj, proj, sink8)


def _rope(x, cos, sin_signed):
    quarter = HEAD_DIM // 4
    lane = lax.broadcasted_iota(jnp.int32, x.shape, 1)
    first = (lane % (2 * quarter)) < quarter
    partner = jnp.where(first, pltpu.roll(x, HEAD_DIM - quarter, 1), pltpu.roll(x, quarter, 1))
    return x * cos + partner * sin_signed


def _attn_lat_kernel(q_ref, k_ref, v_ref, kc_ref, vc_ref, cos_ref, sin_ref, sink_ref, o_ref, kr_ref):
    kvh = pl.program_id(1)
    cos = cos_ref[...]
    sin = sin_ref[...]
    kr_ref[...] = _rope(k_ref[...].astype(f32), cos, sin).astype(bf16)
    kc = kc_ref[...].astype(bf16)
    vc = vc_ref[...].astype(bf16)
    blk = WINDOW
    n_blk = DEC_SEQ // blk
    scale = HEAD_DIM ** -0.5
    for i in range(n_blk):
        lo = max(i - 1, 0) * blk
        hi = min(i + 2, n_blk) * blk
        kw = kr_ref[lo:hi, :]
        vw = v_ref[lo:hi, :].astype(bf16)
        qpos = i * blk + lax.broadcasted_iota(jnp.int32, (blk, hi - lo), 0)
        kpos = lo + lax.broadcasted_iota(jnp.int32, (blk, hi - lo), 1)
        win = jnp.abs(qpos - kpos) <= WINDOW
        rows = slice(i * blk, (i + 1) * blk)
        for g in range(Q_PER_KV):
            cols = slice(g * HEAD_DIM, (g + 1) * HEAD_DIM)
            q = _rope(q_ref[rows, cols].astype(f32), cos[rows, :], sin[rows, :]).astype(bf16)
            s_c = lax.dot_general(q, kc, NT, preferred_element_type=f32) * scale
            s_w = lax.dot_general(q, kw, NT, preferred_element_type=f32) * scale
            s_w = jnp.where(win, s_w, -jnp.inf)
            sink = sink_ref[pl.ds(kvh * Q_PER_KV + g, 1), 0:1]
            m = jnp.maximum(jnp.maximum(jnp.max(s_c, axis=-1, keepdims=True),
                                        jnp.max(s_w, axis=-1, keepdims=True)), sink)
            p_c = jnp.exp(s_c - m)
            p_w = jnp.exp(s_w - m)
            den = (jnp.sum(p_c, axis=-1, keepdims=True) + jnp.sum(p_w, axis=-1, keepdims=True)
                   + jnp.exp(sink - m))
            o = (jnp.dot(p_c.astype(bf16), vc, preferred_element_type=f32)
                 + jnp.dot(p_w.astype(bf16), vw, preferred_element_type=f32)) / den
            o_ref[rows, cols] = o.astype(o_ref.dtype)


def _attn_lat(proj, cache_k, cache_v, l, cos, sin, sink8, rb0=0, n_seq=DEC_BATCH):
    qw = Q_PER_KV * HEAD_DIM
    ck = cache_k.reshape(n_seq, DEPTH, PAST_LEN, ATTN_KV)
    cv = cache_v.reshape(n_seq, DEPTH, PAST_LEN, ATTN_KV)
    return pl.pallas_call(
        _attn_lat_kernel,
        grid=(n_seq, KV_HEADS),
        in_specs=[
            pl.BlockSpec((DEC_SEQ, qw), lambda b, h: (rb0 + b, COL_AQ // qw + h)),
            pl.BlockSpec((DEC_SEQ, HEAD_DIM), lambda b, h: (rb0 + b, COL_AK // HEAD_DIM + h)),
            pl.BlockSpec((DEC_SEQ, HEAD_DIM), lambda b, h: (rb0 + b, COL_AV // HEAD_DIM + h)),
            pl.BlockSpec((None, None, PAST_LEN, HEAD_DIM), lambda b, h: (b, l, 0, h)),
            pl.BlockSpec((None, None, PAST_LEN, HEAD_DIM), lambda b, h: (b, l, 0, h)),
            pl.BlockSpec((DEC_SEQ, HEAD_DIM), lambda b, h: (0, 0)),
            pl.BlockSpec((DEC_SEQ, HEAD_DIM), lambda b, h: (0, 0)),
            pl.BlockSpec((ATTN_HEADS, 128), lambda b, h: (0, 0)),
        ],
        out_specs=pl.BlockSpec((DEC_SEQ, qw), lambda b, h: (b, h)),
        out_shape=jax.ShapeDtypeStruct((n_seq * DEC_SEQ, ATTN_Q), bf16),
        scratch_shapes=[pltpu.VMEM((DEC_SEQ, HEAD_DIM), bf16)],
        compiler_params=_cparams(("parallel", "arbitrary")),
        name="attn_lat",
    )(proj, proj, proj, ck, cv, cos, sin, sink8)


def _rope_tables():
    quarter = HEAD_DIM // 4
    freqs = ROPE_THETA ** (-np.arange(quarter, dtype=np.float32) / quarter)
    t = np.arange(DEC_SEQ)
    cos = np.zeros((DEC_SEQ, HEAD_DIM), np.float32)
    sin = np.zeros((DEC_SEQ, HEAD_DIM), np.float32)
    for half, pos in enumerate((t // GRID_W, t % GRID_W)):
        ang = pos.astype(np.float32)[:, None] * freqs[None, :]
        base = half * 2 * quarter
        cos[:, base:base + quarter] = np.cos(ang)
        cos[:, base + quarter:base + 2 * quarter] = np.cos(ang)
        sin[:, base:base + quarter] = -np.sin(ang)
        sin[:, base + quarter:base + 2 * quarter] = np.sin(ang)
    return jnp.asarray(cos), jnp.asarray(sin)


def _merge_kernel(x_ref, gt_ref, og_ref, os_ref, oa_ref, b0_ref, b1_ref, b2_ref, sn_ref,
                  wg_ref, ws_ref, wa_ref, wo_ref, o_ref, osn_ref):
    j = pl.program_id(1)

    @pl.when(j == 0)
    def _():
        y = os_ref[...]
        ms = jnp.mean(y * y, axis=-1, keepdims=True)
        osn_ref[...] = (y * lax.rsqrt(ms + RMS_EPS) * sn_ref[...]).astype(bf16)
        o_ref[...] = jnp.zeros_like(o_ref)

    m = (b0_ref[...].astype(f32) * jnp.dot(og_ref[...], wg_ref[...], preferred_element_type=f32)
         + b1_ref[...].astype(f32) * jnp.dot(osn_ref[...], ws_ref[...], preferred_element_type=f32)
         + b2_ref[...].astype(f32) * jnp.dot(oa_ref[...], wa_ref[...], preferred_element_type=f32))
    o_ref[...] += jnp.dot(m.astype(bf16), wo_ref[...], preferred_element_type=f32)

    @pl.when(j == pl.num_programs(1) - 1)
    def _():
        o_ref[...] = x_ref[...] + gt_ref[0] * o_ref[...]


def _merge(st, x, mod3, l, proj, o_gla, o_ssm, o_att, ssm_norm, w_g, w_s, w_a, w_o):
    nj = D_MODEL // TJ

    def br_spec(k):
        return pl.BlockSpec((TM, TJ), lambda i, j: (i, (COL_BR + k * D_MODEL) // TJ + j))

    def wbr_spec():
        return pl.BlockSpec((None, GLA_VAL, TJ), lambda i, j: (l, 0, j))

    return pl.pallas_call(
        _merge_kernel,
        grid=(st.n_tok // TM, nj),
        in_specs=[
            pl.BlockSpec((TM, D_MODEL), lambda i, j: (i, 0)),
            pl.BlockSpec((1, 1, D_MODEL), lambda i, j: (st.mod_row(i, TM), 0, 5)),
            pl.BlockSpec((TM, GLA_VAL), lambda i, j: (i, 0)),
            pl.BlockSpec((TM, SSM_INNER), lambda i, j: (i, 0)),
            pl.BlockSpec((TM, ATTN_Q), lambda i, j: (i, 0)),
            br_spec(0), br_spec(1), br_spec(2),
            pl.BlockSpec((None, 1, SSM_INNER), lambda i, j: (l, 0, 0)),
            wbr_spec(), wbr_spec(), wbr_spec(),
            pl.BlockSpec((None, TJ, D_MODEL), lambda i, j: (l, j, 0)),
        ],
        out_specs=pl.BlockSpec((TM, D_MODEL), lambda i, j: (i, 0)),
        out_shape=jax.ShapeDtypeStruct((st.n_tok, D_MODEL), f32),
        scratch_shapes=[pltpu.VMEM((TM, SSM_INNER), bf16)],
        compiler_params=_cparams(("arbitrary", "arbitrary")),
        name="merge",
    )(x, mod3, o_gla, o_ssm, o_att, proj, proj, proj, ssm_norm, w_g, w_s, w_a, w_o)


_IN_SRC = np.cumsum((0, GLA_QK, GLA_QK, GLA_VAL, GLA_VAL, 2 * GLA_RANK, SSM_INNER, CONV_CH, 2 * SSM_HEADS,
                     ATTN_Q, ATTN_KV, ATTN_KV, 3 * D_MODEL))
SRC_GDOWN, SRC_SZ, SRC_DT, SRC_AQ, SRC_BR, SRC_END = (int(_IN_SRC[k]) for k in (4, 5, 7, 8, 11, 12))
W_PREP_K = 256
W_PREP_CHUNK = 512


def _w_in_prep_kernel(wt_ref, o_ref):
    def move(src, dst, width):
        for c0 in range(0, width, W_PREP_CHUNK):
            w = min(W_PREP_CHUNK, width - c0)
            o_ref[0, :, dst + c0:dst + c0 + w] = wt_ref[0, src + c0:src + c0 + w, :].T.astype(bf16)

    move(0, COL_GQ, SRC_GDOWN)
    move(SRC_SZ, COL_SZ, SRC_DT - SRC_SZ)
    move(SRC_AQ, COL_AQ, SRC_BR - SRC_AQ)
    move(SRC_BR, COL_BR, SRC_END - SRC_BR)

    n_gd, n_dt = 2 * GLA_RANK, 2 * SSM_HEADS
    t_gd = wt_ref[0, SRC_GDOWN:SRC_GDOWN + n_gd, :].astype(bf16)
    t_dt = wt_ref[0, SRC_DT:SRC_DT + n_dt, :].astype(bf16)
    r = lax.broadcasted_iota(jnp.int32, (SMALL_W, n_gd), 0)
    c = lax.broadcasted_iota(jnp.int32, (SMALL_W, n_gd), 1)
    sel_gd = jnp.where(r == c, 1.0, 0.0).astype(bf16)
    k = r - SMALL_DT
    for g in range(SSM_GROUPS):
        src_row = (k // SSM_HPG) * SSM_HEADS + g * SSM_HPG + k % SSM_HPG
        sel_dt = jnp.where((k >= 0) & (k < 2 * SSM_HPG) & (c == src_row), 1.0, 0.0).astype(bf16)
        blk = (jnp.dot(sel_gd, t_gd, preferred_element_type=f32)
               + jnp.dot(sel_dt, t_dt, preferred_element_type=f32))
        o_ref[0, :, COL_SMALL + g * SMALL_W:COL_SMALL + (g + 1) * SMALL_W] = blk.T.astype(bf16)


def _w_in_prep(w_in):
    wt = jnp.swapaxes(w_in, 1, 2)
    return pl.pallas_call(
        _w_in_prep_kernel,
        grid=(DEPTH, D_MODEL // W_PREP_K),
        in_specs=[pl.BlockSpec((1, SRC_END, W_PREP_K), lambda l, i: (l, 0, i))],
        out_specs=pl.BlockSpec((1, W_PREP_K, D_INP), lambda l, i: (l, i, 0)),
        out_shape=jax.ShapeDtypeStruct((DEPTH, D_MODEL, D_INP), bf16),
        compiler_params=_cparams(("arbitrary", "arbitrary")),
        name="w_in_prep",
    )(wt)


def kernel(x_prompt, x_sample, c, cache_k, cache_v, state_gla, state_ssm, c_ctx, w_mod, b_mod, ffn1_norm,
           ffn1_w_gate, ffn1_w_up, ffn1_w_down, mix_norm, w_in, gla_w_up, gla_b_up, gla_norm, ssm_conv_w,
           ssm_conv_b, ssm_dt_bias, ssm_a_log, ssm_d, ssm_norm, attn_sink, w_br_gla, w_br_ssm, w_br_attn,
           w_out, ffn2_norm, ffn2_w_gate, ffn2_w_up, ffn2_w_down, final_norm):
    xs = {CTX: x_prompt.reshape(N_PROMPT, D_MODEL), LAT: x_sample.reshape(N_SAMPLE, D_MODEL)}
    cvec = jnp.concatenate([c_ctx[None], c, jnp.zeros((MOD_ROWS - 1 - DEC_BATCH, D_MODEL), f32)], axis=0)
    mod = _modulation(cvec, w_mod, b_mod)

    w_in_p = _w_in_prep(w_in)
    wbg, wbs, wba, wo = (w_br_gla.astype(bf16), w_br_ssm.astype(bf16), w_br_attn.astype(bf16),
                         w_out.astype(bf16))

    wz = jnp.zeros((DEPTH, 2, SMALL_W, GLA_QK), f32)
    for d in range(2):
        wz = wz.at[:, d, d * GLA_RANK:(d + 1) * GLA_RANK, :].set(gla_w_up[:, d])
    wz = wz.astype(bf16)
    bz = gla_b_up.reshape(DEPTH, 2, 1, GLA_QK)

    def dt_lanes(v):
        vg = v.reshape(DEPTH, 2, SSM_GROUPS, SSM_HPG).transpose(0, 2, 1, 3).reshape(DEPTH, SSM_GROUPS, 2 * SSM_HPG)
        row = jnp.zeros((DEPTH, SSM_GROUPS, SMALL_W), f32)
        row = row.at[:, :, SMALL_DT:SMALL_DT + 2 * SSM_HPG].set(vg)
        return row.reshape(DEPTH, 1, SSM_GROUPS * SMALL_W)

    dtb_rows = dt_lanes(ssm_dt_bias)
    a_rows = dt_lanes(-jnp.exp(ssm_a_log))
    d_skip = jnp.repeat(ssm_d, SSM_HEADDIM, axis=-1).reshape(DEPTH, 1, SSM_INNER)
    conv_w = jnp.concatenate([ssm_conv_w, jnp.zeros((DEPTH, 8 - SSM_CONV, CONV_CH), f32)], axis=1)
    conv_b = ssm_conv_b.reshape(DEPTH, 1, CONV_CH)
    sink8 = jnp.broadcast_to(attn_sink[:, :, None], (DEPTH, ATTN_HEADS, 128))
    cos, sin = _rope_tables()

    n1, nm, n2 = (w.reshape(DEPTH, 1, D_MODEL) for w in (ffn1_norm, mix_norm, ffn2_norm))
    sn = ssm_norm.reshape(DEPTH, 1, SSM_INNER)

    new_k, new_v, new_gla, new_ssm = [], [], [], []
    for l in range(DEPTH):
        mod3 = mod[l].reshape(MOD_ROWS, 1, N_MOD * D_MODEL)
        last = l == DEPTH - 1
        for st in (CTX, LAT):
            x = _ffn(st, xs[st], mod3, l, 0, n1, ffn1_w_gate, ffn1_w_up, ffn1_w_down)
            proj, aux = _inproj(st, x, mod3, l, nm, w_in_p)
            if st is CTX:
                o_gla, st_g = _gla(proj, aux, 0, BATCH, SEQ, wz[l], bz[l], gla_norm[l][None], None)
                o_ssm, st_s = _ssd(proj, aux, 0, BATCH, SEQ, conv_w[l], conv_b[l], dtb_rows[l], a_rows[l],
                                   d_skip[l], None)
                o_att = _attn_ctx(proj, sink8[l])
                new_k.append(aux[:, :ATTN_KV].reshape(BATCH, SEQ, KV_HEADS, HEAD_DIM))
                new_v.append(aux[:, ATTN_KV:2 * ATTN_KV].reshape(BATCH, SEQ, KV_HEADS, HEAD_DIM))
                new_gla.append(st_g)
                new_ssm.append(st_s)
            else:
                o_gla, _ = _gla(proj, aux, 0, DEC_BATCH, DEC_SEQ, wz[l], bz[l], gla_norm[l][None], state_gla[:, l])
                o_ssm, _ = _ssd(proj, aux, 0, DEC_BATCH, DEC_SEQ, conv_w[l], conv_b[l], dtb_rows[l], a_rows[l],
                                d_skip[l], state_ssm[:, l])
                o_att = _attn_lat(proj, cache_k, cache_v, l, cos, sin, sink8[l])
            x = _merge(st, x, mod3, l, proj, o_gla, o_ssm, o_att, sn, wbg, wbs, wba, wo)
            xs[st] = _ffn(st, x, mod3, l, 2, n2, ffn2_w_gate, ffn2_w_up, ffn2_w_down,
                          final_w=final_norm if last else None)

    y_prompt = xs[CTX].reshape(BATCH, SEQ, D_MODEL)
    y_sample = xs[LAT].reshape(DEC_BATCH, DEC_SEQ, D_MODEL)
    return (y_prompt, y_sample, jnp.stack(new_k, axis=1), jnp.stack(new_v, axis=1),
            jnp.stack(new_gla, axis=1), jnp.stack(new_ssm, axis=1))
```

```python
import functools
import math
from typing import NamedTuple

import numpy as np
import jax
import jax.numpy as jnp
from jax import lax
from jax.experimental import pallas as pl
from jax.experimental.pallas import tpu as pltpu

f32 = jnp.float32
bf16 = jnp.bfloat16

D_MODEL = 2048
BATCH = 32
SEQ = 256
DEPTH = 2
DEC_BATCH = 2
DEC_SEQ = 1024
PAST_LEN = 512
GRID_W = 64
RMS_EPS = 1e-6
N_MOD = 9
D_FF = 5632
GLA_HEADS = 4
GLA_DK = 128
GLA_DV = 256
GLA_RANK = 16
GLA_GATE_NORM = 16.0
GLA_CHUNK = 16
SSM_HEADS = 16
SSM_HEADDIM = 64
SSM_GROUPS = 4
SSM_HPG = SSM_HEADS // SSM_GROUPS
SSM_STATE = 128
SSM_CONV = 5
SSM_INNER = SSM_HEADS * SSM_HEADDIM
SSM_BC = SSM_GROUPS * SSM_STATE
CONV_CH = SSM_INNER + 2 * SSM_BC
ATTN_HEADS = 8
KV_HEADS = 2
Q_PER_KV = ATTN_HEADS // KV_HEADS
HEAD_DIM = 128
WINDOW = 128
ROPE_THETA = 10000.0
GLA_QK = GLA_HEADS * GLA_DK
GLA_VAL = GLA_HEADS * GLA_DV
ATTN_Q = ATTN_HEADS * HEAD_DIM
ATTN_KV = KV_HEADS * HEAD_DIM

N_PROMPT = BATCH * SEQ
N_SAMPLE = DEC_BATCH * DEC_SEQ
MOD_ROWS = 8

COL_GQ = 0
COL_GK = COL_GQ + GLA_QK
COL_GV = COL_GK + GLA_QK
COL_GR = COL_GV + GLA_VAL
COL_SZ = COL_GR + GLA_VAL
COL_SX = COL_SZ + SSM_INNER
COL_SB = COL_SX + SSM_INNER
COL_SC = COL_SB + SSM_BC
COL_AQ = COL_SC + SSM_BC
COL_AK = COL_AQ + ATTN_Q
COL_AV = COL_AK + ATTN_KV
COL_SMALL = COL_AV + ATTN_KV
SMALL_W = 128
SMALL_DT = 2 * GLA_RANK
COL_BR = 8192
D_INP = COL_BR + 3 * D_MODEL

TM = 512
TM_BIG = 1024
TF_BIG = 256
TN_IN = 1024
AUX_TILE = COL_AK // TN_IN
AUX_SMALL = (COL_SMALL - COL_AK) // SMALL_W
TN_MOD = 1024
TJ = 512
GLA_SB = 256
GLA_TILE = 128
GLA_HPS = 2
SSD_CHUNK = 128
VMEM_LIMIT = 58 * 1024 * 1024

NT = (((1,), (1,)), ((), ()))


def _silu(x):
    return x * jax.nn.sigmoid(x)


class _Stream(NamedTuple):
    n_tok: int
    row0: int
    per_row: int

    def mod_row(self, i, tm):
        return self.row0 + (i * tm) // self.per_row


CTX = _Stream(N_PROMPT, 0, N_PROMPT)
LAT = _Stream(N_SAMPLE, 1, DEC_SEQ)


def _cparams(sem):
    return pltpu.CompilerParams(dimension_semantics=sem, vmem_limit_bytes=VMEM_LIMIT)


def _mod_kernel(c_ref, w_ref, b_ref, o_ref):
    s = _silu(c_ref[...]).astype(bf16)
    o_ref[0] = jnp.dot(s, w_ref[0].astype(bf16), preferred_element_type=f32) + b_ref[0]


def _modulation(cvec, w_mod, b_mod):
    n = N_MOD * D_MODEL
    return pl.pallas_call(
        _mod_kernel,
        grid=(DEPTH, n // TN_MOD),
        in_specs=[
            pl.BlockSpec((MOD_ROWS, D_MODEL), lambda l, j: (0, 0)),
            pl.BlockSpec((1, D_MODEL, TN_MOD), lambda l, j: (l, 0, j)),
            pl.BlockSpec((1, 1, TN_MOD), lambda l, j: (l, 0, j)),
        ],
        out_specs=pl.BlockSpec((1, MOD_ROWS, TN_MOD), lambda l, j: (l, 0, j)),
        out_shape=jax.ShapeDtypeStruct((DEPTH, MOD_ROWS, n), f32),
        compiler_params=_cparams(("arbitrary", "arbitrary")),
        name="modulation",
    )(cvec, w_mod, b_mod.reshape(DEPTH, 1, n))


def _norm_mod(x, nw, sh, sc):
    ms = jnp.mean(x * x, axis=-1, keepdims=True)
    h = x * lax.rsqrt(ms + RMS_EPS) * nw
    return h * (1.0 + sc) + sh


def _ffn_kernel(x_ref, sh_ref, sc_ref, gt_ref, nw_ref, wg_ref, wu_ref, wd_ref, *rest, final):
    if final:
        fw_ref, o_ref, h_ref = rest
    else:
        o_ref, h_ref = rest
    j = pl.program_id(1)

    @pl.when(j == 0)
    def _():
        h_ref[...] = _norm_mod(x_ref[...], nw_ref[...], sh_ref[0], sc_ref[0]).astype(bf16)
        o_ref[...] = jnp.zeros_like(o_ref)

    h = h_ref[...]
    g = jnp.dot(h, wg_ref[...].astype(bf16), preferred_element_type=f32)
    u = jnp.dot(h, wu_ref[...].astype(bf16), preferred_element_type=f32)
    a = (_silu(g) * u).astype(bf16)
    o_ref[...] += jnp.dot(a, wd_ref[...].astype(bf16), preferred_element_type=f32)

    @pl.when(j == pl.num_programs(1) - 1)
    def _():
        y = x_ref[...] + 0.5 * gt_ref[0] * o_ref[...]
        if final:
            ms = jnp.mean(y * y, axis=-1, keepdims=True)
            y = y * lax.rsqrt(ms + RMS_EPS) * fw_ref[...]
        o_ref[...] = y


def _ffn(st, x, mod3, l, slot, nw, wg, wu, wd, final_w=None):
    final = final_w is not None
    tm, tf = TM_BIG, TF_BIG

    def mod_spec(k):
        return pl.BlockSpec((1, 1, D_MODEL), lambda i, j: (st.mod_row(i, tm), 0, 3 * slot + k))

    in_specs = [
        pl.BlockSpec((tm, D_MODEL), lambda i, j: (i, 0)),
        mod_spec(0), mod_spec(1), mod_spec(2),
        pl.BlockSpec((None, 1, D_MODEL), lambda i, j: (l, 0, 0)),
        pl.BlockSpec((None, D_MODEL, tf), lambda i, j: (l, 0, j)),
        pl.BlockSpec((None, D_MODEL, tf), lambda i, j: (l, 0, j)),
        pl.BlockSpec((None, tf, D_MODEL), lambda i, j: (l, j, 0)),
    ]
    args = [x, mod3, mod3, mod3, nw, wg, wu, wd]
    if final:
        in_specs.append(pl.BlockSpec((1, D_MODEL), lambda i, j: (0, 0)))
        args.append(final_w.reshape(1, D_MODEL))
    return pl.pallas_call(
        functools.partial(_ffn_kernel, final=final),
        grid=(st.n_tok // tm, D_FF // tf),
        in_specs=in_specs,
        out_specs=pl.BlockSpec((tm, D_MODEL), lambda i, j: (i, 0)),
        out_shape=jax.ShapeDtypeStruct((st.n_tok, D_MODEL), f32),
        scratch_shapes=[pltpu.VMEM((tm, D_MODEL), bf16)],
        compiler_params=_cparams(("arbitrary", "arbitrary")),
        name="ffn_final" if final else "ffn",
    )(*args)


def _inproj_kernel(x_ref, sh_ref, sc_ref, nw_ref, w_ref, o_ref, os_ref, h_ref):
    j = pl.program_id(1)

    @pl.when(j == 0)
    def _():
        h_ref[...] = _norm_mod(x_ref[...], nw_ref[...], sh_ref[0], sc_ref[0]).astype(bf16)

    def tile():
        return jnp.dot(h_ref[...], w_ref[...], preferred_element_type=f32)

    @pl.when(j < COL_BR // TN_IN)
    def _():
        o_ref[...] = tile().astype(bf16)

    @pl.when(j >= COL_BR // TN_IN)
    def _():
        o_ref[...] = jax.nn.sigmoid(tile()).astype(bf16)

    @pl.when(j == AUX_TILE)
    def _():
        os_ref[...] = tile()


def _inproj(st, x, mod3, l, nw, w):
    tm = TM_BIG

    def mod_spec(k):
        return pl.BlockSpec((1, 1, D_MODEL), lambda i, j: (st.mod_row(i, tm), 0, 3 + k))

    return pl.pallas_call(
        _inproj_kernel,
        grid=(st.n_tok // tm, D_INP // TN_IN),
        in_specs=[
            pl.BlockSpec((tm, D_MODEL), lambda i, j: (i, 0)),
            mod_spec(0), mod_spec(1),
            pl.BlockSpec((None, 1, D_MODEL), lambda i, j: (l, 0, 0)),
            pl.BlockSpec((None, D_MODEL, TN_IN), lambda i, j: (l, 0, j)),
        ],
        out_specs=[pl.BlockSpec((tm, TN_IN), lambda i, j: (i, j)),
                   pl.BlockSpec((tm, TN_IN), lambda i, j: (i, 0))],
        out_shape=[jax.ShapeDtypeStruct((st.n_tok, D_INP), bf16), jax.ShapeDtypeStruct((st.n_tok, TN_IN), f32)],
        scratch_shapes=[pltpu.VMEM((tm, D_MODEL), bf16)],
        compiler_params=_cparams(("arbitrary", "arbitrary")),
        name="inproj",
    )(x, mod3, mod3, nw, w)


def _split_dot(m, x):
    hi = x.astype(bf16)
    lo = (x - hi.astype(f32)).astype(bf16)
    return (jnp.dot(m, hi, preferred_element_type=f32) + jnp.dot(m, lo, preferred_element_type=f32))


def _log_sigmoid(x):
    return jnp.minimum(x, 0.0) - jnp.log1p(jnp.exp(-jnp.abs(x)))


def _gla_kernel(q_ref, k_ref, v_ref, r_ref, sm_ref, wz_ref, bz_ref, gn_ref, *rest, seq, has_h0, emit_state):
    rest = list(rest)
    h0_ref = rest.pop(0) if has_h0 else None
    o_ref = rest.pop(0)
    st_ref = rest.pop(0) if emit_state else None
    oin_ref, qe_ref, ke_ref, dec_ref, vt_ref = rest
    hps = GLA_HPS
    sb = GLA_SB
    tile = GLA_TILE
    per_tile = tile // GLA_CHUNK
    n_tile = seq // tile

    row = lax.broadcasted_iota(jnp.int32, (sb, sb), 0)
    col = lax.broadcasted_iota(jnp.int32, (sb, sb), 1)
    same = (row // GLA_CHUNK) == (col // GLA_CHUNK)
    masks = (same & (col <= row), same & (col >= row))
    tris = tuple(jnp.where(m, 1.0, 0.0).astype(bf16) for m in masks)
    blk = jnp.where(same, 1.0, 0.0).astype(bf16)

    for s in range(seq // sb):
        rows = slice(s * sb, (s + 1) * sb)
        sm = sm_ref[rows, :].astype(bf16)
        for hh in range(hps):
            kcols = slice(hh * GLA_DK, (hh + 1) * GLA_DK)
            vcols = slice(hh * GLA_DV, (hh + 1) * GLA_DV)
            q = q_ref[rows, kcols].astype(f32) * (GLA_DK ** -0.5)
            k = k_ref[rows, kcols].astype(f32)
            v = v_ref[rows, vcols]
            v_t = v.astype(f32).T.astype(bf16)
            for t in range(sb // tile):
                vt_ref[hh, s * (sb // tile) + t] = v_t[:, t * tile:(t + 1) * tile]
            for d in range(2):
                gz = jnp.dot(sm, wz_ref[d, :, kcols], preferred_element_type=f32) + bz_ref[d, :, kcols]
                la = _log_sigmoid(gz) * (1.0 / GLA_GATE_NORM)
                cum = _split_dot(tris[d], la)
                tot = _split_dot(blk, la)
                qe = q * jnp.exp(cum)
                kinv = k * jnp.exp(-cum)
                kend = k * jnp.exp(tot - cum)
                a = lax.dot_general(qe.astype(bf16), kinv.astype(bf16), NT, preferred_element_type=f32)
                a = jnp.where(masks[d], a, 0.0).astype(bf16)
                oin_ref[hh, d, rows, :] = jnp.dot(a, v, preferred_element_type=f32)
                qe_ref[hh, d, rows, :] = qe.astype(bf16)
                ke_ref[hh, d, rows, :] = kend.astype(bf16)
                dec_ref[hh, d, rows, :] = jnp.exp(tot)

    chains = [(hh, d) for hh in range(hps) for d in range(2)]
    if has_h0:
        s_init = tuple(h0_ref[d, hh].T for hh, d in chains)
    else:
        s_init = (jnp.zeros((GLA_DV, GLA_DK), f32),) * len(chains)

    def scan_tile(o, carry):
        sts = list(carry)
        for u in range(per_tile):
            for ci, (hh, d) in enumerate(chains):
                t_idx = o if d == 0 else n_tile - 1 - o
                pos = u if d == 0 else per_tile - 1 - u
                i0 = t_idx * tile + pos * GLA_CHUNK
                if not isinstance(i0, int):
                    i0 = pl.multiple_of(i0, GLA_CHUNK)
                rows = pl.ds(i0, GLA_CHUNK)
                inter = lax.dot_general(qe_ref[hh, d, rows, :], sts[ci].astype(bf16), NT,
                                        preferred_element_type=f32)
                oin_ref[hh, d, rows, :] += inter
                pieces = [ke_ref[hh, d, rows, :]]
                if pos:
                    pieces.insert(0, jnp.zeros((pos * GLA_CHUNK, GLA_DK), bf16))
                if pos < per_tile - 1:
                    pieces.append(jnp.zeros(((per_tile - 1 - pos) * GLA_CHUNK, GLA_DK), bf16))
                upd = jnp.dot(vt_ref[hh, t_idx], jnp.concatenate(pieces, axis=0), preferred_element_type=f32)
                sts[ci] = sts[ci] * dec_ref[hh, d, pl.ds(i0, 1), :] + upd
        return tuple(sts)

    if n_tile <= 2:
        s_fin = s_init
        for o in range(n_tile):
            s_fin = scan_tile(o, s_fin)
    else:
        s_fin = lax.fori_loop(0, n_tile, scan_tile, s_init)

    for hh in range(hps):
        vcols = slice(hh * GLA_DV, (hh + 1) * GLA_DV)
        o = oin_ref[hh, 0] + oin_ref[hh, 1]
        ms = jnp.mean(o * o, axis=-1, keepdims=True)
        o = o * lax.rsqrt(ms + RMS_EPS) * gn_ref[...]
        o_ref[:, vcols] = (o * _silu(r_ref[:, vcols].astype(f32))).astype(o_ref.dtype)
    if emit_state:
        for ci, (hh, d) in enumerate(chains):
            st_ref[0, d, hh] = s_fin[ci].T


def _gla(proj, small, row_blk0, n_seq, seq, wz, bz, gn, h0):
    has_h0 = h0 is not None
    emit_state = not has_h0
    kq, kv_, hps = GLA_DK, GLA_DV, GLA_HPS
    kw, vw = hps * kq, hps * kv_
    in_specs = [
        pl.BlockSpec((seq, kw), lambda b, h: (row_blk0 + b, COL_GQ // kw + h)),
        pl.BlockSpec((seq, kw), lambda b, h: (row_blk0 + b, COL_GK // kw + h)),
        pl.BlockSpec((seq, vw), lambda b, h: (row_blk0 + b, COL_GV // vw + h)),
        pl.BlockSpec((seq, vw), lambda b, h: (row_blk0 + b, COL_GR // vw + h)),
        pl.BlockSpec((seq, SMALL_W), lambda b, h: (row_blk0 + b, AUX_SMALL)),
        pl.BlockSpec((2, SMALL_W, kw), lambda b, h: (0, 0, h)),
        pl.BlockSpec((2, 1, kw), lambda b, h: (0, 0, h)),
        pl.BlockSpec((1, kv_), lambda b, h: (0, 0)),
    ]
    args = [proj, proj, proj, proj, small, wz, bz, gn]
    if has_h0:
        in_specs.append(pl.BlockSpec((None, 2, hps, kq, kv_), lambda b, h: (b, 0, h, 0, 0)))
        args.append(h0)
    out_specs = [pl.BlockSpec((seq, vw), lambda b, h: (b, h))]
    out_shape = [jax.ShapeDtypeStruct((n_seq * seq, GLA_VAL), bf16)]
    if emit_state:
        out_specs.append(pl.BlockSpec((1, 2, hps, kq, kv_), lambda b, h: (b, 0, h, 0, 0)))
        out_shape.append(jax.ShapeDtypeStruct((n_seq, 2, GLA_HEADS, kq, kv_), f32))
    res = pl.pallas_call(
        functools.partial(_gla_kernel, seq=seq, has_h0=has_h0, emit_state=emit_state),
        grid=(n_seq, GLA_HEADS // hps),
        in_specs=in_specs,
        out_specs=out_specs,
        out_shape=out_shape,
        scratch_shapes=[
            pltpu.VMEM((hps, 2, seq, kv_), f32),
            pltpu.VMEM((hps, 2, seq, kq), bf16),
            pltpu.VMEM((hps, 2, seq, kq), bf16),
            pltpu.VMEM((hps, 2, seq, kq), f32),
            pltpu.VMEM((hps, seq // GLA_TILE, kv_, GLA_TILE), bf16),
        ],
        compiler_params=_cparams(("parallel", "arbitrary")),
        name="gla_lat" if has_h0 else "gla_ctx",
    )(*args)
    return res if emit_state else (res[0], None)


def _softplus(x):
    return jnp.maximum(x, 0.0) + jnp.log1p(jnp.exp(-jnp.abs(x)))


def _split3(x):
    hi = x.astype(bf16)
    r1 = x - hi.astype(f32)
    mid = r1.astype(bf16)
    lo = (r1 - mid.astype(f32)).astype(bf16)
    return hi, mid, lo


def _split3_dot(m, x):
    return sum(jnp.dot(m, t, preferred_element_type=f32) for t in _split3(x))


def _dot_sel(x, sel):
    return sum(jnp.dot(t, sel, preferred_element_type=f32) for t in _split3(x))


def _shift_rows(x, d, t_idx):
    n = x.shape[0]
    if d == 0:
        return x
    y = pltpu.roll(x, (-d) % n, 0)
    ok = (t_idx + d >= 0) & (t_idx + d < n)
    return jnp.where(ok, y, 0.0)


def _ssd_kernel(z_ref, x_ref, b_ref, c_ref, sm_ref, cwx_ref, cwb_ref, cwc_ref, cbx_ref, cbb_ref, cbc_ref,
                dtb_ref, a_ref, dsk_ref, *rest, seq, has_h0, emit_state):
    rest = list(rest)
    h0_ref = rest.pop(0) if has_h0 else None
    o_ref = rest.pop(0)
    st_ref = rest.pop(0) if emit_state else None
    y_ref, xs_ref, bm_ref, cm_ref, dt_ref = rest
    cs = SSD_CHUNK
    n_chunk = seq // cs
    hp = SSM_HPG * SSM_HEADDIM

    t_idx = lax.broadcasted_iota(jnp.int32, (seq, 1), 0)

    def conv_silu(src_ref, w_ref, bias_ref):
        xin = src_ref[...].astype(f32)
        acc = jnp.zeros_like(xin) + bias_ref[...]
        for j in range(SSM_CONV):
            acc = acc + w_ref[j:j + 1, :] * _shift_rows(xin, j - SSM_CONV // 2, t_idx)
        return _silu(acc)

    xs_ref[...] = conv_silu(x_ref, cwx_ref, cbx_ref)
    bm_ref[...] = conv_silu(b_ref, cwb_ref, cbb_ref)
    cm_ref[...] = conv_silu(c_ref, cwc_ref, cbc_ref)

    row = lax.broadcasted_iota(jnp.int32, (cs, cs), 0)
    col = lax.broadcasted_iota(jnp.int32, (cs, cs), 1)
    masks = (col <= row, col >= row)
    tris = tuple(jnp.where(m, 1.0, 0.0).astype(bf16) for m in masks)
    lane = lax.broadcasted_iota(jnp.int32, (1, SMALL_W), 1)

    dt_lane = (lane >= SMALL_DT) & (lane < SMALL_DT + 2 * SSM_HPG)
    a_row = a_ref[...]
    dt_bias = dtb_ref[...]

    def selectors(d):
        r = lax.broadcasted_iota(jnp.int32, (SMALL_W, SSM_HPG * cs), 0)
        c = lax.broadcasted_iota(jnp.int32, (SMALL_W, SSM_HPG * cs), 1)
        bcast = jnp.where(r == SMALL_DT + d * SSM_HPG + c // cs, 1.0, 0.0).astype(bf16)
        r = lax.broadcasted_iota(jnp.int32, (SMALL_W, hp), 0)
        c = lax.broadcasted_iota(jnp.int32, (SMALL_W, hp), 1)
        expand = jnp.where(r == SMALL_DT + d * SSM_HPG + c // SSM_HEADDIM, 1.0, 0.0).astype(bf16)
        return bcast, expand

    head_of_lane = lax.broadcasted_iota(jnp.int32, (cs, hp), 1) // SSM_HEADDIM
    bm_t = [bm_ref[c * cs:(c + 1) * cs, :].T.astype(bf16) for c in range(n_chunk)]

    dt_ref[...] = jnp.where(dt_lane, _softplus(sm_ref[...] + dt_bias), 0.0)
    y_ref[...] = jnp.zeros_like(y_ref)
    states = []
    for d in range(2):
        bcast, expand = selectors(d)
        if has_h0:
            st = h0_ref[d].reshape(hp, SSM_STATE).T
        else:
            st = jnp.zeros((SSM_STATE, hp), f32)
        order = range(n_chunk) if d == 0 else range(n_chunk - 1, -1, -1)
        for cidx in order:
            rows = slice(cidx * cs, (cidx + 1) * cs)
            dt = dt_ref[rows, :]
            cum = _split3_dot(tris[d], dt * a_row)
            cum_t = cum.T
            cum_b = _dot_sel(cum, bcast)
            cum_e = _dot_sel(cum, expand)
            dt_e = _dot_sel(dt, expand)
            tot_e = cum_e[0:1, :] if d == 1 else cum_e[cs - 1:cs, :]
            xs = xs_ref[rows, :]
            cm = cm_ref[rows, :].astype(bf16)
            cb = lax.dot_general(cm, bm_ref[rows, :].astype(bf16), NT, preferred_element_type=f32)
            w_parts = []
            for j in range(SSM_HPG):
                ln = SMALL_DT + d * SSM_HPG + j
                seg = jnp.exp(jnp.where(masks[d], cum_b[:, j * cs:(j + 1) * cs] - cum_t[ln:ln + 1, :], -jnp.inf))
                w_parts.append((cb * seg).astype(bf16))
            w = jnp.concatenate(w_parts, axis=1)
            xd = xs * dt_e
            xd_bd = jnp.concatenate([jnp.where(head_of_lane == j, xd, 0.0).astype(bf16)
                                     for j in range(SSM_HPG)], axis=0)
            inter = jnp.dot(cm, st.astype(bf16), preferred_element_type=f32)
            y_ref[rows, :] += jnp.dot(w, xd_bd, preferred_element_type=f32) + jnp.exp(cum_e) * inter
            xw = (xd * jnp.exp(tot_e - cum_e)).astype(bf16)
            st = st * jnp.exp(tot_e) + jnp.dot(bm_t[cidx], xw, preferred_element_type=f32)
        states.append(st)

    y = y_ref[...] + dsk_ref[...] * xs_ref[...]
    o_ref[...] = y * _silu(z_ref[...].astype(f32))
    if emit_state:
        st_ref[0, 0] = states[0].T.reshape(SSM_HPG, SSM_HEADDIM, SSM_STATE)
        st_ref[0, 1] = states[1].T.reshape(SSM_HPG, SSM_HEADDIM, SSM_STATE)


def _ssd(proj, small, row_blk0, n_seq, seq, conv_w, conv_b, dt_bias_row, a_row, d_skip, h0):
    has_h0 = h0 is not None
    emit_state = not has_h0
    hp = SSM_HPG * SSM_HEADDIM
    ns = SSM_STATE
    in_specs = [
        pl.BlockSpec((seq, hp), lambda b, g: (row_blk0 + b, COL_SZ // hp + g)),
        pl.BlockSpec((seq, hp), lambda b, g: (row_blk0 + b, COL_SX // hp + g)),
        pl.BlockSpec((seq, ns), lambda b, g: (row_blk0 + b, COL_SB // ns + g)),
        pl.BlockSpec((seq, ns), lambda b, g: (row_blk0 + b, COL_SC // ns + g)),
        pl.BlockSpec((seq, SMALL_W), lambda b, g: (row_blk0 + b, AUX_SMALL + g)),
        pl.BlockSpec((8, hp), lambda b, g: (0, g)),
        pl.BlockSpec((8, ns), lambda b, g: (0, SSM_INNER // ns + g)),
        pl.BlockSpec((8, ns), lambda b, g: (0, (SSM_INNER + SSM_BC) // ns + g)),
        pl.BlockSpec((1, hp), lambda b, g: (0, g)),
        pl.BlockSpec((1, ns), lambda b, g: (0, SSM_INNER // ns + g)),
        pl.BlockSpec((1, ns), lambda b, g: (0, (SSM_INNER + SSM_BC) // ns + g)),
        pl.BlockSpec((1, SMALL_W), lambda b, g: (0, g)),
        pl.BlockSpec((1, SMALL_W), lambda b, g: (0, g)),
        pl.BlockSpec((1, hp), lambda b, g: (0, g)),
    ]
    args = [proj, proj, proj, proj, small, conv_w, conv_w, conv_w, conv_b, conv_b, conv_b,
            dt_bias_row, a_row, d_skip]
    if has_h0:
        in_specs.append(pl.BlockSpec((None, 2, SSM_HPG, SSM_HEADDIM, ns), lambda b, g: (b, 0, g, 0, 0)))
        args.append(h0)
    out_specs = [pl.BlockSpec((seq, hp), lambda b, g: (b, g))]
    out_shape = [jax.ShapeDtypeStruct((n_seq * seq, SSM_INNER), f32)]
    if emit_state:
        out_specs.append(pl.BlockSpec((1, 2, SSM_HPG, SSM_HEADDIM, ns), lambda b, g: (b, 0, g, 0, 0)))
        out_shape.append(jax.ShapeDtypeStruct((n_seq, 2, SSM_HEADS, SSM_HEADDIM, ns), f32))
    res = pl.pallas_call(
        functools.partial(_ssd_kernel, seq=seq, has_h0=has_h0, emit_state=emit_state),
        grid=(n_seq, SSM_GROUPS),
        in_specs=in_specs,
        out_specs=out_specs,
        out_shape=out_shape,
        scratch_shapes=[
            pltpu.VMEM((seq, hp), f32),
            pltpu.VMEM((seq, hp), f32),
            pltpu.VMEM((seq, ns), f32),
            pltpu.VMEM((seq, ns), f32),
            pltpu.VMEM((seq, SMALL_W), f32),
        ],
        compiler_params=_cparams(("parallel", "arbitrary")),
        name="ssd_lat" if has_h0 else "ssd_ctx",
    )(*args)
    return res if emit_state else (res[0], None)


def _attn_ctx_kernel(q_ref, k_ref, v_ref, sink_ref, o_ref):
    for head in range(ATTN_HEADS):
        kv = slice(head // Q_PER_KV * HEAD_DIM, (head // Q_PER_KV + 1) * HEAD_DIM)
        cols = slice(head * HEAD_DIM, (head + 1) * HEAD_DIM)
        s = lax.dot_general(q_ref[:, cols], k_ref[:, kv], NT, preferred_element_type=f32) * (HEAD_DIM ** -0.5)
        sink = sink_ref[head:head + 1, 0:1]
        m = jnp.maximum(jnp.max(s, axis=-1, keepdims=True), sink)
        p = jnp.exp(s - m)
        den = jnp.sum(p, axis=-1, keepdims=True) + jnp.exp(sink - m)
        o = jnp.dot(p.astype(bf16), v_ref[:, kv], preferred_element_type=f32) / den
        o_ref[:, cols] = o.astype(o_ref.dtype)


def _attn_ctx(proj, sink8, n_seq=BATCH):
    return pl.pallas_call(
        _attn_ctx_kernel,
        grid=(n_seq,),
        in_specs=[
            pl.BlockSpec((SEQ, ATTN_Q), lambda b: (b, COL_AQ // ATTN_Q)),
            pl.BlockSpec((SEQ, ATTN_KV), lambda b: (b, COL_AK // ATTN_KV)),
            pl.BlockSpec((SEQ, ATTN_KV), lambda b: (b, COL_AV // ATTN_KV)),
            pl.BlockSpec((ATTN_HEADS, 128), lambda b: (0, 0)),
        ],
        out_specs=pl.BlockSpec((SEQ, ATTN_Q), lambda b: (b, 0)),
        out_shape=jax.ShapeDtypeStruct((n_seq * SEQ, ATTN_Q), bf16),
        compiler_params=_cparams(("arbitrary",)),
        name="attn_ctx",
    )(proj, proj, proj, sink8)


def _rope(x, cos, sin_signed):
    quarter = HEAD_DIM // 4
    lane = lax.broadcasted_iota(jnp.int32, x.shape, 1)
    first = (lane % (2 * quarter)) < quarter
    partner = jnp.where(first, pltpu.roll(x, HEAD_DIM - quarter, 1), pltpu.roll(x, quarter, 1))
    return x * cos + partner * sin_signed


def _attn_lat_kernel(q_ref, k_ref, v_ref, kc_ref, vc_ref, cos_ref, sin_ref, sink_ref, o_ref, kr_ref):
    kvh = pl.program_id(1)
    cos = cos_ref[...]
    sin = sin_ref[...]
    kr_ref[...] = _rope(k_ref[...].astype(f32), cos, sin).astype(bf16)
    kc = kc_ref[...].astype(bf16)
    vc = vc_ref[...].astype(bf16)
    blk = WINDOW
    n_blk = DEC_SEQ // blk
    scale = HEAD_DIM ** -0.5
    for i in range(n_blk):
        lo = max(i - 1, 0) * blk
        hi = min(i + 2, n_blk) * blk
        kw = kr_ref[lo:hi, :]
        vw = v_ref[lo:hi, :].astype(bf16)
        qpos = i * blk + lax.broadcasted_iota(jnp.int32, (blk, hi - lo), 0)
        kpos = lo + lax.broadcasted_iota(jnp.int32, (blk, hi - lo), 1)
        win = jnp.abs(qpos - kpos) <= WINDOW
        rows = slice(i * blk, (i + 1) * blk)
        for g in range(Q_PER_KV):
            cols = slice(g * HEAD_DIM, (g + 1) * HEAD_DIM)
            q = _rope(q_ref[rows, cols].astype(f32), cos[rows, :], sin[rows, :]).astype(bf16)
            s_c = lax.dot_general(q, kc, NT, preferred_element_type=f32) * scale
            s_w = lax.dot_general(q, kw, NT, preferred_element_type=f32) * scale
            s_w = jnp.where(win, s_w, -jnp.inf)
            sink = sink_ref[pl.ds(kvh * Q_PER_KV + g, 1), 0:1]
            m = jnp.maximum(jnp.maximum(jnp.max(s_c, axis=-1, keepdims=True),
                                        jnp.max(s_w, axis=-1, keepdims=True)), sink)
            p_c = jnp.exp(s_c - m)
            p_w = jnp.exp(s_w - m)
            den = (jnp.sum(p_c, axis=-1, keepdims=True) + jnp.sum(p_w, axis=-1, keepdims=True)
                   + jnp.exp(sink - m))
            o = (jnp.dot(p_c.astype(bf16), vc, preferred_element_type=f32)
                 + jnp.dot(p_w.astype(bf16), vw, preferred_element_type=f32)) / den
            o_ref[rows, cols] = o.astype(o_ref.dtype)


def _attn_lat(proj, cache_k, cache_v, l, cos, sin, sink8, rb0=0, n_seq=DEC_BATCH):
    qw = Q_PER_KV * HEAD_DIM
    ck = cache_k.reshape(n_seq, DEPTH, PAST_LEN, ATTN_KV)
    cv = cache_v.reshape(n_seq, DEPTH, PAST_LEN, ATTN_KV)
    return pl.pallas_call(
        _attn_lat_kernel,
        grid=(n_seq, KV_HEADS),
        in_specs=[
            pl.BlockSpec((DEC_SEQ, qw), lambda b, h: (rb0 + b, COL_AQ // qw + h)),
            pl.BlockSpec((DEC_SEQ, HEAD_DIM), lambda b, h: (rb0 + b, COL_AK // HEAD_DIM + h)),
            pl.BlockSpec((DEC_SEQ, HEAD_DIM), lambda b, h: (rb0 + b, COL_AV // HEAD_DIM + h)),
            pl.BlockSpec((None, None, PAST_LEN, HEAD_DIM), lambda b, h: (b, l, 0, h)),
            pl.BlockSpec((None, None, PAST_LEN, HEAD_DIM), lambda b, h: (b, l, 0, h)),
            pl.BlockSpec((DEC_SEQ, HEAD_DIM), lambda b, h: (0, 0)),
            pl.BlockSpec((DEC_SEQ, HEAD_DIM), lambda b, h: (0, 0)),
            pl.BlockSpec((ATTN_HEADS, 128), lambda b, h: (0, 0)),
        ],
        out_specs=pl.BlockSpec((DEC_SEQ, qw), lambda b, h: (b, h)),
        out_shape=jax.ShapeDtypeStruct((n_seq * DEC_SEQ, ATTN_Q), bf16),
        scratch_shapes=[pltpu.VMEM((DEC_SEQ, HEAD_DIM), bf16)],
        compiler_params=_cparams(("parallel", "arbitrary")),
        name="attn_lat",
    )(proj, proj, proj, ck, cv, cos, sin, sink8)


def _rope_tables():
    quarter = HEAD_DIM // 4
    freqs = ROPE_THETA ** (-np.arange(quarter, dtype=np.float32) / quarter)
    t = np.arange(DEC_SEQ)
    cos = np.zeros((DEC_SEQ, HEAD_DIM), np.float32)
    sin = np.zeros((DEC_SEQ, HEAD_DIM), np.float32)
    for half, pos in enumerate((t // GRID_W, t % GRID_W)):
        ang = pos.astype(np.float32)[:, None] * freqs[None, :]
        base = half * 2 * quarter
        cos[:, base:base + quarter] = np.cos(ang)
        cos[:, base + quarter:base + 2 * quarter] = np.cos(ang)
        sin[:, base:base + quarter] = -np.sin(ang)
        sin[:, base + quarter:base + 2 * quarter] = np.sin(ang)
    return jnp.asarray(cos), jnp.asarray(sin)


def _merge_kernel(x_ref, gt_ref, og_ref, os_ref, oa_ref, b0_ref, b1_ref, b2_ref, sn_ref,
                  wg_ref, ws_ref, wa_ref, wo_ref, o_ref, osn_ref):
    j = pl.program_id(1)

    @pl.when(j == 0)
    def _():
        y = os_ref[...]
        ms = jnp.mean(y * y, axis=-1, keepdims=True)
        osn_ref[...] = (y * lax.rsqrt(ms + RMS_EPS) * sn_ref[...]).astype(bf16)
        o_ref[...] = jnp.zeros_like(o_ref)

    m = (b0_ref[...].astype(f32) * jnp.dot(og_ref[...], wg_ref[j], preferred_element_type=f32)
         + b1_ref[...].astype(f32) * jnp.dot(osn_ref[...], ws_ref[j], preferred_element_type=f32)
         + b2_ref[...].astype(f32) * jnp.dot(oa_ref[...], wa_ref[j], preferred_element_type=f32))
    o_ref[...] += jnp.dot(m.astype(bf16), wo_ref[j], preferred_element_type=f32)

    @pl.when(j == pl.num_programs(1) - 1)
    def _():
        o_ref[...] = x_ref[...] + gt_ref[0] * o_ref[...]


def _merge(st, x, mod3, l, proj, o_gla, o_ssm, o_att, ssm_norm, w_g, w_s, w_a, w_o):
    nj = D_MODEL // TJ

    def br_spec(k):
        return pl.BlockSpec((TM, TJ), lambda i, j: (i, (COL_BR + k * D_MODEL) // TJ + j))

    def wbr_spec():
        return pl.BlockSpec((None, nj, GLA_VAL, TJ), lambda i, j: (l, 0, 0, 0), pipeline_mode=pl.Buffered(1))

    return pl.pallas_call(
        _merge_kernel,
        grid=(st.n_tok // TM, nj),
        in_specs=[
            pl.BlockSpec((TM, D_MODEL), lambda i, j: (i, 0)),
            pl.BlockSpec((1, 1, D_MODEL), lambda i, j: (st.mod_row(i, TM), 0, 5)),
            pl.BlockSpec((TM, GLA_VAL), lambda i, j: (i, 0)),
            pl.BlockSpec((TM, SSM_INNER), lambda i, j: (i, 0)),
            pl.BlockSpec((TM, ATTN_Q), lambda i, j: (i, 0)),
            br_spec(0), br_spec(1), br_spec(2),
            pl.BlockSpec((None, 1, SSM_INNER), lambda i, j: (l, 0, 0)),
            wbr_spec(), wbr_spec(), wbr_spec(),
            pl.BlockSpec((None, nj, TJ, D_MODEL), lambda i, j: (l, 0, 0, 0), pipeline_mode=pl.Buffered(1)),
        ],
        out_specs=pl.BlockSpec((TM, D_MODEL), lambda i, j: (i, 0)),
        out_shape=jax.ShapeDtypeStruct((st.n_tok, D_MODEL), f32),
        scratch_shapes=[pltpu.VMEM((TM, SSM_INNER), bf16)],
        compiler_params=_cparams(("arbitrary", "arbitrary")),
        name="merge",
    )(x, mod3, o_gla, o_ssm, o_att, proj, proj, proj, ssm_norm, w_g, w_s, w_a, w_o)


_IN_SRC = np.cumsum((0, GLA_QK, GLA_QK, GLA_VAL, GLA_VAL, 2 * GLA_RANK, SSM_INNER, CONV_CH, 2 * SSM_HEADS,
                     ATTN_Q, ATTN_KV, ATTN_KV, 3 * D_MODEL))
SRC_GDOWN, SRC_SZ, SRC_DT, SRC_AQ, SRC_BR, SRC_END = (int(_IN_SRC[k]) for k in (4, 5, 7, 8, 11, 12))
W_PREP_K = 256
W_PREP_CHUNK = 512


def _w_in_prep_kernel(wt_ref, o_ref):
    def move(src, dst, width):
        for c0 in range(0, width, W_PREP_CHUNK):
            w = min(W_PREP_CHUNK, width - c0)
            o_ref[0, :, dst + c0:dst + c0 + w] = wt_ref[0, src + c0:src + c0 + w, :].T.astype(bf16)

    move(0, COL_GQ, SRC_GDOWN)
    move(SRC_SZ, COL_SZ, SRC_DT - SRC_SZ)
    move(SRC_AQ, COL_AQ, SRC_BR - SRC_AQ)
    move(SRC_BR, COL_BR, SRC_END - SRC_BR)

    n_gd, n_dt = 2 * GLA_RANK, 2 * SSM_HEADS
    t_gd = wt_ref[0, SRC_GDOWN:SRC_GDOWN + n_gd, :].astype(bf16)
    t_dt = wt_ref[0, SRC_DT:SRC_DT + n_dt, :].astype(bf16)
    r = lax.broadcasted_iota(jnp.int32, (SMALL_W, n_gd), 0)
    c = lax.broadcasted_iota(jnp.int32, (SMALL_W, n_gd), 1)
    sel_gd = jnp.where(r == c, 1.0, 0.0).astype(bf16)
    k = r - SMALL_DT
    for g in range(SSM_GROUPS):
        src_row = (k // SSM_HPG) * SSM_HEADS + g * SSM_HPG + k % SSM_HPG
        sel_dt = jnp.where((k >= 0) & (k < 2 * SSM_HPG) & (c == src_row), 1.0, 0.0).astype(bf16)
        blk = (jnp.dot(sel_gd, t_gd, preferred_element_type=f32)
               + jnp.dot(sel_dt, t_dt, preferred_element_type=f32))
        o_ref[0, :, COL_SMALL + g * SMALL_W:COL_SMALL + (g + 1) * SMALL_W] = blk.T.astype(bf16)


def _w_in_prep(w_in):
    wt = jnp.swapaxes(w_in, 1, 2)
    return pl.pallas_call(
        _w_in_prep_kernel,
        grid=(DEPTH, D_MODEL // W_PREP_K),
        in_specs=[pl.BlockSpec((1, SRC_END, W_PREP_K), lambda l, i: (l, 0, i))],
        out_specs=pl.BlockSpec((1, W_PREP_K, D_INP), lambda l, i: (l, i, 0)),
        out_shape=jax.ShapeDtypeStruct((DEPTH, D_MODEL, D_INP), bf16),
        compiler_params=_cparams(("arbitrary", "arbitrary")),
        name="w_in_prep",
    )(wt)


def kernel(x_prompt, x_sample, c, cache_k, cache_v, state_gla, state_ssm, c_ctx, w_mod, b_mod, ffn1_norm,
           ffn1_w_gate, ffn1_w_up, ffn1_w_down, mix_norm, w_in, gla_w_up, gla_b_up, gla_norm, ssm_conv_w,
           ssm_conv_b, ssm_dt_bias, ssm_a_log, ssm_d, ssm_norm, attn_sink, w_br_gla, w_br_ssm, w_br_attn,
           w_out, ffn2_norm, ffn2_w_gate, ffn2_w_up, ffn2_w_down, final_norm):
    xs = {CTX: x_prompt.reshape(N_PROMPT, D_MODEL), LAT: x_sample.reshape(N_SAMPLE, D_MODEL)}
    cvec = jnp.concatenate([c_ctx[None], c, jnp.zeros((MOD_ROWS - 1 - DEC_BATCH, D_MODEL), f32)], axis=0)
    mod = _modulation(cvec, w_mod, b_mod)

    w_in_p = _w_in_prep(w_in)
    nj = D_MODEL // TJ
    wbg, wbs, wba = (w.astype(bf16).reshape(DEPTH, GLA_VAL, nj, TJ).transpose(0, 2, 1, 3)
                     for w in (w_br_gla, w_br_ssm, w_br_attn))
    wo = w_out.astype(bf16).reshape(DEPTH, nj, TJ, D_MODEL)

    wz = jnp.zeros((DEPTH, 2, SMALL_W, GLA_QK), f32)
    for d in range(2):
        wz = wz.at[:, d, d * GLA_RANK:(d + 1) * GLA_RANK, :].set(gla_w_up[:, d])
    wz = wz.astype(bf16)
    bz = gla_b_up.reshape(DEPTH, 2, 1, GLA_QK)

    def dt_lanes(v):
        vg = v.reshape(DEPTH, 2, SSM_GROUPS, SSM_HPG).transpose(0, 2, 1, 3).reshape(DEPTH, SSM_GROUPS, 2 * SSM_HPG)
        row = jnp.zeros((DEPTH, SSM_GROUPS, SMALL_W), f32)
        row = row.at[:, :, SMALL_DT:SMALL_DT + 2 * SSM_HPG].set(vg)
        return row.reshape(DEPTH, 1, SSM_GROUPS * SMALL_W)

    dtb_rows = dt_lanes(ssm_dt_bias)
    a_rows = dt_lanes(-jnp.exp(ssm_a_log))
    d_skip = jnp.repeat(ssm_d, SSM_HEADDIM, axis=-1).reshape(DEPTH, 1, SSM_INNER)
    conv_w = jnp.concatenate([ssm_conv_w, jnp.zeros((DEPTH, 8 - SSM_CONV, CONV_CH), f32)], axis=1)
    conv_b = ssm_conv_b.reshape(DEPTH, 1, CONV_CH)
    sink8 = jnp.broadcast_to(attn_sink[:, :, None], (DEPTH, ATTN_HEADS, 128))
    cos, sin = _rope_tables()

    n1, nm, n2 = (w.reshape(DEPTH, 1, D_MODEL) for w in (ffn1_norm, mix_norm, ffn2_norm))
    sn = ssm_norm.reshape(DEPTH, 1, SSM_INNER)

    new_k, new_v, new_gla, new_ssm = [], [], [], []
    for l in range(DEPTH):
        mod3 = mod[l].reshape(MOD_ROWS, 1, N_MOD * D_MODEL)
        last = l == DEPTH - 1
        for st in (CTX, LAT):
            x = _ffn(st, xs[st], mod3, l, 0, n1, ffn1_w_gate, ffn1_w_up, ffn1_w_down)
            proj, aux = _inproj(st, x, mod3, l, nm, w_in_p)
            if st is CTX:
                o_gla, st_g = _gla(proj, aux, 0, BATCH, SEQ, wz[l], bz[l], gla_norm[l][None], None)
                o_ssm, st_s = _ssd(proj, aux, 0, BATCH, SEQ, conv_w[l], conv_b[l], dtb_rows[l], a_rows[l],
                                   d_skip[l], None)
                o_att = _attn_ctx(proj, sink8[l])
                new_k.append(aux[:, :ATTN_KV].reshape(BATCH, SEQ, KV_HEADS, HEAD_DIM))
                new_v.append(aux[:, ATTN_KV:2 * ATTN_KV].reshape(BATCH, SEQ, KV_HEADS, HEAD_DIM))
                new_gla.append(st_g)
                new_ssm.append(st_s)
            else:
                o_gla, _ = _gla(proj, aux, 0, DEC_BATCH, DEC_SEQ, wz[l], bz[l], gla_norm[l][None], state_gla[:, l])
                o_ssm, _ = _ssd(proj, aux, 0, DEC_BATCH, DEC_SEQ, conv_w[l], conv_b[l], dtb_rows[l], a_rows[l],
                                d_skip[l], state_ssm[:, l])
                o_att = _attn_lat(proj, cache_k, cache_v, l, cos, sin, sink8[l])
            x = _merge(st, x, mod3, l, proj, o_gla, o_ssm, o_att, sn, wbg, wbs, wba, wo)
            xs[st] = _ffn(st, x, mod3, l, 2, n2, ffn2_w_gate, ffn2_w_up, ffn2_w_down,
                          final_w=final_norm if last else None)

    y_prompt = xs[CTX].reshape(BATCH, SEQ, D_MODEL)
    y_sample = xs[LAT].reshape(DEC_BATCH, DEC_SEQ, D_MODEL)
    return (y_prompt, y_sample, jnp.stack(new_k, axis=1), jnp.stack(new_v, axis=1),
            jnp.stack(new_gla, axis=1), jnp.stack(new_ssm, axis=1))
```

```python
import functools
import math
from typing import NamedTuple

import numpy as np
import jax
import jax.numpy as jnp
from jax import lax
from jax.experimental import pallas as pl
from jax.experimental.pallas import tpu as pltpu

f32 = jnp.float32
bf16 = jnp.bfloat16

D_MODEL = 2048
BATCH = 32
SEQ = 256
DEPTH = 2
DEC_BATCH = 2
DEC_SEQ = 1024
PAST_LEN = 512
GRID_W = 64
RMS_EPS = 1e-6
N_MOD = 9
D_FF = 5632
GLA_HEADS = 4
GLA_DK = 128
GLA_DV = 256
GLA_RANK = 16
GLA_GATE_NORM = 16.0
GLA_CHUNK = 16
SSM_HEADS = 16
SSM_HEADDIM = 64
SSM_GROUPS = 4
SSM_HPG = SSM_HEADS // SSM_GROUPS
SSM_STATE = 128
SSM_CONV = 5
SSM_INNER = SSM_HEADS * SSM_HEADDIM
SSM_BC = SSM_GROUPS * SSM_STATE
CONV_CH = SSM_INNER + 2 * SSM_BC
ATTN_HEADS = 8
KV_HEADS = 2
Q_PER_KV = ATTN_HEADS // KV_HEADS
HEAD_DIM = 128
WINDOW = 128
ROPE_THETA = 10000.0
GLA_QK = GLA_HEADS * GLA_DK
GLA_VAL = GLA_HEADS * GLA_DV
ATTN_Q = ATTN_HEADS * HEAD_DIM
ATTN_KV = KV_HEADS * HEAD_DIM

N_PROMPT = BATCH * SEQ
N_SAMPLE = DEC_BATCH * DEC_SEQ
MOD_ROWS = 8

COL_GQ = 0
COL_GK = COL_GQ + GLA_QK
COL_GV = COL_GK + GLA_QK
COL_GR = COL_GV + GLA_VAL
COL_SZ = COL_GR + GLA_VAL
COL_SX = COL_SZ + SSM_INNER
COL_SB = COL_SX + SSM_INNER
COL_SC = COL_SB + SSM_BC
COL_AQ = COL_SC + SSM_BC
COL_AK = COL_AQ + ATTN_Q
COL_AV = COL_AK + ATTN_KV
COL_SMALL = COL_AV + ATTN_KV
SMALL_W = 128
SMALL_DT = 2 * GLA_RANK
COL_BR = 8192
D_INP = COL_BR + 3 * D_MODEL

TM = 256
TM_BIG = 1024
TF_BIG = 256
TN_IN = 1024
AUX_TILE = COL_AK // TN_IN
AUX_SMALL = (COL_SMALL - COL_AK) // SMALL_W
TN_MOD = 1024
TJ = 512
GLA_SB = 256
GLA_TILE = 128
GLA_HPS = 2
SSD_CHUNK = 128
VMEM_LIMIT = 58 * 1024 * 1024

NT = (((1,), (1,)), ((), ()))


def _silu(x):
    return x * jax.nn.sigmoid(x)


class _Stream(NamedTuple):
    n_tok: int
    row0: int
    per_row: int

    def mod_row(self, i, tm):
        return self.row0 + (i * tm) // self.per_row


CTX = _Stream(N_PROMPT, 0, N_PROMPT)
LAT = _Stream(N_SAMPLE, 1, DEC_SEQ)


def _cparams(sem):
    return pltpu.CompilerParams(dimension_semantics=sem, vmem_limit_bytes=VMEM_LIMIT)


def _mod_kernel(c_ref, w_ref, b_ref, o_ref):
    s = _silu(c_ref[...]).astype(bf16)
    o_ref[0] = jnp.dot(s, w_ref[0].astype(bf16), preferred_element_type=f32) + b_ref[0]


def _modulation(cvec, w_mod, b_mod):
    n = N_MOD * D_MODEL
    return pl.pallas_call(
        _mod_kernel,
        grid=(DEPTH, n // TN_MOD),
        in_specs=[
            pl.BlockSpec((MOD_ROWS, D_MODEL), lambda l, j: (0, 0)),
            pl.BlockSpec((1, D_MODEL, TN_MOD), lambda l, j: (l, 0, j)),
            pl.BlockSpec((1, 1, TN_MOD), lambda l, j: (l, 0, j)),
        ],
        out_specs=pl.BlockSpec((1, MOD_ROWS, TN_MOD), lambda l, j: (l, 0, j)),
        out_shape=jax.ShapeDtypeStruct((DEPTH, MOD_ROWS, n), f32),
        compiler_params=_cparams(("arbitrary", "arbitrary")),
        name="modulation",
    )(cvec, w_mod, b_mod.reshape(DEPTH, 1, n))


def _norm_mod(x, nw, sh, sc):
    ms = jnp.mean(x * x, axis=-1, keepdims=True)
    h = x * lax.rsqrt(ms + RMS_EPS) * nw
    return h * (1.0 + sc) + sh


def _ffn_kernel(x_ref, sh_ref, sc_ref, gt_ref, nw_ref, wg_ref, wu_ref, wd_ref, *rest, final):
    if final:
        fw_ref, o_ref, h_ref = rest
    else:
        o_ref, h_ref = rest
    j = pl.program_id(1)

    @pl.when(j == 0)
    def _():
        h_ref[...] = _norm_mod(x_ref[...], nw_ref[...], sh_ref[0], sc_ref[0]).astype(bf16)
        o_ref[...] = jnp.zeros_like(o_ref)

    h = h_ref[...]
    g = jnp.dot(h, wg_ref[...].astype(bf16), preferred_element_type=f32)
    u = jnp.dot(h, wu_ref[...].astype(bf16), preferred_element_type=f32)
    a = (_silu(g) * u).astype(bf16)
    o_ref[...] += jnp.dot(a, wd_ref[...].astype(bf16), preferred_element_type=f32)

    @pl.when(j == pl.num_programs(1) - 1)
    def _():
        y = x_ref[...] + 0.5 * gt_ref[0] * o_ref[...]
        if final:
            ms = jnp.mean(y * y, axis=-1, keepdims=True)
            y = y * lax.rsqrt(ms + RMS_EPS) * fw_ref[...]
        o_ref[...] = y


def _ffn(st, x, mod3, l, slot, nw, wg, wu, wd, final_w=None):
    final = final_w is not None
    tm, tf = TM_BIG, TF_BIG

    def mod_spec(k):
        return pl.BlockSpec((1, 1, D_MODEL), lambda i, j: (st.mod_row(i, tm), 0, 3 * slot + k))

    in_specs = [
        pl.BlockSpec((tm, D_MODEL), lambda i, j: (i, 0)),
        mod_spec(0), mod_spec(1), mod_spec(2),
        pl.BlockSpec((None, 1, D_MODEL), lambda i, j: (l, 0, 0)),
        pl.BlockSpec((None, D_MODEL, tf), lambda i, j: (l, 0, j)),
        pl.BlockSpec((None, D_MODEL, tf), lambda i, j: (l, 0, j)),
        pl.BlockSpec((None, tf, D_MODEL), lambda i, j: (l, j, 0)),
    ]
    args = [x, mod3, mod3, mod3, nw, wg, wu, wd]
    if final:
        in_specs.append(pl.BlockSpec((1, D_MODEL), lambda i, j: (0, 0)))
        args.append(final_w.reshape(1, D_MODEL))
    return pl.pallas_call(
        functools.partial(_ffn_kernel, final=final),
        grid=(st.n_tok // tm, D_FF // tf),
        in_specs=in_specs,
        out_specs=pl.BlockSpec((tm, D_MODEL), lambda i, j: (i, 0)),
        out_shape=jax.ShapeDtypeStruct((st.n_tok, D_MODEL), f32),
        scratch_shapes=[pltpu.VMEM((tm, D_MODEL), bf16)],
        compiler_params=_cparams(("arbitrary", "arbitrary")),
        name="ffn_final" if final else "ffn",
    )(*args)


def _inproj_kernel(x_ref, sh_ref, sc_ref, nw_ref, w_ref, o_ref, os_ref, h_ref):
    j = pl.program_id(1)

    @pl.when(j == 0)
    def _():
        h_ref[...] = _norm_mod(x_ref[...], nw_ref[...], sh_ref[0], sc_ref[0]).astype(bf16)

    def tile():
        return jnp.dot(h_ref[...], w_ref[...], preferred_element_type=f32)

    @pl.when((j < COL_BR // TN_IN) & (j != AUX_TILE))
    def _():
        o_ref[...] = tile().astype(bf16)

    @pl.when(j == AUX_TILE)
    def _():
        t = tile()
        os_ref[...] = t
        o_ref[...] = t.astype(bf16)

    @pl.when(j >= COL_BR // TN_IN)
    def _():
        o_ref[...] = jax.nn.sigmoid(tile()).astype(bf16)


def _inproj(st, x, mod3, l, nw, w):
    tm = TM_BIG

    def mod_spec(k):
        return pl.BlockSpec((1, 1, D_MODEL), lambda i, j: (st.mod_row(i, tm), 0, 3 + k))

    return pl.pallas_call(
        _inproj_kernel,
        grid=(st.n_tok // tm, D_INP // TN_IN),
        in_specs=[
            pl.BlockSpec((tm, D_MODEL), lambda i, j: (i, 0)),
            mod_spec(0), mod_spec(1),
            pl.BlockSpec((None, 1, D_MODEL), lambda i, j: (l, 0, 0)),
            pl.BlockSpec((None, D_MODEL, TN_IN), lambda i, j: (l, 0, j)),
        ],
        out_specs=[pl.BlockSpec((tm, TN_IN), lambda i, j: (i, j)),
                   pl.BlockSpec((tm, TN_IN), lambda i, j: (i, 0))],
        out_shape=[jax.ShapeDtypeStruct((st.n_tok, D_INP), bf16), jax.ShapeDtypeStruct((st.n_tok, TN_IN), f32)],
        scratch_shapes=[pltpu.VMEM((tm, D_MODEL), bf16)],
        compiler_params=_cparams(("arbitrary", "arbitrary")),
        name="inproj",
    )(x, mod3, mod3, nw, w)


def _split_dot(m, x):
    n = x.shape[1]
    hi = x.astype(bf16)
    lo = (x - hi.astype(f32)).astype(bf16)
    r = jnp.dot(m, jnp.concatenate([hi, lo], axis=1), preferred_element_type=f32)
    return r[:, :n] + r[:, n:]


def _log_sigmoid(x):
    return jnp.minimum(x, 0.0) - jnp.log(1.0 + jnp.exp(-jnp.abs(x)))


def _gla_kernel(q_ref, k_ref, v_ref, r_ref, sm_ref, wz_ref, bz_ref, gn_ref, *rest, seq, has_h0, emit_state):
    rest = list(rest)
    h0_ref = rest.pop(0) if has_h0 else None
    o_ref = rest.pop(0)
    st_ref = rest.pop(0) if emit_state else None
    oin_ref, qe_ref, ke_ref, dec_ref, vt_ref = rest
    hps = GLA_HPS
    sb = GLA_SB
    tile = GLA_TILE
    per_tile = tile // GLA_CHUNK
    n_tile = seq // tile

    row = lax.broadcasted_iota(jnp.int32, (sb, sb), 0)
    col = lax.broadcasted_iota(jnp.int32, (sb, sb), 1)
    same = (row // GLA_CHUNK) == (col // GLA_CHUNK)
    masks = (same & (col <= row), same & (col >= row))
    blk = jnp.where(same, 1.0, 0.0).astype(bf16)
    tri_blk = tuple(jnp.concatenate([jnp.where(m, 1.0, 0.0).astype(bf16), blk], axis=0) for m in masks)

    for s in range(seq // sb):
        rows = slice(s * sb, (s + 1) * sb)
        sm = sm_ref[rows, :].astype(bf16)
        for hh in range(hps):
            kcols = slice(hh * GLA_DK, (hh + 1) * GLA_DK)
            vcols = slice(hh * GLA_DV, (hh + 1) * GLA_DV)
            q = q_ref[rows, kcols].astype(f32) * (GLA_DK ** -0.5)
            k = k_ref[rows, kcols].astype(f32)
            v = v_ref[rows, vcols]
            v_t = v.astype(f32).T.astype(bf16)
            for t in range(sb // tile):
                vt_ref[hh, s * (sb // tile) + t] = v_t[:, t * tile:(t + 1) * tile]
            for d in range(2):
                gz = jnp.dot(sm, wz_ref[d, :, kcols], preferred_element_type=f32) + bz_ref[d, :, kcols]
                la = _log_sigmoid(gz) * (1.0 / GLA_GATE_NORM)
                sums = _split_dot(tri_blk[d], la)
                cum = sums[:sb]
                tot = sums[sb:]
                qe = q * jnp.exp(cum)
                kinv = k * jnp.exp(-cum)
                kend = k * jnp.exp(tot - cum)
                a = lax.dot_general(qe.astype(bf16), kinv.astype(bf16), NT, preferred_element_type=f32)
                a = jnp.where(masks[d], a, 0.0).astype(bf16)
                oin_ref[hh, d, rows, :] = jnp.dot(a, v, preferred_element_type=f32)
                qe_ref[hh, d, rows, :] = qe.astype(bf16)
                ke_ref[hh, d, rows, :] = kend.astype(bf16)
                dec_ref[hh, d, rows, :] = jnp.exp(tot)

    chains = [(hh, d) for hh in range(hps) for d in range(2)]
    if has_h0:
        s_init = tuple(h0_ref[d, hh].T for hh, d in chains)
    else:
        s_init = (jnp.zeros((GLA_DV, GLA_DK), f32),) * len(chains)

    def scan_tile(o, carry):
        sts = list(carry)
        for u in range(per_tile):
            for ci, (hh, d) in enumerate(chains):
                t_idx = o if d == 0 else n_tile - 1 - o
                pos = u if d == 0 else per_tile - 1 - u
                i0 = t_idx * tile + pos * GLA_CHUNK
                if not isinstance(i0, int):
                    i0 = pl.multiple_of(i0, GLA_CHUNK)
                rows = pl.ds(i0, GLA_CHUNK)
                inter = lax.dot_general(qe_ref[hh, d, rows, :], sts[ci].astype(bf16), NT,
                                        preferred_element_type=f32)
                oin_ref[hh, d, rows, :] += inter
                pieces = [ke_ref[hh, d, rows, :]]
                if pos:
                    pieces.insert(0, jnp.zeros((pos * GLA_CHUNK, GLA_DK), bf16))
                if pos < per_tile - 1:
                    pieces.append(jnp.zeros(((per_tile - 1 - pos) * GLA_CHUNK, GLA_DK), bf16))
                upd = jnp.dot(vt_ref[hh, t_idx], jnp.concatenate(pieces, axis=0), preferred_element_type=f32)
                sts[ci] = sts[ci] * dec_ref[hh, d, pl.ds(i0, 1), :] + upd
        return tuple(sts)

    if n_tile <= 2:
        s_fin = s_init
        for o in range(n_tile):
            s_fin = scan_tile(o, s_fin)
    else:
        s_fin = lax.fori_loop(0, n_tile, scan_tile, s_init)

    for hh in range(hps):
        vcols = slice(hh * GLA_DV, (hh + 1) * GLA_DV)
        o = oin_ref[hh, 0] + oin_ref[hh, 1]
        ms = jnp.mean(o * o, axis=-1, keepdims=True)
        o = o * lax.rsqrt(ms + RMS_EPS) * gn_ref[...]
        o_ref[:, vcols] = (o * _silu(r_ref[:, vcols].astype(f32))).astype(o_ref.dtype)
    if emit_state:
        for ci, (hh, d) in enumerate(chains):
            st_ref[0, d, hh] = s_fin[ci].T


def _gla(proj, small, row_blk0, n_seq, seq, wz, bz, gn, h0):
    has_h0 = h0 is not None
    emit_state = not has_h0
    kq, kv_, hps = GLA_DK, GLA_DV, GLA_HPS
    kw, vw = hps * kq, hps * kv_
    in_specs = [
        pl.BlockSpec((seq, kw), lambda b, h: (row_blk0 + b, COL_GQ // kw + h)),
        pl.BlockSpec((seq, kw), lambda b, h: (row_blk0 + b, COL_GK // kw + h)),
        pl.BlockSpec((seq, vw), lambda b, h: (row_blk0 + b, COL_GV // vw + h)),
        pl.BlockSpec((seq, vw), lambda b, h: (row_blk0 + b, COL_GR // vw + h)),
        pl.BlockSpec((seq, SMALL_W), lambda b, h: (row_blk0 + b, AUX_SMALL)),
        pl.BlockSpec((2, SMALL_W, kw), lambda b, h: (0, 0, h)),
        pl.BlockSpec((2, 1, kw), lambda b, h: (0, 0, h)),
        pl.BlockSpec((1, kv_), lambda b, h: (0, 0)),
    ]
    args = [proj, proj, proj, proj, small, wz, bz, gn]
    if has_h0:
        in_specs.append(pl.BlockSpec((None, 2, hps, kq, kv_), lambda b, h: (b, 0, h, 0, 0)))
        args.append(h0)
    out_specs = [pl.BlockSpec((seq, vw), lambda b, h: (b, h))]
    out_shape = [jax.ShapeDtypeStruct((n_seq * seq, GLA_VAL), bf16)]
    if emit_state:
        out_specs.append(pl.BlockSpec((1, 2, hps, kq, kv_), lambda b, h: (b, 0, h, 0, 0)))
        out_shape.append(jax.ShapeDtypeStruct((n_seq, 2, GLA_HEADS, kq, kv_), f32))
    res = pl.pallas_call(
        functools.partial(_gla_kernel, seq=seq, has_h0=has_h0, emit_state=emit_state),
        grid=(n_seq, GLA_HEADS // hps),
        in_specs=in_specs,
        out_specs=out_specs,
        out_shape=out_shape,
        scratch_shapes=[
            pltpu.VMEM((hps, 2, seq, kv_), f32),
            pltpu.VMEM((hps, 2, seq, kq), bf16),
            pltpu.VMEM((hps, 2, seq, kq), bf16),
            pltpu.VMEM((hps, 2, seq, kq), f32),
            pltpu.VMEM((hps, seq // GLA_TILE, kv_, GLA_TILE), bf16),
        ],
        compiler_params=_cparams(("parallel", "arbitrary")),
        name="gla_lat" if has_h0 else "gla_ctx",
    )(*args)
    return res if emit_state else (res[0], None)


def _softplus(x):
    return jnp.maximum(x, 0.0) + jnp.log(1.0 + jnp.exp(-jnp.abs(x)))


def _split3(x):
    hi = x.astype(bf16)
    r1 = x - hi.astype(f32)
    mid = r1.astype(bf16)
    lo = (r1 - mid.astype(f32)).astype(bf16)
    return [hi, mid, lo]


def _split3_dot(m, x):
    n = x.shape[1]
    r = jnp.dot(m, jnp.concatenate(_split3(x), axis=1), preferred_element_type=f32)
    return r[:, :n] + r[:, n:2 * n] + r[:, 2 * n:]


def _dot_sel(x, sel):
    return sum(jnp.dot(t, sel, preferred_element_type=f32) for t in _split3(x))


def _shift_rows(x, d, t_idx):
    n = x.shape[0]
    if d == 0:
        return x
    y = pltpu.roll(x, (-d) % n, 0)
    ok = (t_idx + d >= 0) & (t_idx + d < n)
    return jnp.where(ok, y, 0.0)


def _ssd_kernel(z_ref, x_ref, b_ref, c_ref, sm_ref, cwx_ref, cwb_ref, cwc_ref, cbx_ref, cbb_ref, cbc_ref,
                dtb_ref, a_ref, dsk_ref, *rest, seq, has_h0, emit_state):
    rest = list(rest)
    h0_ref = rest.pop(0) if has_h0 else None
    o_ref = rest.pop(0)
    st_ref = rest.pop(0) if emit_state else None
    y_ref, xs_ref, bm_ref, cm_ref, dt_ref = rest
    cs = SSD_CHUNK
    n_chunk = seq // cs
    hp = SSM_HPG * SSM_HEADDIM

    t_idx = lax.broadcasted_iota(jnp.int32, (seq, 1), 0)

    def conv_silu(src_ref, w_ref, bias_ref):
        xin = src_ref[...].astype(f32)
        acc = jnp.zeros_like(xin) + bias_ref[...]
        for j in range(SSM_CONV):
            acc = acc + w_ref[j:j + 1, :] * _shift_rows(xin, j - SSM_CONV // 2, t_idx)
        return _silu(acc)

    xs_ref[...] = conv_silu(x_ref, cwx_ref, cbx_ref)
    bm_ref[...] = conv_silu(b_ref, cwb_ref, cbb_ref)
    cm_ref[...] = conv_silu(c_ref, cwc_ref, cbc_ref)

    row = lax.broadcasted_iota(jnp.int32, (cs, cs), 0)
    col = lax.broadcasted_iota(jnp.int32, (cs, cs), 1)
    masks = (col <= row, col >= row)
    tris = tuple(jnp.where(m, 1.0, 0.0).astype(bf16) for m in masks)
    lane = lax.broadcasted_iota(jnp.int32, (1, SMALL_W), 1)

    dt_lane = (lane >= SMALL_DT) & (lane < SMALL_DT + 2 * SSM_HPG)
    a_row = a_ref[...]
    dt_bias = dtb_ref[...]

    def selectors(d):
        r = lax.broadcasted_iota(jnp.int32, (SMALL_W, SSM_HPG * cs), 0)
        c = lax.broadcasted_iota(jnp.int32, (SMALL_W, SSM_HPG * cs), 1)
        bcast = jnp.where(r == SMALL_DT + d * SSM_HPG + c // cs, 1.0, 0.0).astype(bf16)
        r = lax.broadcasted_iota(jnp.int32, (SMALL_W, hp), 0)
        c = lax.broadcasted_iota(jnp.int32, (SMALL_W, hp), 1)
        expand = jnp.where(r == SMALL_DT + d * SSM_HPG + c // SSM_HEADDIM, 1.0, 0.0).astype(bf16)
        return jnp.concatenate([bcast, expand], axis=1)

    head_of_lane = lax.broadcasted_iota(jnp.int32, (cs, hp), 1) // SSM_HEADDIM
    bm_t = [bm_ref[c * cs:(c + 1) * cs, :].T.astype(bf16) for c in range(n_chunk)]

    dt_ref[...] = jnp.where(dt_lane, _softplus(sm_ref[...] + dt_bias), 0.0)
    y_ref[...] = jnp.zeros_like(y_ref)
    states = []
    for d in range(2):
        sel = selectors(d)
        if has_h0:
            st = h0_ref[d].reshape(hp, SSM_STATE).T
        else:
            st = jnp.zeros((SSM_STATE, hp), f32)
        order = range(n_chunk) if d == 0 else range(n_chunk - 1, -1, -1)
        for cidx in order:
            rows = slice(cidx * cs, (cidx + 1) * cs)
            dt = dt_ref[rows, :]
            cum = _split3_dot(tris[d], dt * a_row)
            cum_t = cum.T
            cum_s = _dot_sel(cum, sel)
            cum_b = cum_s[:, :SSM_HPG * cs]
            cum_e = cum_s[:, SSM_HPG * cs:]
            dt_e = _dot_sel(dt, sel[:, SSM_HPG * cs:])
            tot_e = cum_e[0:1, :] if d == 1 else cum_e[cs - 1:cs, :]
            xs = xs_ref[rows, :]
            cm = cm_ref[rows, :].astype(bf16)
            cb = lax.dot_general(cm, bm_ref[rows, :].astype(bf16), NT, preferred_element_type=f32)
            w_parts = []
            for j in range(SSM_HPG):
                ln = SMALL_DT + d * SSM_HPG + j
                seg = jnp.exp(jnp.where(masks[d], cum_b[:, j * cs:(j + 1) * cs] - cum_t[ln:ln + 1, :], -jnp.inf))
                w_parts.append((cb * seg).astype(bf16))
            w = jnp.concatenate(w_parts, axis=1)
            xd = xs * dt_e
            xd_bd = jnp.concatenate([jnp.where(head_of_lane == j, xd, 0.0).astype(bf16)
                                     for j in range(SSM_HPG)], axis=0)
            inter = jnp.dot(cm, st.astype(bf16), preferred_element_type=f32)
            y_ref[rows, :] += jnp.dot(w, xd_bd, preferred_element_type=f32) + jnp.exp(cum_e) * inter
            xw = (xd * jnp.exp(tot_e - cum_e)).astype(bf16)
            st = st * jnp.exp(tot_e) + jnp.dot(bm_t[cidx], xw, preferred_element_type=f32)
        states.append(st)

    y = y_ref[...] + dsk_ref[...] * xs_ref[...]
    o_ref[...] = y * _silu(z_ref[...].astype(f32))
    if emit_state:
        st_ref[0, 0] = states[0].T.reshape(SSM_HPG, SSM_HEADDIM, SSM_STATE)
        st_ref[0, 1] = states[1].T.reshape(SSM_HPG, SSM_HEADDIM, SSM_STATE)


def _ssd(proj, small, row_blk0, n_seq, seq, conv_w, conv_b, dt_bias_row, a_row, d_skip, h0):
    has_h0 = h0 is not None
    emit_state = not has_h0
    hp = SSM_HPG * SSM_HEADDIM
    ns = SSM_STATE
    in_specs = [
        pl.BlockSpec((seq, hp), lambda b, g: (row_blk0 + b, COL_SZ // hp + g)),
        pl.BlockSpec((seq, hp), lambda b, g: (row_blk0 + b, COL_SX // hp + g)),
        pl.BlockSpec((seq, ns), lambda b, g: (row_blk0 + b, COL_SB // ns + g)),
        pl.BlockSpec((seq, ns), lambda b, g: (row_blk0 + b, COL_SC // ns + g)),
        pl.BlockSpec((seq, SMALL_W), lambda b, g: (row_blk0 + b, AUX_SMALL + g)),
        pl.BlockSpec((8, hp), lambda b, g: (0, g)),
        pl.BlockSpec((8, ns), lambda b, g: (0, SSM_INNER // ns + g)),
        pl.BlockSpec((8, ns), lambda b, g: (0, (SSM_INNER + SSM_BC) // ns + g)),
        pl.BlockSpec((1, hp), lambda b, g: (0, g)),
        pl.BlockSpec((1, ns), lambda b, g: (0, SSM_INNER // ns + g)),
        pl.BlockSpec((1, ns), lambda b, g: (0, (SSM_INNER + SSM_BC) // ns + g)),
        pl.BlockSpec((1, SMALL_W), lambda b, g: (0, g)),
        pl.BlockSpec((1, SMALL_W), lambda b, g: (0, g)),
        pl.BlockSpec((1, hp), lambda b, g: (0, g)),
    ]
    args = [proj, proj, proj, proj, small, conv_w, conv_w, conv_w, conv_b, conv_b, conv_b,
            dt_bias_row, a_row, d_skip]
    if has_h0:
        in_specs.append(pl.BlockSpec((None, 2, SSM_HPG, SSM_HEADDIM, ns), lambda b, g: (b, 0, g, 0, 0)))
        args.append(h0)
    out_specs = [pl.BlockSpec((seq, hp), lambda b, g: (b, g))]
    out_shape = [jax.ShapeDtypeStruct((n_seq * seq, SSM_INNER), f32)]
    if emit_state:
        out_specs.append(pl.BlockSpec((1, 2, SSM_HPG, SSM_HEADDIM, ns), lambda b, g: (b, 0, g, 0, 0)))
        out_shape.append(jax.ShapeDtypeStruct((n_seq, 2, SSM_HEADS, SSM_HEADDIM, ns), f32))
    res = pl.pallas_call(
        functools.partial(_ssd_kernel, seq=seq, has_h0=has_h0, emit_state=emit_state),
        grid=(n_seq, SSM_GROUPS),
        in_specs=in_specs,
        out_specs=out_specs,
        out_shape=out_shape,
        scratch_shapes=[
            pltpu.VMEM((seq, hp), f32),
            pltpu.VMEM((seq, hp), f32),
            pltpu.VMEM((seq, ns), f32),
            pltpu.VMEM((seq, ns), f32),
            pltpu.VMEM((seq, SMALL_W), f32),
        ],
        compiler_params=_cparams(("parallel", "arbitrary")),
        name="ssd_lat" if has_h0 else "ssd_ctx",
    )(*args)
    return res if emit_state else (res[0], None)


def _attn_ctx_kernel(q_ref, k_ref, v_ref, sink_ref, o_ref):
    for head in range(ATTN_HEADS):
        kv = slice(head // Q_PER_KV * HEAD_DIM, (head // Q_PER_KV + 1) * HEAD_DIM)
        cols = slice(head * HEAD_DIM, (head + 1) * HEAD_DIM)
        s = lax.dot_general(q_ref[:, cols], k_ref[:, kv], NT, preferred_element_type=f32) * (HEAD_DIM ** -0.5)
        sink = sink_ref[head:head + 1, 0:1]
        m = jnp.maximum(jnp.max(s, axis=-1, keepdims=True), sink)
        p = jnp.exp(s - m)
        den = jnp.sum(p, axis=-1, keepdims=True) + jnp.exp(sink - m)
        o = jnp.dot(p.astype(bf16), v_ref[:, kv], preferred_element_type=f32) / den
        o_ref[:, cols] = o.astype(o_ref.dtype)


def _attn_ctx(proj, sink8, n_seq=BATCH):
    return pl.pallas_call(
        _attn_ctx_kernel,
        grid=(n_seq,),
        in_specs=[
            pl.BlockSpec((SEQ, ATTN_Q), lambda b: (b, COL_AQ // ATTN_Q)),
            pl.BlockSpec((SEQ, ATTN_KV), lambda b: (b, COL_AK // ATTN_KV)),
            pl.BlockSpec((SEQ, ATTN_KV), lambda b: (b, COL_AV // ATTN_KV)),
            pl.BlockSpec((ATTN_HEADS, 128), lambda b: (0, 0)),
        ],
        out_specs=pl.BlockSpec((SEQ, ATTN_Q), lambda b: (b, 0)),
        out_shape=jax.ShapeDtypeStruct((n_seq * SEQ, ATTN_Q), bf16),
        compiler_params=_cparams(("arbitrary",)),
        name="attn_ctx",
    )(proj, proj, proj, sink8)


def _rope(x, cos, sin_signed):
    quarter = HEAD_DIM // 4
    lane = lax.broadcasted_iota(jnp.int32, x.shape, 1)
    first = (lane % (2 * quarter)) < quarter
    partner = jnp.where(first, pltpu.roll(x, HEAD_DIM - quarter, 1), pltpu.roll(x, quarter, 1))
    return x * cos + partner * sin_signed


def _attn_lat_kernel(q_ref, k_ref, v_ref, kc_ref, vc_ref, cos_ref, sin_ref, sink_ref, o_ref, kr_ref):
    kvh = pl.program_id(1)
    cos = cos_ref[...]
    sin = sin_ref[...]
    kr_ref[...] = _rope(k_ref[...].astype(f32), cos, sin).astype(bf16)
    kc = kc_ref[...].astype(bf16)
    vc = vc_ref[...].astype(bf16)
    blk = WINDOW
    n_blk = DEC_SEQ // blk
    scale = HEAD_DIM ** -0.5
    for i in range(n_blk):
        lo = max(i - 1, 0) * blk
        hi = min(i + 2, n_blk) * blk
        kw = kr_ref[lo:hi, :]
        vw = v_ref[lo:hi, :].astype(bf16)
        qpos = i * blk + lax.broadcasted_iota(jnp.int32, (blk, hi - lo), 0)
        kpos = lo + lax.broadcasted_iota(jnp.int32, (blk, hi - lo), 1)
        win = jnp.abs(qpos - kpos) <= WINDOW
        rows = slice(i * blk, (i + 1) * blk)
        for g in range(Q_PER_KV):
            cols = slice(g * HEAD_DIM, (g + 1) * HEAD_DIM)
            q = _rope(q_ref[rows, cols].astype(f32), cos[rows, :], sin[rows, :]).astype(bf16)
            s_c = lax.dot_general(q, kc, NT, preferred_element_type=f32) * scale
            s_w = lax.dot_general(q, kw, NT, preferred_element_type=f32) * scale
            s_w = jnp.where(win, s_w, -jnp.inf)
            sink = sink_ref[pl.ds(kvh * Q_PER_KV + g, 1), 0:1]
            m = jnp.maximum(jnp.maximum(jnp.max(s_c, axis=-1, keepdims=True),
                                        jnp.max(s_w, axis=-1, keepdims=True)), sink)
            p_c = jnp.exp(s_c - m)
            p_w = jnp.exp(s_w - m)
            den = (jnp.sum(p_c, axis=-1, keepdims=True) + jnp.sum(p_w, axis=-1, keepdims=True)
                   + jnp.exp(sink - m))
            o = (jnp.dot(p_c.astype(bf16), vc, preferred_element_type=f32)
                 + jnp.dot(p_w.astype(bf16), vw, preferred_element_type=f32)) / den
            o_ref[rows, cols] = o.astype(o_ref.dtype)


def _attn_lat(proj, cache_k, cache_v, l, cos, sin, sink8, rb0=0, n_seq=DEC_BATCH):
    qw = Q_PER_KV * HEAD_DIM
    ck = cache_k.reshape(n_seq, DEPTH, PAST_LEN, ATTN_KV)
    cv = cache_v.reshape(n_seq, DEPTH, PAST_LEN, ATTN_KV)
    return pl.pallas_call(
        _attn_lat_kernel,
        grid=(n_seq, KV_HEADS),
        in_specs=[
            pl.BlockSpec((DEC_SEQ, qw), lambda b, h: (rb0 + b, COL_AQ // qw + h)),
            pl.BlockSpec((DEC_SEQ, HEAD_DIM), lambda b, h: (rb0 + b, COL_AK // HEAD_DIM + h)),
            pl.BlockSpec((DEC_SEQ, HEAD_DIM), lambda b, h: (rb0 + b, COL_AV // HEAD_DIM + h)),
            pl.BlockSpec((None, None, PAST_LEN, HEAD_DIM), lambda b, h: (b, l, 0, h)),
            pl.BlockSpec((None, None, PAST_LEN, HEAD_DIM), lambda b, h: (b, l, 0, h)),
            pl.BlockSpec((DEC_SEQ, HEAD_DIM), lambda b, h: (0, 0)),
            pl.BlockSpec((DEC_SEQ, HEAD_DIM), lambda b, h: (0, 0)),
            pl.BlockSpec((ATTN_HEADS, 128), lambda b, h: (0, 0)),
        ],
        out_specs=pl.BlockSpec((DEC_SEQ, qw), lambda b, h: (b, h)),
        out_shape=jax.ShapeDtypeStruct((n_seq * DEC_SEQ, ATTN_Q), bf16),
        scratch_shapes=[pltpu.VMEM((DEC_SEQ, HEAD_DIM), bf16)],
        compiler_params=_cparams(("parallel", "arbitrary")),
        name="attn_lat",
    )(proj, proj, proj, ck, cv, cos, sin, sink8)


def _rope_tables():
    quarter = HEAD_DIM // 4
    freqs = ROPE_THETA ** (-np.arange(quarter, dtype=np.float32) / quarter)
    t = np.arange(DEC_SEQ)
    cos = np.zeros((DEC_SEQ, HEAD_DIM), np.float32)
    sin = np.zeros((DEC_SEQ, HEAD_DIM), np.float32)
    for half, pos in enumerate((t // GRID_W, t % GRID_W)):
        ang = pos.astype(np.float32)[:, None] * freqs[None, :]
        base = half * 2 * quarter
        cos[:, base:base + quarter] = np.cos(ang)
        cos[:, base + quarter:base + 2 * quarter] = np.cos(ang)
        sin[:, base:base + quarter] = -np.sin(ang)
        sin[:, base + quarter:base + 2 * quarter] = np.sin(ang)
    return jnp.asarray(cos), jnp.asarray(sin)


def _merge_kernel(x_ref, gt_ref, og_ref, os_ref, oa_ref, b0_ref, b1_ref, b2_ref, sn_ref,
                  wg_ref, ws_ref, wa_ref, wo_ref, o_ref):
    y = os_ref[...]
    ms = jnp.mean(y * y, axis=-1, keepdims=True)
    osn = (y * lax.rsqrt(ms + RMS_EPS) * sn_ref[...]).astype(bf16)
    og = og_ref[...]
    oa = oa_ref[...]
    for jt in range(D_MODEL // TJ):
        cols = slice(jt * TJ, (jt + 1) * TJ)
        m = (b0_ref[:, cols].astype(f32) * jnp.dot(og, wg_ref[jt], preferred_element_type=f32)
             + b1_ref[:, cols].astype(f32) * jnp.dot(osn, ws_ref[jt], preferred_element_type=f32)
             + b2_ref[:, cols].astype(f32) * jnp.dot(oa, wa_ref[jt], preferred_element_type=f32))
        part = jnp.dot(m.astype(bf16), wo_ref[jt], preferred_element_type=f32)
        if jt == 0:
            o_ref[...] = part
        else:
            o_ref[...] += part
    o_ref[...] = x_ref[...] + gt_ref[0] * o_ref[...]


def _merge(st, x, mod3, l, proj, o_gla, o_ssm, o_att, ssm_norm, w_g, w_s, w_a, w_o):
    nj = D_MODEL // TJ
    tm = TM

    def br_spec(k):
        return pl.BlockSpec((tm, D_MODEL), lambda i: (i, COL_BR // D_MODEL + k))

    def wbr_spec():
        return pl.BlockSpec((None, nj, GLA_VAL, TJ), lambda i: (l, 0, 0, 0), pipeline_mode=pl.Buffered(1))

    return pl.pallas_call(
        _merge_kernel,
        grid=(st.n_tok // tm,),
        in_specs=[
            pl.BlockSpec((tm, D_MODEL), lambda i: (i, 0)),
            pl.BlockSpec((1, 1, D_MODEL), lambda i: (st.mod_row(i, tm), 0, 5)),
            pl.BlockSpec((tm, GLA_VAL), lambda i: (i, 0)),
            pl.BlockSpec((tm, SSM_INNER), lambda i: (i, 0)),
            pl.BlockSpec((tm, ATTN_Q), lambda i: (i, 0)),
            br_spec(0), br_spec(1), br_spec(2),
            pl.BlockSpec((None, 1, SSM_INNER), lambda i: (l, 0, 0)),
            wbr_spec(), wbr_spec(), wbr_spec(),
            pl.BlockSpec((None, nj, TJ, D_MODEL), lambda i: (l, 0, 0, 0), pipeline_mode=pl.Buffered(1)),
        ],
        out_specs=pl.BlockSpec((tm, D_MODEL), lambda i: (i, 0)),
        out_shape=jax.ShapeDtypeStruct((st.n_tok, D_MODEL), f32),
        compiler_params=_cparams(("arbitrary",)),
        name="merge",
    )(x, mod3, o_gla, o_ssm, o_att, proj, proj, proj, ssm_norm, w_g, w_s, w_a, w_o)


_IN_SRC = np.cumsum((0, GLA_QK, GLA_QK, GLA_VAL, GLA_VAL, 2 * GLA_RANK, SSM_INNER, CONV_CH, 2 * SSM_HEADS,
                     ATTN_Q, ATTN_KV, ATTN_KV, 3 * D_MODEL))
SRC_GDOWN, SRC_SZ, SRC_DT, SRC_AQ, SRC_BR, SRC_END = (int(_IN_SRC[k]) for k in (4, 5, 7, 8, 11, 12))
W_PREP_K = 256
W_PREP_CHUNK = 512


def _w_in_prep_kernel(wt_ref, o_ref):
    def move(src, dst, width):
        for c0 in range(0, width, W_PREP_CHUNK):
            w = min(W_PREP_CHUNK, width - c0)
            o_ref[0, :, dst + c0:dst + c0 + w] = wt_ref[0, src + c0:src + c0 + w, :].T.astype(bf16)

    move(0, COL_GQ, SRC_GDOWN)
    move(SRC_SZ, COL_SZ, SRC_DT - SRC_SZ)
    move(SRC_AQ, COL_AQ, SRC_BR - SRC_AQ)
    move(SRC_BR, COL_BR, SRC_END - SRC_BR)

    n_gd, n_dt = 2 * GLA_RANK, 2 * SSM_HEADS
    t_gd = wt_ref[0, SRC_GDOWN:SRC_GDOWN + n_gd, :].astype(bf16)
    t_dt = wt_ref[0, SRC_DT:SRC_DT + n_dt, :].astype(bf16)
    r = lax.broadcasted_iota(jnp.int32, (SMALL_W, n_gd), 0)
    c = lax.broadcasted_iota(jnp.int32, (SMALL_W, n_gd), 1)
    sel_gd = jnp.where(r == c, 1.0, 0.0).astype(bf16)
    k = r - SMALL_DT
    for g in range(SSM_GROUPS):
        src_row = (k // SSM_HPG) * SSM_HEADS + g * SSM_HPG + k % SSM_HPG
        sel_dt = jnp.where((k >= 0) & (k < 2 * SSM_HPG) & (c == src_row), 1.0, 0.0).astype(bf16)
        blk = (jnp.dot(sel_gd, t_gd, preferred_element_type=f32)
               + jnp.dot(sel_dt, t_dt, preferred_element_type=f32))
        o_ref[0, :, COL_SMALL + g * SMALL_W:COL_SMALL + (g + 1) * SMALL_W] = blk.T.astype(bf16)


def _w_in_prep(w_in):
    wt = jnp.swapaxes(w_in, 1, 2)
    return pl.pallas_call(
        _w_in_prep_kernel,
        grid=(DEPTH, D_MODEL // W_PREP_K),
        in_specs=[pl.BlockSpec((1, SRC_END, W_PREP_K), lambda l, i: (l, 0, i))],
        out_specs=pl.BlockSpec((1, W_PREP_K, D_INP), lambda l, i: (l, i, 0)),
        out_shape=jax.ShapeDtypeStruct((DEPTH, D_MODEL, D_INP), bf16),
        compiler_params=_cparams(("arbitrary", "arbitrary")),
        name="w_in_prep",
    )(wt)


def kernel(x_prompt, x_sample, c, cache_k, cache_v, state_gla, state_ssm, c_ctx, w_mod, b_mod, ffn1_norm,
           ffn1_w_gate, ffn1_w_up, ffn1_w_down, mix_norm, w_in, gla_w_up, gla_b_up, gla_norm, ssm_conv_w,
           ssm_conv_b, ssm_dt_bias, ssm_a_log, ssm_d, ssm_norm, attn_sink, w_br_gla, w_br_ssm, w_br_attn,
           w_out, ffn2_norm, ffn2_w_gate, ffn2_w_up, ffn2_w_down, final_norm):
    xs = {CTX: x_prompt.reshape(N_PROMPT, D_MODEL), LAT: x_sample.reshape(N_SAMPLE, D_MODEL)}
    cvec = jnp.concatenate([c_ctx[None], c, jnp.zeros((MOD_ROWS - 1 - DEC_BATCH, D_MODEL), f32)], axis=0)
    mod = _modulation(cvec, w_mod, b_mod)

    w_in_p = _w_in_prep(w_in)
    nj = D_MODEL // TJ
    wbg, wbs, wba = (w.astype(bf16).reshape(DEPTH, GLA_VAL, nj, TJ).transpose(0, 2, 1, 3)
                     for w in (w_br_gla, w_br_ssm, w_br_attn))
    wo = w_out.astype(bf16).reshape(DEPTH, nj, TJ, D_MODEL)

    wz = jnp.zeros((DEPTH, 2, SMALL_W, GLA_QK), f32)
    for d in range(2):
        wz = wz.at[:, d, d * GLA_RANK:(d + 1) * GLA_RANK, :].set(gla_w_up[:, d])
    wz = wz.astype(bf16)
    bz = gla_b_up.reshape(DEPTH, 2, 1, GLA_QK)

    def dt_lanes(v):
        vg = v.reshape(DEPTH, 2, SSM_GROUPS, SSM_HPG).transpose(0, 2, 1, 3).reshape(DEPTH, SSM_GROUPS, 2 * SSM_HPG)
        row = jnp.zeros((DEPTH, SSM_GROUPS, SMALL_W), f32)
        row = row.at[:, :, SMALL_DT:SMALL_DT + 2 * SSM_HPG].set(vg)
        return row.reshape(DEPTH, 1, SSM_GROUPS * SMALL_W)

    dtb_rows = dt_lanes(ssm_dt_bias)
    a_rows = dt_lanes(-jnp.exp(ssm_a_log))
    d_skip = jnp.repeat(ssm_d, SSM_HEADDIM, axis=-1).reshape(DEPTH, 1, SSM_INNER)
    conv_w = jnp.concatenate([ssm_conv_w, jnp.zeros((DEPTH, 8 - SSM_CONV, CONV_CH), f32)], axis=1)
    conv_b = ssm_conv_b.reshape(DEPTH, 1, CONV_CH)
    sink8 = jnp.broadcast_to(attn_sink[:, :, None], (DEPTH, ATTN_HEADS, 128))
    cos, sin = _rope_tables()

    n1, nm, n2 = (w.reshape(DEPTH, 1, D_MODEL) for w in (ffn1_norm, mix_norm, ffn2_norm))
    sn = ssm_norm.reshape(DEPTH, 1, SSM_INNER)

    new_k, new_v, new_gla, new_ssm = [], [], [], []
    for l in range(DEPTH):
        mod3 = mod[l].reshape(MOD_ROWS, 1, N_MOD * D_MODEL)
        last = l == DEPTH - 1
        for st in (CTX, LAT):
            x = _ffn(st, xs[st], mod3, l, 0, n1, ffn1_w_gate, ffn1_w_up, ffn1_w_down)
            proj, aux = _inproj(st, x, mod3, l, nm, w_in_p)
            if st is CTX:
                o_gla, st_g = _gla(proj, aux, 0, BATCH, SEQ, wz[l], bz[l], gla_norm[l][None], None)
                o_ssm, st_s = _ssd(proj, aux, 0, BATCH, SEQ, conv_w[l], conv_b[l], dtb_rows[l], a_rows[l],
                                   d_skip[l], None)
                o_att = _attn_ctx(proj, sink8[l])
                new_k.append(aux[:, :ATTN_KV].reshape(BATCH, SEQ, KV_HEADS, HEAD_DIM))
                new_v.append(aux[:, ATTN_KV:2 * ATTN_KV].reshape(BATCH, SEQ, KV_HEADS, HEAD_DIM))
                new_gla.append(st_g)
                new_ssm.append(st_s)
            else:
                o_gla, _ = _gla(proj, aux, 0, DEC_BATCH, DEC_SEQ, wz[l], bz[l], gla_norm[l][None], state_gla[:, l])
                o_ssm, _ = _ssd(proj, aux, 0, DEC_BATCH, DEC_SEQ, conv_w[l], conv_b[l], dtb_rows[l], a_rows[l],
                                d_skip[l], state_ssm[:, l])
                o_att = _attn_lat(proj, cache_k, cache_v, l, cos, sin, sink8[l])
            x = _merge(st, x, mod3, l, proj, o_gla, o_ssm, o_att, sn, wbg, wbs, wba, wo)
            xs[st] = _ffn(st, x, mod3, l, 2, n2, ffn2_w_gate, ffn2_w_up, ffn2_w_down,
                          final_w=final_norm if last else None)

    y_prompt = xs[CTX].reshape(BATCH, SEQ, D_MODEL)
    y_sample = xs[LAT].reshape(DEC_BATCH, DEC_SEQ, D_MODEL)
    return (y_prompt, y_sample, jnp.stack(new_k, axis=1), jnp.stack(new_v, axis=1),
            jnp.stack(new_gla, axis=1), jnp.stack(new_ssm, axis=1))
```

```python
import functools
import math
from typing import NamedTuple

import numpy as np
import jax
import jax.numpy as jnp
from jax import lax
from jax.experimental import pallas as pl
from jax.experimental.pallas import tpu as pltpu

f32 = jnp.float32
bf16 = jnp.bfloat16

D_MODEL = 2048
BATCH = 32
SEQ = 256
DEPTH = 2
DEC_BATCH = 2
DEC_SEQ = 1024
PAST_LEN = 512
GRID_W = 64
RMS_EPS = 1e-6
N_MOD = 9
D_FF = 5632
GLA_HEADS = 4
GLA_DK = 128
GLA_DV = 256
GLA_RANK = 16
GLA_GATE_NORM = 16.0
GLA_CHUNK = 16
SSM_HEADS = 16
SSM_HEADDIM = 64
SSM_GROUPS = 4
SSM_HPG = SSM_HEADS // SSM_GROUPS
SSM_STATE = 128
SSM_CONV = 5
SSM_INNER = SSM_HEADS * SSM_HEADDIM
SSM_BC = SSM_GROUPS * SSM_STATE
CONV_CH = SSM_INNER + 2 * SSM_BC
ATTN_HEADS = 8
KV_HEADS = 2
Q_PER_KV = ATTN_HEADS // KV_HEADS
HEAD_DIM = 128
WINDOW = 128
ROPE_THETA = 10000.0
GLA_QK = GLA_HEADS * GLA_DK
GLA_VAL = GLA_HEADS * GLA_DV
ATTN_Q = ATTN_HEADS * HEAD_DIM
ATTN_KV = KV_HEADS * HEAD_DIM

N_PROMPT = BATCH * SEQ
N_SAMPLE = DEC_BATCH * DEC_SEQ
MOD_ROWS = 8

COL_GQ = 0
COL_GK = COL_GQ + GLA_QK
COL_GV = COL_GK + GLA_QK
COL_GR = COL_GV + GLA_VAL
COL_SZ = COL_GR + GLA_VAL
COL_SX = COL_SZ + SSM_INNER
COL_SB = COL_SX + SSM_INNER
COL_SC = COL_SB + SSM_BC
COL_AQ = COL_SC + SSM_BC
COL_AK = COL_AQ + ATTN_Q
COL_AV = COL_AK + ATTN_KV
COL_SMALL = COL_AV + ATTN_KV
SMALL_W = 128
SMALL_DT = 2 * GLA_RANK
COL_BR = 8192
D_INP = COL_BR + 3 * D_MODEL

TM = 256
TM_BIG = 1024
TF_BIG = 256
TN_IN = 1024
AUX_TILE = COL_AK // TN_IN
AUX_SMALL = (COL_SMALL - COL_AK) // SMALL_W
TN_MOD = 1024
TJ = 512
GLA_SB = 256
GLA_TILE = 128
GLA_HPS = 2
SSD_CHUNK = 128
VMEM_LIMIT = 58 * 1024 * 1024

NT = (((1,), (1,)), ((), ()))


def _silu(x):
    return x * jax.nn.sigmoid(x)


class _Stream(NamedTuple):
    n_tok: int
    row0: int
    per_row: int

    def mod_row(self, i, tm):
        return self.row0 + (i * tm) // self.per_row


CTX = _Stream(N_PROMPT, 0, N_PROMPT)
LAT = _Stream(N_SAMPLE, 1, DEC_SEQ)


def _cparams(sem):
    return pltpu.CompilerParams(dimension_semantics=sem, vmem_limit_bytes=VMEM_LIMIT)


def _mod_kernel(c_ref, w_ref, b_ref, o_ref):
    s = _silu(c_ref[...]).astype(bf16)
    o_ref[0] = jnp.dot(s, w_ref[0].astype(bf16), preferred_element_type=f32) + b_ref[0]


def _modulation(cvec, w_mod, b_mod):
    n = N_MOD * D_MODEL
    return pl.pallas_call(
        _mod_kernel,
        grid=(DEPTH, n // TN_MOD),
        in_specs=[
            pl.BlockSpec((MOD_ROWS, D_MODEL), lambda l, j: (0, 0)),
            pl.BlockSpec((1, D_MODEL, TN_MOD), lambda l, j: (l, 0, j)),
            pl.BlockSpec((1, 1, TN_MOD), lambda l, j: (l, 0, j)),
        ],
        out_specs=pl.BlockSpec((1, MOD_ROWS, TN_MOD), lambda l, j: (l, 0, j)),
        out_shape=jax.ShapeDtypeStruct((DEPTH, MOD_ROWS, n), f32),
        compiler_params=_cparams(("arbitrary", "arbitrary")),
        name="modulation",
    )(cvec, w_mod, b_mod.reshape(DEPTH, 1, n))


def _norm_mod(x, nw, sh, sc):
    ms = jnp.mean(x * x, axis=-1, keepdims=True)
    h = x * lax.rsqrt(ms + RMS_EPS) * nw
    return h * (1.0 + sc) + sh


def _ffn_kernel(x_ref, sh_ref, sc_ref, gt_ref, nw_ref, wg_ref, wu_ref, wd_ref, *rest, final):
    if final:
        fw_ref, o_ref, h_ref = rest
    else:
        o_ref, h_ref = rest
    j = pl.program_id(1)

    def hidden_tile():
        h = h_ref[...]
        g = jnp.dot(h, wg_ref[...].astype(bf16), preferred_element_type=f32)
        u = jnp.dot(h, wu_ref[...].astype(bf16), preferred_element_type=f32)
        a = (_silu(g) * u).astype(bf16)
        return jnp.dot(a, wd_ref[...].astype(bf16), preferred_element_type=f32)

    @pl.when(j == 0)
    def _():
        h_ref[...] = _norm_mod(x_ref[...], nw_ref[...], sh_ref[0], sc_ref[0]).astype(bf16)
        o_ref[...] = hidden_tile()

    @pl.when(j > 0)
    def _():
        o_ref[...] += hidden_tile()

    @pl.when(j == pl.num_programs(1) - 1)
    def _():
        y = x_ref[...] + 0.5 * gt_ref[0] * o_ref[...]
        if final:
            ms = jnp.mean(y * y, axis=-1, keepdims=True)
            y = y * lax.rsqrt(ms + RMS_EPS) * fw_ref[...]
        o_ref[...] = y


def _ffn(st, x, mod3, l, slot, nw, wg, wu, wd, final_w=None):
    final = final_w is not None
    tm, tf = TM_BIG, TF_BIG

    def mod_spec(k):
        return pl.BlockSpec((1, 1, D_MODEL), lambda i, j: (st.mod_row(i, tm), 0, 3 * slot + k))

    in_specs = [
        pl.BlockSpec((tm, D_MODEL), lambda i, j: (i, 0)),
        mod_spec(0), mod_spec(1), mod_spec(2),
        pl.BlockSpec((None, 1, D_MODEL), lambda i, j: (l, 0, 0)),
        pl.BlockSpec((None, D_MODEL, tf), lambda i, j: (l, 0, j)),
        pl.BlockSpec((None, D_MODEL, tf), lambda i, j: (l, 0, j)),
        pl.BlockSpec((None, tf, D_MODEL), lambda i, j: (l, j, 0)),
    ]
    args = [x, mod3, mod3, mod3, nw, wg, wu, wd]
    if final:
        in_specs.append(pl.BlockSpec((1, D_MODEL), lambda i, j: (0, 0)))
        args.append(final_w.reshape(1, D_MODEL))
    return pl.pallas_call(
        functools.partial(_ffn_kernel, final=final),
        grid=(st.n_tok // tm, D_FF // tf),
        in_specs=in_specs,
        out_specs=pl.BlockSpec((tm, D_MODEL), lambda i, j: (i, 0)),
        out_shape=jax.ShapeDtypeStruct((st.n_tok, D_MODEL), f32),
        scratch_shapes=[pltpu.VMEM((tm, D_MODEL), bf16)],
        compiler_params=_cparams(("arbitrary", "arbitrary")),
        name="ffn_final" if final else "ffn",
    )(*args)


def _inproj_kernel(x_ref, sh_ref, sc_ref, nw_ref, w_ref, o_ref, os_ref, h_ref):
    j = pl.program_id(1)

    def tile():
        return jnp.dot(h_ref[...], w_ref[...], preferred_element_type=f32)

    @pl.when(j == 0)
    def _():
        h_ref[...] = _norm_mod(x_ref[...], nw_ref[...], sh_ref[0], sc_ref[0]).astype(bf16)
        o_ref[...] = tile().astype(bf16)

    @pl.when((j > 0) & (j < COL_BR // TN_IN) & (j != AUX_TILE))
    def _():
        o_ref[...] = tile().astype(bf16)

    @pl.when(j == AUX_TILE)
    def _():
        t = tile()
        os_ref[...] = t
        o_ref[...] = t.astype(bf16)

    @pl.when(j >= COL_BR // TN_IN)
    def _():
        o_ref[...] = jax.nn.sigmoid(tile()).astype(bf16)


def _inproj(st, x, mod3, l, nw, w):
    tm = TM_BIG

    def mod_spec(k):
        return pl.BlockSpec((1, 1, D_MODEL), lambda i, j: (st.mod_row(i, tm), 0, 3 + k))

    return pl.pallas_call(
        _inproj_kernel,
        grid=(st.n_tok // tm, D_INP // TN_IN),
        in_specs=[
            pl.BlockSpec((tm, D_MODEL), lambda i, j: (i, 0)),
            mod_spec(0), mod_spec(1),
            pl.BlockSpec((None, 1, D_MODEL), lambda i, j: (l, 0, 0)),
            pl.BlockSpec((None, D_MODEL, TN_IN), lambda i, j: (l, 0, j)),
        ],
        out_specs=[pl.BlockSpec((tm, TN_IN), lambda i, j: (i, j)),
                   pl.BlockSpec((tm, TN_IN), lambda i, j: (i, 0))],
        out_shape=[jax.ShapeDtypeStruct((st.n_tok, D_INP), bf16), jax.ShapeDtypeStruct((st.n_tok, TN_IN), f32)],
        scratch_shapes=[pltpu.VMEM((tm, D_MODEL), bf16)],
        compiler_params=_cparams(("arbitrary", "arbitrary")),
        name="inproj",
    )(x, mod3, mod3, nw, w)


def _split_dot(m, x):
    n = x.shape[1]
    hi = x.astype(bf16)
    lo = (x - hi.astype(f32)).astype(bf16)
    r = jnp.dot(m, jnp.concatenate([hi, lo], axis=1), preferred_element_type=f32)
    return r[:, :n] + r[:, n:]


def _log_sigmoid(x):
    return jnp.minimum(x, 0.0) - jnp.log(1.0 + jnp.exp(-jnp.abs(x)))


def _gla_kernel(q_ref, k_ref, v_ref, r_ref, sm_ref, wz_ref, bz_ref, gn_ref, *rest, seq, has_h0, emit_state):
    rest = list(rest)
    h0_ref = rest.pop(0) if has_h0 else None
    o_ref = rest.pop(0)
    st_ref = rest.pop(0) if emit_state else None
    oin_ref, qe_ref, ke_ref, dec_ref, vt_ref = rest
    hps = GLA_HPS
    sb = GLA_SB
    tile = GLA_TILE
    per_tile = tile // GLA_CHUNK
    n_tile = seq // tile

    row = lax.broadcasted_iota(jnp.int32, (sb, sb), 0)
    col = lax.broadcasted_iota(jnp.int32, (sb, sb), 1)
    same = (row // GLA_CHUNK) == (col // GLA_CHUNK)
    masks = (same & (col <= row), same & (col >= row))
    blk = jnp.where(same, 1.0, 0.0).astype(bf16)
    tri_blk = tuple(jnp.concatenate([jnp.where(m, 1.0, 0.0).astype(bf16), blk], axis=0) for m in masks)

    chains = [(hh, d) for hh in range(hps) for d in range(2)]
    kcols = [slice(hh * GLA_DK, (hh + 1) * GLA_DK) for hh in range(hps)]
    vcols = [slice(hh * GLA_DV, (hh + 1) * GLA_DV) for hh in range(hps)]

    for s in range(seq // sb):
        rows = slice(s * sb, (s + 1) * sb)
        sm = sm_ref[rows, :].astype(bf16)
        q = [q_ref[rows, kcols[hh]].astype(f32) * (GLA_DK ** -0.5) for hh in range(hps)]
        k = [k_ref[rows, kcols[hh]].astype(f32) for hh in range(hps)]
        v = [v_ref[rows, vcols[hh]] for hh in range(hps)]
        for hh in range(hps):
            v_t = v[hh].astype(f32).T.astype(bf16)
            for t in range(sb // tile):
                vt_ref[hh, s * (sb // tile) + t] = v_t[:, t * tile:(t + 1) * tile]
        gz = [jnp.dot(sm, wz_ref[d, :, kcols[hh]], preferred_element_type=f32) + bz_ref[d, :, kcols[hh]]
              for hh, d in chains]
        la = [_log_sigmoid(g) * (1.0 / GLA_GATE_NORM) for g in gz]
        sums = [_split_dot(tri_blk[d], la[i]) for i, (hh, d) in enumerate(chains)]
        cum = [x[:sb] for x in sums]
        tot = [x[sb:] for x in sums]
        qe = [q[hh] * jnp.exp(cum[i]) for i, (hh, d) in enumerate(chains)]
        kinv = [k[hh] * jnp.exp(-cum[i]) for i, (hh, d) in enumerate(chains)]
        kend = [k[hh] * jnp.exp(tot[i] - cum[i]) for i, (hh, d) in enumerate(chains)]
        a = [lax.dot_general(qe[i].astype(bf16), kinv[i].astype(bf16), NT, preferred_element_type=f32)
             for i in range(len(chains))]
        a = [jnp.where(masks[d], a[i], 0.0).astype(bf16) for i, (hh, d) in enumerate(chains)]
        for i, (hh, d) in enumerate(chains):
            oin_ref[hh, d, rows, :] = jnp.dot(a[i], v[hh], preferred_element_type=f32)
            qe_ref[hh, d, rows, :] = qe[i].astype(bf16)
            ke_ref[hh, d, rows, :] = kend[i].astype(bf16)
            dec_ref[hh, d, rows, :] = jnp.exp(tot[i])

    if has_h0:
        s_init = tuple(h0_ref[d, hh].T for hh, d in chains)
    else:
        s_init = (jnp.zeros((GLA_DV, GLA_DK), f32),) * len(chains)

    def scan_tile(o, carry):
        sts = list(carry)
        for u in range(per_tile):
            for ci, (hh, d) in enumerate(chains):
                t_idx = o if d == 0 else n_tile - 1 - o
                pos = u if d == 0 else per_tile - 1 - u
                i0 = t_idx * tile + pos * GLA_CHUNK
                if not isinstance(i0, int):
                    i0 = pl.multiple_of(i0, GLA_CHUNK)
                rows = pl.ds(i0, GLA_CHUNK)
                inter = lax.dot_general(qe_ref[hh, d, rows, :], sts[ci].astype(bf16), NT,
                                        preferred_element_type=f32)
                oin_ref[hh, d, rows, :] += inter
                pieces = [ke_ref[hh, d, rows, :]]
                if pos:
                    pieces.insert(0, jnp.zeros((pos * GLA_CHUNK, GLA_DK), bf16))
                if pos < per_tile - 1:
                    pieces.append(jnp.zeros(((per_tile - 1 - pos) * GLA_CHUNK, GLA_DK), bf16))
                upd = jnp.dot(vt_ref[hh, t_idx], jnp.concatenate(pieces, axis=0), preferred_element_type=f32)
                sts[ci] = sts[ci] * dec_ref[hh, d, pl.ds(i0, 1), :] + upd
        return tuple(sts)

    if n_tile <= 2:
        s_fin = s_init
        for o in range(n_tile):
            s_fin = scan_tile(o, s_fin)
    else:
        s_fin = lax.fori_loop(0, n_tile, scan_tile, s_init)

    for hh in range(hps):
        vcols = slice(hh * GLA_DV, (hh + 1) * GLA_DV)
        o = oin_ref[hh, 0] + oin_ref[hh, 1]
        ms = jnp.mean(o * o, axis=-1, keepdims=True)
        o = o * lax.rsqrt(ms + RMS_EPS) * gn_ref[...]
        o_ref[:, vcols] = (o * _silu(r_ref[:, vcols].astype(f32))).astype(o_ref.dtype)
    if emit_state:
        for ci, (hh, d) in enumerate(chains):
            st_ref[0, d, hh] = s_fin[ci].T


def _gla(proj, small, row_blk0, n_seq, seq, wz, bz, gn, h0):
    has_h0 = h0 is not None
    emit_state = not has_h0
    kq, kv_, hps = GLA_DK, GLA_DV, GLA_HPS
    kw, vw = hps * kq, hps * kv_
    in_specs = [
        pl.BlockSpec((seq, kw), lambda b, h: (row_blk0 + b, COL_GQ // kw + h)),
        pl.BlockSpec((seq, kw), lambda b, h: (row_blk0 + b, COL_GK // kw + h)),
        pl.BlockSpec((seq, vw), lambda b, h: (row_blk0 + b, COL_GV // vw + h)),
        pl.BlockSpec((seq, vw), lambda b, h: (row_blk0 + b, COL_GR // vw + h)),
        pl.BlockSpec((seq, SMALL_W), lambda b, h: (row_blk0 + b, AUX_SMALL)),
        pl.BlockSpec((2, SMALL_W, kw), lambda b, h: (0, 0, h)),
        pl.BlockSpec((2, 1, kw), lambda b, h: (0, 0, h)),
        pl.BlockSpec((1, kv_), lambda b, h: (0, 0)),
    ]
    args = [proj, proj, proj, proj, small, wz, bz, gn]
    if has_h0:
        in_specs.append(pl.BlockSpec((None, 2, hps, kq, kv_), lambda b, h: (b, 0, h, 0, 0)))
        args.append(h0)
    out_specs = [pl.BlockSpec((seq, vw), lambda b, h: (b, h))]
    out_shape = [jax.ShapeDtypeStruct((n_seq * seq, GLA_VAL), bf16)]
    if emit_state:
        out_specs.append(pl.BlockSpec((1, 2, hps, kq, kv_), lambda b, h: (b, 0, h, 0, 0)))
        out_shape.append(jax.ShapeDtypeStruct((n_seq, 2, GLA_HEADS, kq, kv_), f32))
    res = pl.pallas_call(
        functools.partial(_gla_kernel, seq=seq, has_h0=has_h0, emit_state=emit_state),
        grid=(n_seq, GLA_HEADS // hps),
        in_specs=in_specs,
        out_specs=out_specs,
        out_shape=out_shape,
        scratch_shapes=[
            pltpu.VMEM((hps, 2, seq, kv_), f32),
            pltpu.VMEM((hps, 2, seq, kq), bf16),
            pltpu.VMEM((hps, 2, seq, kq), bf16),
            pltpu.VMEM((hps, 2, seq, kq), f32),
            pltpu.VMEM((hps, seq // GLA_TILE, kv_, GLA_TILE), bf16),
        ],
        compiler_params=_cparams(("parallel", "arbitrary")),
        name="gla_lat" if has_h0 else "gla_ctx",
    )(*args)
    return res if emit_state else (res[0], None)


def _softplus(x):
    return jnp.maximum(x, 0.0) + jnp.log(1.0 + jnp.exp(-jnp.abs(x)))


def _split3(x):
    hi = x.astype(bf16)
    r1 = x - hi.astype(f32)
    mid = r1.astype(bf16)
    lo = (r1 - mid.astype(f32)).astype(bf16)
    return [hi, mid, lo]


def _split3_dot(m, x):
    n = x.shape[1]
    r = jnp.dot(m, jnp.concatenate(_split3(x), axis=1), preferred_element_type=f32)
    return r[:, :n] + r[:, n:2 * n] + r[:, 2 * n:]


def _dot_sel(x, sel):
    return sum(jnp.dot(t, sel, preferred_element_type=f32) for t in _split3(x))


def _shift_rows(x, d, t_idx):
    n = x.shape[0]
    if d == 0:
        return x
    y = pltpu.roll(x, (-d) % n, 0)
    ok = (t_idx + d >= 0) & (t_idx + d < n)
    return jnp.where(ok, y, 0.0)


def _ssd_kernel(z_ref, x_ref, b_ref, c_ref, sm_ref, cwx_ref, cwb_ref, cwc_ref, cbx_ref, cbb_ref, cbc_ref,
                dtb_ref, a_ref, dsk_ref, *rest, seq, has_h0, emit_state):
    rest = list(rest)
    h0_ref = rest.pop(0) if has_h0 else None
    o_ref = rest.pop(0)
    st_ref = rest.pop(0) if emit_state else None
    y_ref, xs_ref, bm_ref, cm_ref, dt_ref = rest
    cs = SSD_CHUNK
    n_chunk = seq // cs
    hp = SSM_HPG * SSM_HEADDIM

    t_idx = lax.broadcasted_iota(jnp.int32, (seq, 1), 0)

    def conv_silu(src_ref, w_ref, bias_ref):
        xin = src_ref[...].astype(f32)
        acc = jnp.zeros_like(xin) + bias_ref[...]
        for j in range(SSM_CONV):
            acc = acc + w_ref[j:j + 1, :] * _shift_rows(xin, j - SSM_CONV // 2, t_idx)
        return _silu(acc)

    xs_ref[...] = conv_silu(x_ref, cwx_ref, cbx_ref)
    bm_ref[...] = conv_silu(b_ref, cwb_ref, cbb_ref)
    cm_ref[...] = conv_silu(c_ref, cwc_ref, cbc_ref)

    row = lax.broadcasted_iota(jnp.int32, (cs, cs), 0)
    col = lax.broadcasted_iota(jnp.int32, (cs, cs), 1)
    masks = (col <= row, col >= row)
    tris = tuple(jnp.where(m, 1.0, 0.0).astype(bf16) for m in masks)
    lane = lax.broadcasted_iota(jnp.int32, (1, SMALL_W), 1)

    dt_lane = (lane >= SMALL_DT) & (lane < SMALL_DT + 2 * SSM_HPG)
    a_row = a_ref[...]
    dt_bias = dtb_ref[...]

    def selectors(d):
        r = lax.broadcasted_iota(jnp.int32, (SMALL_W, SSM_HPG * cs), 0)
        c = lax.broadcasted_iota(jnp.int32, (SMALL_W, SSM_HPG * cs), 1)
        bcast = jnp.where(r == SMALL_DT + d * SSM_HPG + c // cs, 1.0, 0.0).astype(bf16)
        r = lax.broadcasted_iota(jnp.int32, (SMALL_W, hp), 0)
        c = lax.broadcasted_iota(jnp.int32, (SMALL_W, hp), 1)
        expand = jnp.where(r == SMALL_DT + d * SSM_HPG + c // SSM_HEADDIM, 1.0, 0.0).astype(bf16)
        return jnp.concatenate([bcast, expand], axis=1)

    head_of_lane = lax.broadcasted_iota(jnp.int32, (cs, hp), 1) // SSM_HEADDIM
    bm_t = [bm_ref[c * cs:(c + 1) * cs, :].T.astype(bf16) for c in range(n_chunk)]

    dt_ref[...] = jnp.where(dt_lane, _softplus(sm_ref[...] + dt_bias), 0.0)
    y_ref[...] = jnp.zeros_like(y_ref)
    sel = [selectors(0), selectors(1)]
    if has_h0:
        st = [h0_ref[d].reshape(hp, SSM_STATE).T for d in range(2)]
    else:
        st = [jnp.zeros((SSM_STATE, hp), f32)] * 2
    for t in range(n_chunk):
        cidx = [t, n_chunk - 1 - t]
        rows = [slice(c * cs, (c + 1) * cs) for c in cidx]
        dirs = range(2)
        dt = [dt_ref[rows[d], :] for d in dirs]
        cum = [_split3_dot(tris[d], dt[d] * a_row) for d in dirs]
        cum_t = [cum[d].T for d in dirs]
        cum_s = [_dot_sel(cum[d], sel[d]) for d in dirs]
        cum_b = [x[:, :SSM_HPG * cs] for x in cum_s]
        cum_e = [x[:, SSM_HPG * cs:] for x in cum_s]
        dt_e = [_dot_sel(dt[d], sel[d][:, SSM_HPG * cs:]) for d in dirs]
        tot_e = [cum_e[0][cs - 1:cs, :], cum_e[1][0:1, :]]
        xs = [xs_ref[rows[d], :] for d in dirs]
        cm = [cm_ref[rows[d], :].astype(bf16) for d in dirs]
        cb = [lax.dot_general(cm[d], bm_ref[rows[d], :].astype(bf16), NT, preferred_element_type=f32)
              for d in dirs]
        seg = [[jnp.exp(jnp.where(masks[d], cum_b[d][:, j * cs:(j + 1) * cs]
                                  - cum_t[d][SMALL_DT + d * SSM_HPG + j:SMALL_DT + d * SSM_HPG + j + 1, :],
                                  -jnp.inf)) for j in range(SSM_HPG)] for d in dirs]
        w = [jnp.concatenate([(cb[d] * seg[d][j]).astype(bf16) for j in range(SSM_HPG)], axis=1)
             for d in dirs]
        xd = [xs[d] * dt_e[d] for d in dirs]
        xd_bd = [jnp.concatenate([jnp.where(head_of_lane == j, xd[d], 0.0).astype(bf16)
                                  for j in range(SSM_HPG)], axis=0) for d in dirs]
        inter = [jnp.dot(cm[d], st[d].astype(bf16), preferred_element_type=f32) for d in dirs]
        y_new = [jnp.dot(w[d], xd_bd[d], preferred_element_type=f32) + jnp.exp(cum_e[d]) * inter[d]
                 for d in dirs]
        xw = [(xd[d] * jnp.exp(tot_e[d] - cum_e[d])).astype(bf16) for d in dirs]
        st = [st[d] * jnp.exp(tot_e[d]) + jnp.dot(bm_t[cidx[d]], xw[d], preferred_element_type=f32)
              for d in dirs]
        for d in dirs:
            y_ref[rows[d], :] += y_new[d]
    states = st

    y = y_ref[...] + dsk_ref[...] * xs_ref[...]
    o_ref[...] = y * _silu(z_ref[...].astype(f32))
    if emit_state:
        st_ref[0, 0] = states[0].T.reshape(SSM_HPG, SSM_HEADDIM, SSM_STATE)
        st_ref[0, 1] = states[1].T.reshape(SSM_HPG, SSM_HEADDIM, SSM_STATE)


def _ssd(proj, small, row_blk0, n_seq, seq, conv_w, conv_b, dt_bias_row, a_row, d_skip, h0):
    has_h0 = h0 is not None
    emit_state = not has_h0
    hp = SSM_HPG * SSM_HEADDIM
    ns = SSM_STATE
    in_specs = [
        pl.BlockSpec((seq, hp), lambda b, g: (row_blk0 + b, COL_SZ // hp + g)),
        pl.BlockSpec((seq, hp), lambda b, g: (row_blk0 + b, COL_SX // hp + g)),
        pl.BlockSpec((seq, ns), lambda b, g: (row_blk0 + b, COL_SB // ns + g)),
        pl.BlockSpec((seq, ns), lambda b, g: (row_blk0 + b, COL_SC // ns + g)),
        pl.BlockSpec((seq, SMALL_W), lambda b, g: (row_blk0 + b, AUX_SMALL + g)),
        pl.BlockSpec((8, hp), lambda b, g: (0, g)),
        pl.BlockSpec((8, ns), lambda b, g: (0, SSM_INNER // ns + g)),
        pl.BlockSpec((8, ns), lambda b, g: (0, (SSM_INNER + SSM_BC) // ns + g)),
        pl.BlockSpec((1, hp), lambda b, g: (0, g)),
        pl.BlockSpec((1, ns), lambda b, g: (0, SSM_INNER // ns + g)),
        pl.BlockSpec((1, ns), lambda b, g: (0, (SSM_INNER + SSM_BC) // ns + g)),
        pl.BlockSpec((1, SMALL_W), lambda b, g: (0, g)),
        pl.BlockSpec((1, SMALL_W), lambda b, g: (0, g)),
        pl.BlockSpec((1, hp), lambda b, g: (0, g)),
    ]
    args = [proj, proj, proj, proj, small, conv_w, conv_w, conv_w, conv_b, conv_b, conv_b,
            dt_bias_row, a_row, d_skip]
    if has_h0:
        in_specs.append(pl.BlockSpec((None, 2, SSM_HPG, SSM_HEADDIM, ns), lambda b, g: (b, 0, g, 0, 0)))
        args.append(h0)
    out_specs = [pl.BlockSpec((seq, hp), lambda b, g: (b, g))]
    out_shape = [jax.ShapeDtypeStruct((n_seq * seq, SSM_INNER), f32)]
    if emit_state:
        out_specs.append(pl.BlockSpec((1, 2, SSM_HPG, SSM_HEADDIM, ns), lambda b, g: (b, 0, g, 0, 0)))
        out_shape.append(jax.ShapeDtypeStruct((n_seq, 2, SSM_HEADS, SSM_HEADDIM, ns), f32))
    res = pl.pallas_call(
        functools.partial(_ssd_kernel, seq=seq, has_h0=has_h0, emit_state=emit_state),
        grid=(n_seq, SSM_GROUPS),
        in_specs=in_specs,
        out_specs=out_specs,
        out_shape=out_shape,
        scratch_shapes=[
            pltpu.VMEM((seq, hp), f32),
            pltpu.VMEM((seq, hp), f32),
            pltpu.VMEM((seq, ns), f32),
            pltpu.VMEM((seq, ns), f32),
            pltpu.VMEM((seq, SMALL_W), f32),
        ],
        compiler_params=_cparams(("parallel", "arbitrary")),
        name="ssd_lat" if has_h0 else "ssd_ctx",
    )(*args)
    return res if emit_state else (res[0], None)


def _attn_ctx_kernel(q_ref, k_ref, v_ref, sink_ref, o_ref):
    heads = range(ATTN_HEADS)
    kv = [slice(h // Q_PER_KV * HEAD_DIM, (h // Q_PER_KV + 1) * HEAD_DIM) for h in heads]
    cols = [slice(h * HEAD_DIM, (h + 1) * HEAD_DIM) for h in heads]
    s = [lax.dot_general(q_ref[:, cols[h]], k_ref[:, kv[h]], NT, preferred_element_type=f32) * (HEAD_DIM ** -0.5)
         for h in heads]
    sink = [sink_ref[h:h + 1, 0:1] for h in heads]
    m = [jnp.maximum(jnp.max(s[h], axis=-1, keepdims=True), sink[h]) for h in heads]
    p = [jnp.exp(s[h] - m[h]) for h in heads]
    den = [jnp.sum(p[h], axis=-1, keepdims=True) + jnp.exp(sink[h] - m[h]) for h in heads]
    for h in heads:
        o = jnp.dot(p[h].astype(bf16), v_ref[:, kv[h]], preferred_element_type=f32) / den[h]
        o_ref[:, cols[h]] = o.astype(o_ref.dtype)


def _attn_ctx(proj, sink8, n_seq=BATCH):
    return pl.pallas_call(
        _attn_ctx_kernel,
        grid=(n_seq,),
        in_specs=[
            pl.BlockSpec((SEQ, ATTN_Q), lambda b: (b, COL_AQ // ATTN_Q)),
            pl.BlockSpec((SEQ, ATTN_KV), lambda b: (b, COL_AK // ATTN_KV)),
            pl.BlockSpec((SEQ, ATTN_KV), lambda b: (b, COL_AV // ATTN_KV)),
            pl.BlockSpec((ATTN_HEADS, 128), lambda b: (0, 0)),
        ],
        out_specs=pl.BlockSpec((SEQ, ATTN_Q), lambda b: (b, 0)),
        out_shape=jax.ShapeDtypeStruct((n_seq * SEQ, ATTN_Q), bf16),
        compiler_params=_cparams(("arbitrary",)),
        name="attn_ctx",
    )(proj, proj, proj, sink8)


def _rope(x, cos, sin_signed):
    quarter = HEAD_DIM // 4
    lane = lax.broadcasted_iota(jnp.int32, x.shape, 1)
    first = (lane % (2 * quarter)) < quarter
    partner = jnp.where(first, pltpu.roll(x, HEAD_DIM - quarter, 1), pltpu.roll(x, quarter, 1))
    return x * cos + partner * sin_signed


def _attn_lat_kernel(q_ref, k_ref, v_ref, kc_ref, vc_ref, cos_ref, sin_ref, sink_ref, o_ref, kr_ref):
    kvh = pl.program_id(1)
    cos = cos_ref[...]
    sin = sin_ref[...]
    kr_ref[...] = _rope(k_ref[...].astype(f32), cos, sin).astype(bf16)
    kc = kc_ref[...].astype(bf16)
    vc = vc_ref[...].astype(bf16)
    blk = WINDOW
    n_blk = DEC_SEQ // blk
    scale = HEAD_DIM ** -0.5
    for i in range(n_blk):
        lo = max(i - 1, 0) * blk
        hi = min(i + 2, n_blk) * blk
        kw = kr_ref[lo:hi, :]
        vw = v_ref[lo:hi, :].astype(bf16)
        qpos = i * blk + lax.broadcasted_iota(jnp.int32, (blk, hi - lo), 0)
        kpos = lo + lax.broadcasted_iota(jnp.int32, (blk, hi - lo), 1)
        win = jnp.abs(qpos - kpos) <= WINDOW
        rows = slice(i * blk, (i + 1) * blk)
        grp = range(Q_PER_KV)
        cols = [slice(g * HEAD_DIM, (g + 1) * HEAD_DIM) for g in grp]
        q = [_rope(q_ref[rows, cols[g]].astype(f32), cos[rows, :], sin[rows, :]).astype(bf16) for g in grp]
        s_c = [lax.dot_general(q[g], kc, NT, preferred_element_type=f32) * scale for g in grp]
        s_w = [jnp.where(win, lax.dot_general(q[g], kw, NT, preferred_element_type=f32) * scale, -jnp.inf)
               for g in grp]
        sink = [sink_ref[pl.ds(kvh * Q_PER_KV + g, 1), 0:1] for g in grp]
        m = [jnp.maximum(jnp.maximum(jnp.max(s_c[g], axis=-1, keepdims=True),
                                     jnp.max(s_w[g], axis=-1, keepdims=True)), sink[g]) for g in grp]
        p_c = [jnp.exp(s_c[g] - m[g]) for g in grp]
        p_w = [jnp.exp(s_w[g] - m[g]) for g in grp]
        den = [jnp.sum(p_c[g], axis=-1, keepdims=True) + jnp.sum(p_w[g], axis=-1, keepdims=True)
               + jnp.exp(sink[g] - m[g]) for g in grp]
        for g in grp:
            o = (jnp.dot(p_c[g].astype(bf16), vc, preferred_element_type=f32)
                 + jnp.dot(p_w[g].astype(bf16), vw, preferred_element_type=f32)) / den[g]
            o_ref[rows, cols[g]] = o.astype(o_ref.dtype)


def _attn_lat(proj, cache_k, cache_v, l, cos, sin, sink8, rb0=0, n_seq=DEC_BATCH):
    qw = Q_PER_KV * HEAD_DIM
    ck = cache_k.reshape(n_seq, DEPTH, PAST_LEN, ATTN_KV)
    cv = cache_v.reshape(n_seq, DEPTH, PAST_LEN, ATTN_KV)
    return pl.pallas_call(
        _attn_lat_kernel,
        grid=(n_seq, KV_HEADS),
        in_specs=[
            pl.BlockSpec((DEC_SEQ, qw), lambda b, h: (rb0 + b, COL_AQ // qw + h)),
            pl.BlockSpec((DEC_SEQ, HEAD_DIM), lambda b, h: (rb0 + b, COL_AK // HEAD_DIM + h)),
            pl.BlockSpec((DEC_SEQ, HEAD_DIM), lambda b, h: (rb0 + b, COL_AV // HEAD_DIM + h)),
            pl.BlockSpec((None, None, PAST_LEN, HEAD_DIM), lambda b, h: (b, l, 0, h)),
            pl.BlockSpec((None, None, PAST_LEN, HEAD_DIM), lambda b, h: (b, l, 0, h)),
            pl.BlockSpec((DEC_SEQ, HEAD_DIM), lambda b, h: (0, 0)),
            pl.BlockSpec((DEC_SEQ, HEAD_DIM), lambda b, h: (0, 0)),
            pl.BlockSpec((ATTN_HEADS, 128), lambda b, h: (0, 0)),
        ],
        out_specs=pl.BlockSpec((DEC_SEQ, qw), lambda b, h: (b, h)),
        out_shape=jax.ShapeDtypeStruct((n_seq * DEC_SEQ, ATTN_Q), bf16),
        scratch_shapes=[pltpu.VMEM((DEC_SEQ, HEAD_DIM), bf16)],
        compiler_params=_cparams(("parallel", "arbitrary")),
        name="attn_lat",
    )(proj, proj, proj, ck, cv, cos, sin, sink8)


def _rope_tables():
    quarter = HEAD_DIM // 4
    freqs = ROPE_THETA ** (-np.arange(quarter, dtype=np.float32) / quarter)
    t = np.arange(DEC_SEQ)
    cos = np.zeros((DEC_SEQ, HEAD_DIM), np.float32)
    sin = np.zeros((DEC_SEQ, HEAD_DIM), np.float32)
    for half, pos in enumerate((t // GRID_W, t % GRID_W)):
        ang = pos.astype(np.float32)[:, None] * freqs[None, :]
        base = half * 2 * quarter
        cos[:, base:base + quarter] = np.cos(ang)
        cos[:, base + quarter:base + 2 * quarter] = np.cos(ang)
        sin[:, base:base + quarter] = -np.sin(ang)
        sin[:, base + quarter:base + 2 * quarter] = np.sin(ang)
    return jnp.asarray(cos), jnp.asarray(sin)


def _merge_kernel(x_ref, gt_ref, og_ref, os_ref, oa_ref, b0_ref, b1_ref, b2_ref, sn_ref,
                  wg_ref, ws_ref, wa_ref, wo_ref, o_ref):
    y = os_ref[...]
    ms = jnp.mean(y * y, axis=-1, keepdims=True)
    osn = (y * lax.rsqrt(ms + RMS_EPS) * sn_ref[...]).astype(bf16)
    og = og_ref[...]
    oa = oa_ref[...]
    for jt in range(D_MODEL // TJ):
        cols = slice(jt * TJ, (jt + 1) * TJ)
        m = (b0_ref[:, cols].astype(f32) * jnp.dot(og, wg_ref[jt], preferred_element_type=f32)
             + b1_ref[:, cols].astype(f32) * jnp.dot(osn, ws_ref[jt], preferred_element_type=f32)
             + b2_ref[:, cols].astype(f32) * jnp.dot(oa, wa_ref[jt], preferred_element_type=f32))
        part = jnp.dot(m.astype(bf16), wo_ref[jt], preferred_element_type=f32)
        if jt == 0:
            o_ref[...] = part
        else:
            o_ref[...] += part
    o_ref[...] = x_ref[...] + gt_ref[0] * o_ref[...]


def _merge(st, x, mod3, l, proj, o_gla, o_ssm, o_att, ssm_norm, w_g, w_s, w_a, w_o):
    nj = D_MODEL // TJ
    tm = TM

    def br_spec(k):
        return pl.BlockSpec((tm, D_MODEL), lambda i: (i, COL_BR // D_MODEL + k))

    def wbr_spec():
        return pl.BlockSpec((None, nj, GLA_VAL, TJ), lambda i: (l, 0, 0, 0), pipeline_mode=pl.Buffered(1))

    return pl.pallas_call(
        _merge_kernel,
        grid=(st.n_tok // tm,),
        in_specs=[
            pl.BlockSpec((tm, D_MODEL), lambda i: (i, 0)),
            pl.BlockSpec((1, 1, D_MODEL), lambda i: (st.mod_row(i, tm), 0, 5)),
            pl.BlockSpec((tm, GLA_VAL), lambda i: (i, 0)),
            pl.BlockSpec((tm, SSM_INNER), lambda i: (i, 0)),
            pl.BlockSpec((tm, ATTN_Q), lambda i: (i, 0)),
            br_spec(0), br_spec(1), br_spec(2),
            pl.BlockSpec((None, 1, SSM_INNER), lambda i: (l, 0, 0)),
            wbr_spec(), wbr_spec(), wbr_spec(),
            pl.BlockSpec((None, nj, TJ, D_MODEL), lambda i: (l, 0, 0, 0), pipeline_mode=pl.Buffered(1)),
        ],
        out_specs=pl.BlockSpec((tm, D_MODEL), lambda i: (i, 0)),
        out_shape=jax.ShapeDtypeStruct((st.n_tok, D_MODEL), f32),
        compiler_params=_cparams(("arbitrary",)),
        name="merge",
    )(x, mod3, o_gla, o_ssm, o_att, proj, proj, proj, ssm_norm, w_g, w_s, w_a, w_o)


_IN_SRC = np.cumsum((0, GLA_QK, GLA_QK, GLA_VAL, GLA_VAL, 2 * GLA_RANK, SSM_INNER, CONV_CH, 2 * SSM_HEADS,
                     ATTN_Q, ATTN_KV, ATTN_KV, 3 * D_MODEL))
SRC_GDOWN, SRC_SZ, SRC_DT, SRC_AQ, SRC_BR, SRC_END = (int(_IN_SRC[k]) for k in (4, 5, 7, 8, 11, 12))
W_PREP_K = 256
W_PREP_CHUNK = 512


def _w_in_prep_kernel(wt_ref, o_ref):
    def move(src, dst, width):
        for c0 in range(0, width, W_PREP_CHUNK):
            w = min(W_PREP_CHUNK, width - c0)
            o_ref[0, :, dst + c0:dst + c0 + w] = wt_ref[0, src + c0:src + c0 + w, :].T.astype(bf16)

    move(0, COL_GQ, SRC_GDOWN)
    move(SRC_SZ, COL_SZ, SRC_DT - SRC_SZ)
    move(SRC_AQ, COL_AQ, SRC_BR - SRC_AQ)
    move(SRC_BR, COL_BR, SRC_END - SRC_BR)

    n_gd, n_dt = 2 * GLA_RANK, 2 * SSM_HEADS
    t_gd = wt_ref[0, SRC_GDOWN:SRC_GDOWN + n_gd, :].astype(bf16)
    t_dt = wt_ref[0, SRC_DT:SRC_DT + n_dt, :].astype(bf16)
    r = lax.broadcasted_iota(jnp.int32, (SMALL_W, n_gd), 0)
    c = lax.broadcasted_iota(jnp.int32, (SMALL_W, n_gd), 1)
    sel_gd = jnp.where(r == c, 1.0, 0.0).astype(bf16)
    k = r - SMALL_DT
    for g in range(SSM_GROUPS):
        src_row = (k // SSM_HPG) * SSM_HEADS + g * SSM_HPG + k % SSM_HPG
        sel_dt = jnp.where((k >= 0) & (k < 2 * SSM_HPG) & (c == src_row), 1.0, 0.0).astype(bf16)
        blk = (jnp.dot(sel_gd, t_gd, preferred_element_type=f32)
               + jnp.dot(sel_dt, t_dt, preferred_element_type=f32))
        o_ref[0, :, COL_SMALL + g * SMALL_W:COL_SMALL + (g + 1) * SMALL_W] = blk.T.astype(bf16)


def _w_in_prep(w_in):
    wt = jnp.swapaxes(w_in, 1, 2)
    return pl.pallas_call(
        _w_in_prep_kernel,
        grid=(DEPTH, D_MODEL // W_PREP_K),
        in_specs=[pl.BlockSpec((1, SRC_END, W_PREP_K), lambda l, i: (l, 0, i))],
        out_specs=pl.BlockSpec((1, W_PREP_K, D_INP), lambda l, i: (l, i, 0)),
        out_shape=jax.ShapeDtypeStruct((DEPTH, D_MODEL, D_INP), bf16),
        compiler_params=_cparams(("arbitrary", "arbitrary")),
        name="w_in_prep",
    )(wt)


def kernel(x_prompt, x_sample, c, cache_k, cache_v, state_gla, state_ssm, c_ctx, w_mod, b_mod, ffn1_norm,
           ffn1_w_gate, ffn1_w_up, ffn1_w_down, mix_norm, w_in, gla_w_up, gla_b_up, gla_norm, ssm_conv_w,
           ssm_conv_b, ssm_dt_bias, ssm_a_log, ssm_d, ssm_norm, attn_sink, w_br_gla, w_br_ssm, w_br_attn,
           w_out, ffn2_norm, ffn2_w_gate, ffn2_w_up, ffn2_w_down, final_norm):
    xs = {CTX: x_prompt.reshape(N_PROMPT, D_MODEL), LAT: x_sample.reshape(N_SAMPLE, D_MODEL)}
    cvec = jnp.concatenate([c_ctx[None], c, jnp.zeros((MOD_ROWS - 1 - DEC_BATCH, D_MODEL), f32)], axis=0)
    mod = _modulation(cvec, w_mod, b_mod)

    w_in_p = _w_in_prep(w_in)
    nj = D_MODEL // TJ
    wbg, wbs, wba = (w.astype(bf16).reshape(DEPTH, GLA_VAL, nj, TJ).transpose(0, 2, 1, 3)
                     for w in (w_br_gla, w_br_ssm, w_br_attn))
    wo = w_out.astype(bf16).reshape(DEPTH, nj, TJ, D_MODEL)

    wz = jnp.zeros((DEPTH, 2, SMALL_W, GLA_QK), f32)
    for d in range(2):
        wz = wz.at[:, d, d * GLA_RANK:(d + 1) * GLA_RANK, :].set(gla_w_up[:, d])
    wz = wz.astype(bf16)
    bz = gla_b_up.reshape(DEPTH, 2, 1, GLA_QK)

    def dt_lanes(v):
        vg = v.reshape(DEPTH, 2, SSM_GROUPS, SSM_HPG).transpose(0, 2, 1, 3).reshape(DEPTH, SSM_GROUPS, 2 * SSM_HPG)
        row = jnp.zeros((DEPTH, SSM_GROUPS, SMALL_W), f32)
        row = row.at[:, :, SMALL_DT:SMALL_DT + 2 * SSM_HPG].set(vg)
        return row.reshape(DEPTH, 1, SSM_GROUPS * SMALL_W)

    dtb_rows = dt_lanes(ssm_dt_bias)
    a_rows = dt_lanes(-jnp.exp(ssm_a_log))
    d_skip = jnp.repeat(ssm_d, SSM_HEADDIM, axis=-1).reshape(DEPTH, 1, SSM_INNER)
    conv_w = jnp.concatenate([ssm_conv_w, jnp.zeros((DEPTH, 8 - SSM_CONV, CONV_CH), f32)], axis=1)
    conv_b = ssm_conv_b.reshape(DEPTH, 1, CONV_CH)
    sink8 = jnp.broadcast_to(attn_sink[:, :, None], (DEPTH, ATTN_HEADS, 128))
    cos, sin = _rope_tables()

    n1, nm, n2 = (w.reshape(DEPTH, 1, D_MODEL) for w in (ffn1_norm, mix_norm, ffn2_norm))
    sn = ssm_norm.reshape(DEPTH, 1, SSM_INNER)

    new_k, new_v, new_gla, new_ssm = [], [], [], []
    for l in range(DEPTH):
        mod3 = mod[l].reshape(MOD_ROWS, 1, N_MOD * D_MODEL)
        last = l == DEPTH - 1
        for st in (CTX, LAT):
            x = _ffn(st, xs[st], mod3, l, 0, n1, ffn1_w_gate, ffn1_w_up, ffn1_w_down)
            proj, aux = _inproj(st, x, mod3, l, nm, w_in_p)
            if st is CTX:
                o_gla, st_g = _gla(proj, aux, 0, BATCH, SEQ, wz[l], bz[l], gla_norm[l][None], None)
                o_ssm, st_s = _ssd(proj, aux, 0, BATCH, SEQ, conv_w[l], conv_b[l], dtb_rows[l], a_rows[l],
                                   d_skip[l], None)
                o_att = _attn_ctx(proj, sink8[l])
                new_k.append(aux[:, :ATTN_KV].reshape(BATCH, SEQ, KV_HEADS, HEAD_DIM))
                new_v.append(aux[:, ATTN_KV:2 * ATTN_KV].reshape(BATCH, SEQ, KV_HEADS, HEAD_DIM))
                new_gla.append(st_g)
                new_ssm.append(st_s)
            else:
                o_gla, _ = _gla(proj, aux, 0, DEC_BATCH, DEC_SEQ, wz[l], bz[l], gla_norm[l][None], state_gla[:, l])
                o_ssm, _ = _ssd(proj, aux, 0, DEC_BATCH, DEC_SEQ, conv_w[l], conv_b[l], dtb_rows[l], a_rows[l],
                                d_skip[l], state_ssm[:, l])
                o_att = _attn_lat(proj, cache_k, cache_v, l, cos, sin, sink8[l])
            x = _merge(st, x, mod3, l, proj, o_gla, o_ssm, o_att, sn, wbg, wbs, wba, wo)
            xs[st] = _ffn(st, x, mod3, l, 2, n2, ffn2_w_gate, ffn2_w_up, ffn2_w_down,
                          final_w=final_norm if last else None)

    y_prompt = xs[CTX].reshape(BATCH, SEQ, D_MODEL)
    y_sample = xs[LAT].reshape(DEC_BATCH, DEC_SEQ, D_MODEL)
    return (y_prompt, y_sample, jnp.stack(new_k, axis=1), jnp.stack(new_v, axis=1),
            jnp.stack(new_gla, axis=1), jnp.stack(new_ssm, axis=1))
```

```python
import functools
import math
from typing import NamedTuple

import numpy as np
import jax
import jax.numpy as jnp
from jax import lax
from jax.experimental import pallas as pl
from jax.experimental.pallas import tpu as pltpu

f32 = jnp.float32
bf16 = jnp.bfloat16

D_MODEL = 2048
BATCH = 32
SEQ = 256
DEPTH = 2
DEC_BATCH = 2
DEC_SEQ = 1024
PAST_LEN = 512
GRID_W = 64
RMS_EPS = 1e-6
N_MOD = 9
D_FF = 5632
GLA_HEADS = 4
GLA_DK = 128
GLA_DV = 256
GLA_RANK = 16
GLA_GATE_NORM = 16.0
GLA_CHUNK = 16
SSM_HEADS = 16
SSM_HEADDIM = 64
SSM_GROUPS = 4
SSM_HPG = SSM_HEADS // SSM_GROUPS
SSM_STATE = 128
SSM_CONV = 5
SSM_INNER = SSM_HEADS * SSM_HEADDIM
SSM_BC = SSM_GROUPS * SSM_STATE
CONV_CH = SSM_INNER + 2 * SSM_BC
ATTN_HEADS = 8
KV_HEADS = 2
Q_PER_KV = ATTN_HEADS // KV_HEADS
HEAD_DIM = 128
WINDOW = 128
ROPE_THETA = 10000.0
GLA_QK = GLA_HEADS * GLA_DK
GLA_VAL = GLA_HEADS * GLA_DV
ATTN_Q = ATTN_HEADS * HEAD_DIM
ATTN_KV = KV_HEADS * HEAD_DIM

N_PROMPT = BATCH * SEQ
N_SAMPLE = DEC_BATCH * DEC_SEQ
MOD_ROWS = 8

COL_GQ = 0
COL_GK = COL_GQ + GLA_QK
COL_GV = COL_GK + GLA_QK
COL_GR = COL_GV + GLA_VAL
COL_SZ = COL_GR + GLA_VAL
COL_SX = COL_SZ + SSM_INNER
COL_SB = COL_SX + SSM_INNER
COL_SC = COL_SB + SSM_BC
COL_AQ = COL_SC + SSM_BC
COL_AK = COL_AQ + ATTN_Q
COL_AV = COL_AK + ATTN_KV
COL_SMALL = COL_AV + ATTN_KV
SMALL_W = 128
SMALL_DT = 2 * GLA_RANK
COL_BR = 8192
D_INP = COL_BR + 3 * D_MODEL

TM = 256
TM_BIG = 1024
TF_BIG = 256
TN_IN = 1024
AUX_TILE = COL_AK // TN_IN
AUX_SMALL = (COL_SMALL - COL_AK) // SMALL_W
TN_MOD = 1024
TJ = 512
GLA_SB = 256
GLA_TILE = 128
GLA_HPS = 4
SSD_CHUNK = 128
SSD_GPS = 2
VMEM_LIMIT = 58 * 1024 * 1024

NT = (((1,), (1,)), ((), ()))


def _silu(x):
    return x * jax.nn.sigmoid(x)


class _Stream(NamedTuple):
    n_tok: int
    row0: int
    per_row: int

    def mod_row(self, i, tm):
        return self.row0 + (i * tm) // self.per_row


CTX = _Stream(N_PROMPT, 0, N_PROMPT)
LAT = _Stream(N_SAMPLE, 1, DEC_SEQ)


def _cparams(sem):
    return pltpu.CompilerParams(dimension_semantics=sem, vmem_limit_bytes=VMEM_LIMIT)


def _mod_kernel(c_ref, w_ref, b_ref, o_ref):
    s = _silu(c_ref[...]).astype(bf16)
    o_ref[0] = jnp.dot(s, w_ref[0].astype(bf16), preferred_element_type=f32) + b_ref[0]


def _modulation(cvec, w_mod, b_mod):
    n = N_MOD * D_MODEL
    return pl.pallas_call(
        _mod_kernel,
        grid=(DEPTH, n // TN_MOD),
        in_specs=[
            pl.BlockSpec((MOD_ROWS, D_MODEL), lambda l, j: (0, 0)),
            pl.BlockSpec((1, D_MODEL, TN_MOD), lambda l, j: (l, 0, j)),
            pl.BlockSpec((1, 1, TN_MOD), lambda l, j: (l, 0, j)),
        ],
        out_specs=pl.BlockSpec((1, MOD_ROWS, TN_MOD), lambda l, j: (l, 0, j)),
        out_shape=jax.ShapeDtypeStruct((DEPTH, MOD_ROWS, n), f32),
        compiler_params=_cparams(("arbitrary", "arbitrary")),
        name="modulation",
    )(cvec, w_mod, b_mod.reshape(DEPTH, 1, n))


def _norm_mod(x, nw, sh, sc):
    ms = jnp.mean(x * x, axis=-1, keepdims=True)
    h = x * lax.rsqrt(ms + RMS_EPS) * nw
    return h * (1.0 + sc) + sh


def _ffn_kernel(x_ref, sh_ref, sc_ref, gt_ref, nw_ref, wg_ref, wu_ref, wd_ref, *rest, final):
    if final:
        fw_ref, o_ref, h_ref = rest
    else:
        o_ref, h_ref = rest
    j = pl.program_id(1)

    def hidden_tile():
        h = h_ref[...]
        g = jnp.dot(h, wg_ref[...].astype(bf16), preferred_element_type=f32)
        u = jnp.dot(h, wu_ref[...].astype(bf16), preferred_element_type=f32)
        a = (_silu(g) * u).astype(bf16)
        return jnp.dot(a, wd_ref[...].astype(bf16), preferred_element_type=f32)

    @pl.when(j == 0)
    def _():
        h_ref[...] = _norm_mod(x_ref[...], nw_ref[...], sh_ref[0], sc_ref[0]).astype(bf16)
        o_ref[...] = hidden_tile()

    @pl.when(j > 0)
    def _():
        o_ref[...] += hidden_tile()

    @pl.when(j == pl.num_programs(1) - 1)
    def _():
        y = x_ref[...] + 0.5 * gt_ref[0] * o_ref[...]
        if final:
            ms = jnp.mean(y * y, axis=-1, keepdims=True)
            y = y * lax.rsqrt(ms + RMS_EPS) * fw_ref[...]
        o_ref[...] = y


def _ffn(st, x, mod3, l, slot, nw, wg, wu, wd, final_w=None):
    final = final_w is not None
    tm, tf = TM_BIG, TF_BIG

    def mod_spec(k):
        return pl.BlockSpec((1, 1, D_MODEL), lambda i, j: (st.mod_row(i, tm), 0, 3 * slot + k))

    in_specs = [
        pl.BlockSpec((tm, D_MODEL), lambda i, j: (i, 0)),
        mod_spec(0), mod_spec(1), mod_spec(2),
        pl.BlockSpec((None, 1, D_MODEL), lambda i, j: (l, 0, 0)),
        pl.BlockSpec((None, D_MODEL, tf), lambda i, j: (l, 0, j)),
        pl.BlockSpec((None, D_MODEL, tf), lambda i, j: (l, 0, j)),
        pl.BlockSpec((None, tf, D_MODEL), lambda i, j: (l, j, 0)),
    ]
    args = [x, mod3, mod3, mod3, nw, wg, wu, wd]
    if final:
        in_specs.append(pl.BlockSpec((1, D_MODEL), lambda i, j: (0, 0)))
        args.append(final_w.reshape(1, D_MODEL))
    return pl.pallas_call(
        functools.partial(_ffn_kernel, final=final),
        grid=(st.n_tok // tm, D_FF // tf),
        in_specs=in_specs,
        out_specs=pl.BlockSpec((tm, D_MODEL), lambda i, j: (i, 0)),
        out_shape=jax.ShapeDtypeStruct((st.n_tok, D_MODEL), f32),
        scratch_shapes=[pltpu.VMEM((tm, D_MODEL), bf16)],
        compiler_params=_cparams(("arbitrary", "arbitrary")),
        name="ffn_final" if final else "ffn",
    )(*args)


def _inproj_kernel(x_ref, sh_ref, sc_ref, nw_ref, w_ref, o_ref, os_ref, h_ref):
    j = pl.program_id(1)

    def tile():
        return jnp.dot(h_ref[...], w_ref[...], preferred_element_type=f32)

    @pl.when(j == 0)
    def _():
        h_ref[...] = _norm_mod(x_ref[...], nw_ref[...], sh_ref[0], sc_ref[0]).astype(bf16)
        o_ref[...] = tile().astype(bf16)

    @pl.when((j > 0) & (j < COL_BR // TN_IN) & (j != AUX_TILE))
    def _():
        o_ref[...] = tile().astype(bf16)

    @pl.when(j == AUX_TILE)
    def _():
        t = tile()
        os_ref[...] = t
        o_ref[...] = t.astype(bf16)

    @pl.when(j >= COL_BR // TN_IN)
    def _():
        o_ref[...] = jax.nn.sigmoid(tile()).astype(bf16)


def _inproj(st, x, mod3, l, nw, w):
    tm = TM_BIG

    def mod_spec(k):
        return pl.BlockSpec((1, 1, D_MODEL), lambda i, j: (st.mod_row(i, tm), 0, 3 + k))

    return pl.pallas_call(
        _inproj_kernel,
        grid=(st.n_tok // tm, D_INP // TN_IN),
        in_specs=[
            pl.BlockSpec((tm, D_MODEL), lambda i, j: (i, 0)),
            mod_spec(0), mod_spec(1),
            pl.BlockSpec((None, 1, D_MODEL), lambda i, j: (l, 0, 0)),
            pl.BlockSpec((None, D_MODEL, TN_IN), lambda i, j: (l, 0, j)),
        ],
        out_specs=[pl.BlockSpec((tm, TN_IN), lambda i, j: (i, j)),
                   pl.BlockSpec((tm, TN_IN), lambda i, j: (i, 0))],
        out_shape=[jax.ShapeDtypeStruct((st.n_tok, D_INP), bf16), jax.ShapeDtypeStruct((st.n_tok, TN_IN), f32)],
        scratch_shapes=[pltpu.VMEM((tm, D_MODEL), bf16)],
        compiler_params=_cparams(("arbitrary", "arbitrary")),
        name="inproj",
    )(x, mod3, mod3, nw, w)


def _split_dot(m, x):
    n = x.shape[1]
    hi = x.astype(bf16)
    lo = (x - hi.astype(f32)).astype(bf16)
    r = jnp.dot(m, jnp.concatenate([hi, lo], axis=1), preferred_element_type=f32)
    return r[:, :n] + r[:, n:]


def _log_sigmoid(x):
    return jnp.minimum(x, 0.0) - jnp.log(1.0 + jnp.exp(-jnp.abs(x)))


def _gla_kernel(q_ref, k_ref, v_ref, r_ref, sm_ref, wz_ref, bz_ref, gn_ref, *rest, seq, has_h0, emit_state):
    rest = list(rest)
    h0_ref = rest.pop(0) if has_h0 else None
    o_ref = rest.pop(0)
    st_ref = rest.pop(0) if emit_state else None
    oin_ref, qe_ref, ke_ref, dec_ref, vt_ref = rest
    hps = GLA_HPS
    sb = GLA_SB
    tile = GLA_TILE
    per_tile = tile // GLA_CHUNK
    n_tile = seq // tile

    row = lax.broadcasted_iota(jnp.int32, (sb, sb), 0)
    col = lax.broadcasted_iota(jnp.int32, (sb, sb), 1)
    same = (row // GLA_CHUNK) == (col // GLA_CHUNK)
    masks = (same & (col <= row), same & (col >= row))
    blk = jnp.where(same, 1.0, 0.0).astype(bf16)
    tri_blk = tuple(jnp.concatenate([jnp.where(m, 1.0, 0.0).astype(bf16), blk], axis=0) for m in masks)

    chains = [(hh, d) for hh in range(hps) for d in range(2)]
    kcols = [slice(hh * GLA_DK, (hh + 1) * GLA_DK) for hh in range(hps)]
    vcols = [slice(hh * GLA_DV, (hh + 1) * GLA_DV) for hh in range(hps)]

    for s in range(seq // sb):
        rows = slice(s * sb, (s + 1) * sb)
        sm = sm_ref[rows, :].astype(bf16)
        q = [q_ref[rows, kcols[hh]].astype(f32) * (GLA_DK ** -0.5) for hh in range(hps)]
        k = [k_ref[rows, kcols[hh]].astype(f32) for hh in range(hps)]
        v = [v_ref[rows, vcols[hh]] for hh in range(hps)]
        for hh in range(hps):
            v_t = v[hh].astype(f32).T.astype(bf16)
            for t in range(sb // tile):
                vt_ref[hh, s * (sb // tile) + t] = v_t[:, t * tile:(t + 1) * tile]
        gz = [jnp.dot(sm, wz_ref[d, :, kcols[hh]], preferred_element_type=f32) + bz_ref[d, :, kcols[hh]]
              for hh, d in chains]
        la = [_log_sigmoid(g) * (1.0 / GLA_GATE_NORM) for g in gz]
        sums = [_split_dot(tri_blk[d], la[i]) for i, (hh, d) in enumerate(chains)]
        cum = [x[:sb] for x in sums]
        tot = [x[sb:] for x in sums]
        qe = [q[hh] * jnp.exp(cum[i]) for i, (hh, d) in enumerate(chains)]
        kinv = [k[hh] * jnp.exp(-cum[i]) for i, (hh, d) in enumerate(chains)]
        kend = [k[hh] * jnp.exp(tot[i] - cum[i]) for i, (hh, d) in enumerate(chains)]
        a = [lax.dot_general(qe[i].astype(bf16), kinv[i].astype(bf16), NT, preferred_element_type=f32)
             for i in range(len(chains))]
        a = [jnp.where(masks[d], a[i], 0.0).astype(bf16) for i, (hh, d) in enumerate(chains)]
        for i, (hh, d) in enumerate(chains):
            oin_ref[hh, d, rows, :] = jnp.dot(a[i], v[hh], preferred_element_type=f32)
            qe_ref[hh, d, rows, :] = qe[i].astype(bf16)
            ke_ref[hh, d, rows, :] = kend[i].astype(bf16)
            dec_ref[hh, d, rows, :] = jnp.exp(tot[i])

    if has_h0:
        s_init = tuple(h0_ref[d, hh].T for hh, d in chains)
    else:
        s_init = (jnp.zeros((GLA_DV, GLA_DK), f32),) * len(chains)

    def scan_tile(o, carry):
        sts = list(carry)
        for u in range(per_tile):
            for ci, (hh, d) in enumerate(chains):
                t_idx = o if d == 0 else n_tile - 1 - o
                pos = u if d == 0 else per_tile - 1 - u
                i0 = t_idx * tile + pos * GLA_CHUNK
                if not isinstance(i0, int):
                    i0 = pl.multiple_of(i0, GLA_CHUNK)
                rows = pl.ds(i0, GLA_CHUNK)
                inter = lax.dot_general(qe_ref[hh, d, rows, :], sts[ci].astype(bf16), NT,
                                        preferred_element_type=f32)
                oin_ref[hh, d, rows, :] += inter
                pieces = [ke_ref[hh, d, rows, :]]
                if pos:
                    pieces.insert(0, jnp.zeros((pos * GLA_CHUNK, GLA_DK), bf16))
                if pos < per_tile - 1:
                    pieces.append(jnp.zeros(((per_tile - 1 - pos) * GLA_CHUNK, GLA_DK), bf16))
                upd = jnp.dot(vt_ref[hh, t_idx], jnp.concatenate(pieces, axis=0), preferred_element_type=f32)
                sts[ci] = sts[ci] * dec_ref[hh, d, pl.ds(i0, 1), :] + upd
        return tuple(sts)

    if n_tile <= 2:
        s_fin = s_init
        for o in range(n_tile):
            s_fin = scan_tile(o, s_fin)
    else:
        s_fin = lax.fori_loop(0, n_tile, scan_tile, s_init)

    for hh in range(hps):
        vcols = slice(hh * GLA_DV, (hh + 1) * GLA_DV)
        o = oin_ref[hh, 0] + oin_ref[hh, 1]
        ms = jnp.mean(o * o, axis=-1, keepdims=True)
        o = o * lax.rsqrt(ms + RMS_EPS) * gn_ref[...]
        o_ref[:, vcols] = (o * _silu(r_ref[:, vcols].astype(f32))).astype(o_ref.dtype)
    if emit_state:
        for ci, (hh, d) in enumerate(chains):
            st_ref[0, d, hh] = s_fin[ci].T


def _gla(proj, small, row_blk0, n_seq, seq, wz, bz, gn, h0):
    has_h0 = h0 is not None
    emit_state = not has_h0
    kq, kv_, hps = GLA_DK, GLA_DV, GLA_HPS
    kw, vw = hps * kq, hps * kv_
    in_specs = [
        pl.BlockSpec((seq, kw), lambda b, h: (row_blk0 + b, COL_GQ // kw + h)),
        pl.BlockSpec((seq, kw), lambda b, h: (row_blk0 + b, COL_GK // kw + h)),
        pl.BlockSpec((seq, vw), lambda b, h: (row_blk0 + b, COL_GV // vw + h)),
        pl.BlockSpec((seq, vw), lambda b, h: (row_blk0 + b, COL_GR // vw + h)),
        pl.BlockSpec((seq, SMALL_W), lambda b, h: (row_blk0 + b, AUX_SMALL)),
        pl.BlockSpec((2, SMALL_W, kw), lambda b, h: (0, 0, h)),
        pl.BlockSpec((2, 1, kw), lambda b, h: (0, 0, h)),
        pl.BlockSpec((1, kv_), lambda b, h: (0, 0)),
    ]
    args = [proj, proj, proj, proj, small, wz, bz, gn]
    if has_h0:
        in_specs.append(pl.BlockSpec((None, 2, hps, kq, kv_), lambda b, h: (b, 0, h, 0, 0)))
        args.append(h0)
    out_specs = [pl.BlockSpec((seq, vw), lambda b, h: (b, h))]
    out_shape = [jax.ShapeDtypeStruct((n_seq * seq, GLA_VAL), bf16)]
    if emit_state:
        out_specs.append(pl.BlockSpec((1, 2, hps, kq, kv_), lambda b, h: (b, 0, h, 0, 0)))
        out_shape.append(jax.ShapeDtypeStruct((n_seq, 2, GLA_HEADS, kq, kv_), f32))
    res = pl.pallas_call(
        functools.partial(_gla_kernel, seq=seq, has_h0=has_h0, emit_state=emit_state),
        grid=(n_seq, GLA_HEADS // hps),
        in_specs=in_specs,
        out_specs=out_specs,
        out_shape=out_shape,
        scratch_shapes=[
            pltpu.VMEM((hps, 2, seq, kv_), f32),
            pltpu.VMEM((hps, 2, seq, kq), bf16),
            pltpu.VMEM((hps, 2, seq, kq), bf16),
            pltpu.VMEM((hps, 2, seq, kq), f32),
            pltpu.VMEM((hps, seq // GLA_TILE, kv_, GLA_TILE), bf16),
        ],
        compiler_params=_cparams(("parallel", "arbitrary")),
        name="gla_lat" if has_h0 else "gla_ctx",
    )(*args)
    return res if emit_state else (res[0], None)


def _softplus(x):
    return jnp.maximum(x, 0.0) + jnp.log(1.0 + jnp.exp(-jnp.abs(x)))


def _split3(x):
    hi = x.astype(bf16)
    r1 = x - hi.astype(f32)
    mid = r1.astype(bf16)
    lo = (r1 - mid.astype(f32)).astype(bf16)
    return [hi, mid, lo]


def _split3_dot(m, x):
    n = x.shape[1]
    r = jnp.dot(m, jnp.concatenate(_split3(x), axis=1), preferred_element_type=f32)
    return r[:, :n] + r[:, n:2 * n] + r[:, 2 * n:]


def _dot_sel(x, sel):
    return sum(jnp.dot(t, sel, preferred_element_type=f32) for t in _split3(x))


def _shift_rows(x, d, t_idx):
    n = x.shape[0]
    if d == 0:
        return x
    y = pltpu.roll(x, (-d) % n, 0)
    ok = (t_idx + d >= 0) & (t_idx + d < n)
    return jnp.where(ok, y, 0.0)


def _ssd_kernel(z_ref, x_ref, b_ref, c_ref, sm_ref, cwx_ref, cwb_ref, cwc_ref, cbx_ref, cbb_ref, cbc_ref,
                dtb_ref, a_ref, dsk_ref, *rest, seq, has_h0, emit_state):
    rest = list(rest)
    h0_ref = rest.pop(0) if has_h0 else None
    o_ref = rest.pop(0)
    st_ref = rest.pop(0) if emit_state else None
    y_ref, xs_ref, bm_ref, cm_ref, dt_ref = rest
    gps = SSD_GPS
    cs = SSD_CHUNK
    n_chunk = seq // cs
    hp = SSM_HPG * SSM_HEADDIM

    t_idx = lax.broadcasted_iota(jnp.int32, (seq, 1), 0)

    def conv_silu(src_ref, w_ref, bias_ref):
        xin = src_ref[...].astype(f32)
        acc = jnp.zeros_like(xin) + bias_ref[...]
        for j in range(SSM_CONV):
            acc = acc + w_ref[j:j + 1, :] * _shift_rows(xin, j - SSM_CONV // 2, t_idx)
        return _silu(acc)

    xs_ref[...] = conv_silu(x_ref, cwx_ref, cbx_ref)
    bm_ref[...] = conv_silu(b_ref, cwb_ref, cbb_ref)
    cm_ref[...] = conv_silu(c_ref, cwc_ref, cbc_ref)

    row = lax.broadcasted_iota(jnp.int32, (cs, cs), 0)
    col = lax.broadcasted_iota(jnp.int32, (cs, cs), 1)
    masks = (col <= row, col >= row)
    tris = tuple(jnp.where(m, 1.0, 0.0).astype(bf16) for m in masks)
    lane = lax.broadcasted_iota(jnp.int32, (1, gps * SMALL_W), 1) % SMALL_W

    dt_lane = (lane >= SMALL_DT) & (lane < SMALL_DT + 2 * SSM_HPG)
    a_row = a_ref[...]

    def selectors(d):
        r = lax.broadcasted_iota(jnp.int32, (SMALL_W, SSM_HPG * cs), 0)
        c = lax.broadcasted_iota(jnp.int32, (SMALL_W, SSM_HPG * cs), 1)
        bcast = jnp.where(r == SMALL_DT + d * SSM_HPG + c // cs, 1.0, 0.0).astype(bf16)
        r = lax.broadcasted_iota(jnp.int32, (SMALL_W, hp), 0)
        c = lax.broadcasted_iota(jnp.int32, (SMALL_W, hp), 1)
        expand = jnp.where(r == SMALL_DT + d * SSM_HPG + c // SSM_HEADDIM, 1.0, 0.0).astype(bf16)
        return jnp.concatenate([bcast, expand], axis=1)

    head_of_lane = lax.broadcasted_iota(jnp.int32, (cs, hp), 1) // SSM_HEADDIM
    sl = [slice(g * SMALL_W, (g + 1) * SMALL_W) for g in range(gps)]
    hl = [slice(g * hp, (g + 1) * hp) for g in range(gps)]
    bm_t = [[bm_ref[c * cs:(c + 1) * cs, sl[g]].T.astype(bf16) for c in range(n_chunk)]
            for g in range(gps)]

    dt_ref[...] = jnp.where(dt_lane, _softplus(sm_ref[...] + dtb_ref[...]), 0.0)
    y_ref[...] = jnp.zeros_like(y_ref)
    sel = [selectors(0), selectors(1)]
    chains = [(g, d) for g in range(gps) for d in range(2)]
    n = range(len(chains))
    if has_h0:
        st = [h0_ref[d, g * SSM_HPG:(g + 1) * SSM_HPG].reshape(hp, SSM_STATE).T for g, d in chains]
    else:
        st = [jnp.zeros((SSM_STATE, hp), f32)] * len(chains)
    for t in range(n_chunk):
        cidx = [t if d == 0 else n_chunk - 1 - t for g, d in chains]
        rows = [slice(c * cs, (c + 1) * cs) for c in cidx]
        dt = [dt_ref[rows[i], sl[g]] for i, (g, d) in enumerate(chains)]
        cum = [_split3_dot(tris[d], dt[i] * a_row[:, sl[g]]) for i, (g, d) in enumerate(chains)]
        cum_t = [cum[i].T for i in n]
        cum_s = [_dot_sel(cum[i], sel[d]) for i, (g, d) in enumerate(chains)]
        cum_b = [x[:, :SSM_HPG * cs] for x in cum_s]
        cum_e = [x[:, SSM_HPG * cs:] for x in cum_s]
        dt_e = [_dot_sel(dt[i], sel[d][:, SSM_HPG * cs:]) for i, (g, d) in enumerate(chains)]
        tot_e = [cum_e[i][cs - 1:cs, :] if d == 0 else cum_e[i][0:1, :] for i, (g, d) in enumerate(chains)]
        xs = [xs_ref[rows[i], hl[g]] for i, (g, d) in enumerate(chains)]
        cm = [cm_ref[rows[i], sl[g]].astype(bf16) for i, (g, d) in enumerate(chains)]
        cb = [lax.dot_general(cm[i], bm_ref[rows[i], sl[g]].astype(bf16), NT, preferred_element_type=f32)
              for i, (g, d) in enumerate(chains)]
        seg = [[jnp.exp(jnp.where(masks[d], cum_b[i][:, j * cs:(j + 1) * cs]
                                  - cum_t[i][SMALL_DT + d * SSM_HPG + j:SMALL_DT + d * SSM_HPG + j + 1, :],
                                  -jnp.inf)) for j in range(SSM_HPG)] for i, (g, d) in enumerate(chains)]
        w = [jnp.concatenate([(cb[i] * seg[i][j]).astype(bf16) for j in range(SSM_HPG)], axis=1)
             for i in n]
        xd = [xs[i] * dt_e[i] for i in n]
        xd_bd = [jnp.concatenate([jnp.where(head_of_lane == j, xd[i], 0.0).astype(bf16)
                                  for j in range(SSM_HPG)], axis=0) for i in n]
        inter = [jnp.dot(cm[i], st[i].astype(bf16), preferred_element_type=f32) for i in n]
        y_new = [jnp.dot(w[i], xd_bd[i], preferred_element_type=f32) + jnp.exp(cum_e[i]) * inter[i] for i in n]
        xw = [(xd[i] * jnp.exp(tot_e[i] - cum_e[i])).astype(bf16) for i in n]
        st = [st[i] * jnp.exp(tot_e[i]) + jnp.dot(bm_t[g][cidx[i]], xw[i], preferred_element_type=f32)
              for i, (g, d) in enumerate(chains)]
        for i, (g, d) in enumerate(chains):
            y_ref[rows[i], hl[g]] += y_new[i]

    y = y_ref[...] + dsk_ref[...] * xs_ref[...]
    o_ref[...] = y * _silu(z_ref[...].astype(f32))
    if emit_state:
        for i, (g, d) in enumerate(chains):
            st_ref[0, d, g * SSM_HPG:(g + 1) * SSM_HPG] = st[i].T.reshape(SSM_HPG, SSM_HEADDIM, SSM_STATE)


def _ssd(proj, small, row_blk0, n_seq, seq, conv_w, conv_b, dt_bias_row, a_row, d_skip, h0):
    has_h0 = h0 is not None
    emit_state = not has_h0
    gps = SSD_GPS
    hw = gps * SSM_HPG * SSM_HEADDIM
    nw = gps * SSM_STATE
    hg = gps * SSM_HPG
    in_specs = [
        pl.BlockSpec((seq, hw), lambda b, g: (row_blk0 + b, COL_SZ // hw + g)),
        pl.BlockSpec((seq, hw), lambda b, g: (row_blk0 + b, COL_SX // hw + g)),
        pl.BlockSpec((seq, nw), lambda b, g: (row_blk0 + b, COL_SB // nw + g)),
        pl.BlockSpec((seq, nw), lambda b, g: (row_blk0 + b, COL_SC // nw + g)),
        pl.BlockSpec((seq, nw), lambda b, g: (row_blk0 + b, AUX_SMALL // gps + g)),
        pl.BlockSpec((8, hw), lambda b, g: (0, g)),
        pl.BlockSpec((8, nw), lambda b, g: (0, SSM_INNER // nw + g)),
        pl.BlockSpec((8, nw), lambda b, g: (0, (SSM_INNER + SSM_BC) // nw + g)),
        pl.BlockSpec((1, hw), lambda b, g: (0, g)),
        pl.BlockSpec((1, nw), lambda b, g: (0, SSM_INNER // nw + g)),
        pl.BlockSpec((1, nw), lambda b, g: (0, (SSM_INNER + SSM_BC) // nw + g)),
        pl.BlockSpec((1, nw), lambda b, g: (0, g)),
        pl.BlockSpec((1, nw), lambda b, g: (0, g)),
        pl.BlockSpec((1, hw), lambda b, g: (0, g)),
    ]
    args = [proj, proj, proj, proj, small, conv_w, conv_w, conv_w, conv_b, conv_b, conv_b,
            dt_bias_row, a_row, d_skip]
    if has_h0:
        in_specs.append(pl.BlockSpec((None, 2, hg, SSM_HEADDIM, SSM_STATE), lambda b, g: (b, 0, g, 0, 0)))
        args.append(h0)
    out_specs = [pl.BlockSpec((seq, hw), lambda b, g: (b, g))]
    out_shape = [jax.ShapeDtypeStruct((n_seq * seq, SSM_INNER), f32)]
    if emit_state:
        out_specs.append(pl.BlockSpec((1, 2, hg, SSM_HEADDIM, SSM_STATE), lambda b, g: (b, 0, g, 0, 0)))
        out_shape.append(jax.ShapeDtypeStruct((n_seq, 2, SSM_HEADS, SSM_HEADDIM, SSM_STATE), f32))
    res = pl.pallas_call(
        functools.partial(_ssd_kernel, seq=seq, has_h0=has_h0, emit_state=emit_state),
        grid=(n_seq, SSM_GROUPS // gps),
        in_specs=in_specs,
        out_specs=out_specs,
        out_shape=out_shape,
        scratch_shapes=[
            pltpu.VMEM((seq, hw), f32),
            pltpu.VMEM((seq, hw), f32),
            pltpu.VMEM((seq, nw), f32),
            pltpu.VMEM((seq, nw), f32),
            pltpu.VMEM((seq, nw), f32),
        ],
        compiler_params=_cparams(("parallel", "arbitrary")),
        name="ssd_lat" if has_h0 else "ssd_ctx",
    )(*args)
    return res if emit_state else (res[0], None)


def _attn_ctx_kernel(q_ref, k_ref, v_ref, sink_ref, o_ref):
    heads = range(ATTN_HEADS)
    kv = [slice(h // Q_PER_KV * HEAD_DIM, (h // Q_PER_KV + 1) * HEAD_DIM) for h in heads]
    cols = [slice(h * HEAD_DIM, (h + 1) * HEAD_DIM) for h in heads]
    s = [lax.dot_general(q_ref[:, cols[h]], k_ref[:, kv[h]], NT, preferred_element_type=f32) * (HEAD_DIM ** -0.5)
         for h in heads]
    sink = [sink_ref[h:h + 1, 0:1] for h in heads]
    m = [jnp.maximum(jnp.max(s[h], axis=-1, keepdims=True), sink[h]) for h in heads]
    p = [jnp.exp(s[h] - m[h]) for h in heads]
    den = [jnp.sum(p[h], axis=-1, keepdims=True) + jnp.exp(sink[h] - m[h]) for h in heads]
    for h in heads:
        o = jnp.dot(p[h].astype(bf16), v_ref[:, kv[h]], preferred_element_type=f32) / den[h]
        o_ref[:, cols[h]] = o.astype(o_ref.dtype)


def _attn_ctx(proj, sink8, n_seq=BATCH):
    return pl.pallas_call(
        _attn_ctx_kernel,
        grid=(n_seq,),
        in_specs=[
            pl.BlockSpec((SEQ, ATTN_Q), lambda b: (b, COL_AQ // ATTN_Q)),
            pl.BlockSpec((SEQ, ATTN_KV), lambda b: (b, COL_AK // ATTN_KV)),
            pl.BlockSpec((SEQ, ATTN_KV), lambda b: (b, COL_AV // ATTN_KV)),
            pl.BlockSpec((ATTN_HEADS, 128), lambda b: (0, 0)),
        ],
        out_specs=pl.BlockSpec((SEQ, ATTN_Q), lambda b: (b, 0)),
        out_shape=jax.ShapeDtypeStruct((n_seq * SEQ, ATTN_Q), bf16),
        compiler_params=_cparams(("arbitrary",)),
        name="attn_ctx",
    )(proj, proj, proj, sink8)


def _rope(x, cos, sin_signed):
    quarter = HEAD_DIM // 4
    lane = lax.broadcasted_iota(jnp.int32, x.shape, 1)
    first = (lane % (2 * quarter)) < quarter
    partner = jnp.where(first, pltpu.roll(x, HEAD_DIM - quarter, 1), pltpu.roll(x, quarter, 1))
    return x * cos + partner * sin_signed


def _attn_lat_kernel(q_ref, k_ref, v_ref, kc_ref, vc_ref, cos_ref, sin_ref, sink_ref, o_ref, kr_ref):
    kvh = pl.program_id(1)
    cos = cos_ref[...]
    sin = sin_ref[...]
    kr_ref[...] = _rope(k_ref[...].astype(f32), cos, sin).astype(bf16)
    kc = kc_ref[...].astype(bf16)
    vc = vc_ref[...].astype(bf16)
    blk = WINDOW
    n_blk = DEC_SEQ // blk
    scale = HEAD_DIM ** -0.5
    for i in range(n_blk):
        lo = max(i - 1, 0) * blk
        hi = min(i + 2, n_blk) * blk
        kw = kr_ref[lo:hi, :]
        vw = v_ref[lo:hi, :].astype(bf16)
        qpos = i * blk + lax.broadcasted_iota(jnp.int32, (blk, hi - lo), 0)
        kpos = lo + lax.broadcasted_iota(jnp.int32, (blk, hi - lo), 1)
        win = jnp.abs(qpos - kpos) <= WINDOW
        rows = slice(i * blk, (i + 1) * blk)
        grp = range(Q_PER_KV)
        cols = [slice(g * HEAD_DIM, (g + 1) * HEAD_DIM) for g in grp]
        q = [_rope(q_ref[rows, cols[g]].astype(f32), cos[rows, :], sin[rows, :]).astype(bf16) for g in grp]
        s_c = [lax.dot_general(q[g], kc, NT, preferred_element_type=f32) * scale for g in grp]
        s_w = [jnp.where(win, lax.dot_general(q[g], kw, NT, preferred_element_type=f32) * scale, -jnp.inf)
               for g in grp]
        sink = [sink_ref[pl.ds(kvh * Q_PER_KV + g, 1), 0:1] for g in grp]
        m = [jnp.maximum(jnp.maximum(jnp.max(s_c[g], axis=-1, keepdims=True),
                                     jnp.max(s_w[g], axis=-1, keepdims=True)), sink[g]) for g in grp]
        p_c = [jnp.exp(s_c[g] - m[g]) for g in grp]
        p_w = [jnp.exp(s_w[g] - m[g]) for g in grp]
        den = [jnp.sum(p_c[g], axis=-1, keepdims=True) + jnp.sum(p_w[g], axis=-1, keepdims=True)
               + jnp.exp(sink[g] - m[g]) for g in grp]
        for g in grp:
            o = (jnp.dot(p_c[g].astype(bf16), vc, preferred_element_type=f32)
                 + jnp.dot(p_w[g].astype(bf16), vw, preferred_element_type=f32)) / den[g]
            o_ref[rows, cols[g]] = o.astype(o_ref.dtype)


def _attn_lat(proj, cache_k, cache_v, l, cos, sin, sink8, rb0=0, n_seq=DEC_BATCH):
    qw = Q_PER_KV * HEAD_DIM
    ck = cache_k.reshape(n_seq, DEPTH, PAST_LEN, ATTN_KV)
    cv = cache_v.reshape(n_seq, DEPTH, PAST_LEN, ATTN_KV)
    return pl.pallas_call(
        _attn_lat_kernel,
        grid=(n_seq, KV_HEADS),
        in_specs=[
            pl.BlockSpec((DEC_SEQ, qw), lambda b, h: (rb0 + b, COL_AQ // qw + h)),
            pl.BlockSpec((DEC_SEQ, HEAD_DIM), lambda b, h: (rb0 + b, COL_AK // HEAD_DIM + h)),
            pl.BlockSpec((DEC_SEQ, HEAD_DIM), lambda b, h: (rb0 + b, COL_AV // HEAD_DIM + h)),
            pl.BlockSpec((None, None, PAST_LEN, HEAD_DIM), lambda b, h: (b, l, 0, h)),
            pl.BlockSpec((None, None, PAST_LEN, HEAD_DIM), lambda b, h: (b, l, 0, h)),
            pl.BlockSpec((DEC_SEQ, HEAD_DIM), lambda b, h: (0, 0)),
            pl.BlockSpec((DEC_SEQ, HEAD_DIM), lambda b, h: (0, 0)),
            pl.BlockSpec((ATTN_HEADS, 128), lambda b, h: (0, 0)),
        ],
        out_specs=pl.BlockSpec((DEC_SEQ, qw), lambda b, h: (b, h)),
        out_shape=jax.ShapeDtypeStruct((n_seq * DEC_SEQ, ATTN_Q), bf16),
        scratch_shapes=[pltpu.VMEM((DEC_SEQ, HEAD_DIM), bf16)],
        compiler_params=_cparams(("parallel", "arbitrary")),
        name="attn_lat",
    )(proj, proj, proj, ck, cv, cos, sin, sink8)


def _rope_tables():
    quarter = HEAD_DIM // 4
    freqs = ROPE_THETA ** (-np.arange(quarter, dtype=np.float32) / quarter)
    t = np.arange(DEC_SEQ)
    cos = np.zeros((DEC_SEQ, HEAD_DIM), np.float32)
    sin = np.zeros((DEC_SEQ, HEAD_DIM), np.float32)
    for half, pos in enumerate((t // GRID_W, t % GRID_W)):
        ang = pos.astype(np.float32)[:, None] * freqs[None, :]
        base = half * 2 * quarter
        cos[:, base:base + quarter] = np.cos(ang)
        cos[:, base + quarter:base + 2 * quarter] = np.cos(ang)
        sin[:, base:base + quarter] = -np.sin(ang)
        sin[:, base + quarter:base + 2 * quarter] = np.sin(ang)
    return jnp.asarray(cos), jnp.asarray(sin)


def _merge_kernel(x_ref, gt_ref, og_ref, os_ref, oa_ref, b0_ref, b1_ref, b2_ref, sn_ref,
                  wg_ref, ws_ref, wa_ref, wo_ref, o_ref):
    y = os_ref[...]
    ms = jnp.mean(y * y, axis=-1, keepdims=True)
    osn = (y * lax.rsqrt(ms + RMS_EPS) * sn_ref[...]).astype(bf16)
    og = og_ref[...]
    oa = oa_ref[...]
    for jt in range(D_MODEL // TJ):
        cols = slice(jt * TJ, (jt + 1) * TJ)
        m = (b0_ref[:, cols].astype(f32) * jnp.dot(og, wg_ref[jt], preferred_element_type=f32)
             + b1_ref[:, cols].astype(f32) * jnp.dot(osn, ws_ref[jt], preferred_element_type=f32)
             + b2_ref[:, cols].astype(f32) * jnp.dot(oa, wa_ref[jt], preferred_element_type=f32))
        part = jnp.dot(m.astype(bf16), wo_ref[jt], preferred_element_type=f32)
        if jt == 0:
            o_ref[...] = part
        else:
            o_ref[...] += part
    o_ref[...] = x_ref[...] + gt_ref[0] * o_ref[...]


def _merge(st, x, mod3, l, proj, o_gla, o_ssm, o_att, ssm_norm, w_g, w_s, w_a, w_o):
    nj = D_MODEL // TJ
    tm = TM

    def br_spec(k):
        return pl.BlockSpec((tm, D_MODEL), lambda i: (i, COL_BR // D_MODEL + k))

    def wbr_spec():
        return pl.BlockSpec((None, nj, GLA_VAL, TJ), lambda i: (l, 0, 0, 0), pipeline_mode=pl.Buffered(1))

    return pl.pallas_call(
        _merge_kernel,
        grid=(st.n_tok // tm,),
        in_specs=[
            pl.BlockSpec((tm, D_MODEL), lambda i: (i, 0)),
            pl.BlockSpec((1, 1, D_MODEL), lambda i: (st.mod_row(i, tm), 0, 5)),
            pl.BlockSpec((tm, GLA_VAL), lambda i: (i, 0)),
            pl.BlockSpec((tm, SSM_INNER), lambda i: (i, 0)),
            pl.BlockSpec((tm, ATTN_Q), lambda i: (i, 0)),
            br_spec(0), br_spec(1), br_spec(2),
            pl.BlockSpec((None, 1, SSM_INNER), lambda i: (l, 0, 0)),
            wbr_spec(), wbr_spec(), wbr_spec(),
            pl.BlockSpec((None, nj, TJ, D_MODEL), lambda i: (l, 0, 0, 0), pipeline_mode=pl.Buffered(1)),
        ],
        out_specs=pl.BlockSpec((tm, D_MODEL), lambda i: (i, 0)),
        out_shape=jax.ShapeDtypeStruct((st.n_tok, D_MODEL), f32),
        compiler_params=_cparams(("arbitrary",)),
        name="merge",
    )(x, mod3, o_gla, o_ssm, o_att, proj, proj, proj, ssm_norm, w_g, w_s, w_a, w_o)


_IN_SRC = np.cumsum((0, GLA_QK, GLA_QK, GLA_VAL, GLA_VAL, 2 * GLA_RANK, SSM_INNER, CONV_CH, 2 * SSM_HEADS,
                     ATTN_Q, ATTN_KV, ATTN_KV, 3 * D_MODEL))
SRC_GDOWN, SRC_SZ, SRC_DT, SRC_AQ, SRC_BR, SRC_END = (int(_IN_SRC[k]) for k in (4, 5, 7, 8, 11, 12))
W_PREP_K = 256
W_PREP_CHUNK = 512


def _w_in_prep_kernel(wt_ref, o_ref):
    def move(src, dst, width):
        for c0 in range(0, width, W_PREP_CHUNK):
            w = min(W_PREP_CHUNK, width - c0)
            o_ref[0, :, dst + c0:dst + c0 + w] = wt_ref[0, src + c0:src + c0 + w, :].T.astype(bf16)

    move(0, COL_GQ, SRC_GDOWN)
    move(SRC_SZ, COL_SZ, SRC_DT - SRC_SZ)
    move(SRC_AQ, COL_AQ, SRC_BR - SRC_AQ)
    move(SRC_BR, COL_BR, SRC_END - SRC_BR)

    n_gd, n_dt = 2 * GLA_RANK, 2 * SSM_HEADS
    t_gd = wt_ref[0, SRC_GDOWN:SRC_GDOWN + n_gd, :].astype(bf16)
    t_dt = wt_ref[0, SRC_DT:SRC_DT + n_dt, :].astype(bf16)
    r = lax.broadcasted_iota(jnp.int32, (SMALL_W, n_gd), 0)
    c = lax.broadcasted_iota(jnp.int32, (SMALL_W, n_gd), 1)
    sel_gd = jnp.where(r == c, 1.0, 0.0).astype(bf16)
    k = r - SMALL_DT
    for g in range(SSM_GROUPS):
        src_row = (k // SSM_HPG) * SSM_HEADS + g * SSM_HPG + k % SSM_HPG
        sel_dt = jnp.where((k >= 0) & (k < 2 * SSM_HPG) & (c == src_row), 1.0, 0.0).astype(bf16)
        blk = (jnp.dot(sel_gd, t_gd, preferred_element_type=f32)
               + jnp.dot(sel_dt, t_dt, preferred_element_type=f32))
        o_ref[0, :, COL_SMALL + g * SMALL_W:COL_SMALL + (g + 1) * SMALL_W] = blk.T.astype(bf16)


def _w_in_prep(w_in):
    wt = jnp.swapaxes(w_in, 1, 2)
    return pl.pallas_call(
        _w_in_prep_kernel,
        grid=(DEPTH, D_MODEL // W_PREP_K),
        in_specs=[pl.BlockSpec((1, SRC_END, W_PREP_K), lambda l, i: (l, 0, i))],
        out_specs=pl.BlockSpec((1, W_PREP_K, D_INP), lambda l, i: (l, i, 0)),
        out_shape=jax.ShapeDtypeStruct((DEPTH, D_MODEL, D_INP), bf16),
        compiler_params=_cparams(("arbitrary", "arbitrary")),
        name="w_in_prep",
    )(wt)


def kernel(x_prompt, x_sample, c, cache_k, cache_v, state_gla, state_ssm, c_ctx, w_mod, b_mod, ffn1_norm,
           ffn1_w_gate, ffn1_w_up, ffn1_w_down, mix_norm, w_in, gla_w_up, gla_b_up, gla_norm, ssm_conv_w,
           ssm_conv_b, ssm_dt_bias, ssm_a_log, ssm_d, ssm_norm, attn_sink, w_br_gla, w_br_ssm, w_br_attn,
           w_out, ffn2_norm, ffn2_w_gate, ffn2_w_up, ffn2_w_down, final_norm):
    xs = {CTX: x_prompt.reshape(N_PROMPT, D_MODEL), LAT: x_sample.reshape(N_SAMPLE, D_MODEL)}
    cvec = jnp.concatenate([c_ctx[None], c, jnp.zeros((MOD_ROWS - 1 - DEC_BATCH, D_MODEL), f32)], axis=0)
    mod = _modulation(cvec, w_mod, b_mod)

    w_in_p = _w_in_prep(w_in)
    nj = D_MODEL // TJ
    wbg, wbs, wba = (w.astype(bf16).reshape(DEPTH, GLA_VAL, nj, TJ).transpose(0, 2, 1, 3)
                     for w in (w_br_gla, w_br_ssm, w_br_attn))
    wo = w_out.astype(bf16).reshape(DEPTH, nj, TJ, D_MODEL)

    wz = jnp.zeros((DEPTH, 2, SMALL_W, GLA_QK), f32)
    for d in range(2):
        wz = wz.at[:, d, d * GLA_RANK:(d + 1) * GLA_RANK, :].set(gla_w_up[:, d])
    wz = wz.astype(bf16)
    bz = gla_b_up.reshape(DEPTH, 2, 1, GLA_QK)

    def dt_lanes(v):
        vg = v.reshape(DEPTH, 2, SSM_GROUPS, SSM_HPG).transpose(0, 2, 1, 3).reshape(DEPTH, SSM_GROUPS, 2 * SSM_HPG)
        row = jnp.zeros((DEPTH, SSM_GROUPS, SMALL_W), f32)
        row = row.at[:, :, SMALL_DT:SMALL_DT + 2 * SSM_HPG].set(vg)
        return row.reshape(DEPTH, 1, SSM_GROUPS * SMALL_W)

    dtb_rows = dt_lanes(ssm_dt_bias)
    a_rows = dt_lanes(-jnp.exp(ssm_a_log))
    d_skip = jnp.repeat(ssm_d, SSM_HEADDIM, axis=-1).reshape(DEPTH, 1, SSM_INNER)
    conv_w = jnp.concatenate([ssm_conv_w, jnp.zeros((DEPTH, 8 - SSM_CONV, CONV_CH), f32)], axis=1)
    conv_b = ssm_conv_b.reshape(DEPTH, 1, CONV_CH)
    sink8 = jnp.broadcast_to(attn_sink[:, :, None], (DEPTH, ATTN_HEADS, 128))
    cos, sin = _rope_tables()

    n1, nm, n2 = (w.reshape(DEPTH, 1, D_MODEL) for w in (ffn1_norm, mix_norm, ffn2_norm))
    sn = ssm_norm.reshape(DEPTH, 1, SSM_INNER)

    new_k, new_v, new_gla, new_ssm = [], [], [], []
    for l in range(DEPTH):
        mod3 = mod[l].reshape(MOD_ROWS, 1, N_MOD * D_MODEL)
        last = l == DEPTH - 1
        for st in (CTX, LAT):
            x = _ffn(st, xs[st], mod3, l, 0, n1, ffn1_w_gate, ffn1_w_up, ffn1_w_down)
            proj, aux = _inproj(st, x, mod3, l, nm, w_in_p)
            if st is CTX:
                o_gla, st_g = _gla(proj, aux, 0, BATCH, SEQ, wz[l], bz[l], gla_norm[l][None], None)
                o_ssm, st_s = _ssd(proj, aux, 0, BATCH, SEQ, conv_w[l], conv_b[l], dtb_rows[l], a_rows[l],
                                   d_skip[l], None)
                o_att = _attn_ctx(proj, sink8[l])
                new_k.append(aux[:, :ATTN_KV].reshape(BATCH, SEQ, KV_HEADS, HEAD_DIM))
                new_v.append(aux[:, ATTN_KV:2 * ATTN_KV].reshape(BATCH, SEQ, KV_HEADS, HEAD_DIM))
                new_gla.append(st_g)
                new_ssm.append(st_s)
            else:
                o_gla, _ = _gla(proj, aux, 0, DEC_BATCH, DEC_SEQ, wz[l], bz[l], gla_norm[l][None], state_gla[:, l])
                o_ssm, _ = _ssd(proj, aux, 0, DEC_BATCH, DEC_SEQ, conv_w[l], conv_b[l], dtb_rows[l], a_rows[l],
                                d_skip[l], state_ssm[:, l])
                o_att = _attn_lat(proj, cache_k, cache_v, l, cos, sin, sink8[l])
            x = _merge(st, x, mod3, l, proj, o_gla, o_ssm, o_att, sn, wbg, wbs, wba, wo)
            xs[st] = _ffn(st, x, mod3, l, 2, n2, ffn2_w_gate, ffn2_w_up, ffn2_w_down,
                          final_w=final_norm if last else None)

    y_prompt = xs[CTX].reshape(BATCH, SEQ, D_MODEL)
    y_sample = xs[LAT].reshape(DEC_BATCH, DEC_SEQ, D_MODEL)
    return (y_prompt, y_sample, jnp.stack(new_k, axis=1), jnp.stack(new_v, axis=1),
            jnp.stack(new_gla, axis=1), jnp.stack(new_ssm, axis=1))
```

```python
import functools
import math
from typing import NamedTuple

import numpy as np
import jax
import jax.numpy as jnp
from jax import lax
from jax.experimental import pallas as pl
from jax.experimental.pallas import tpu as pltpu

f32 = jnp.float32
bf16 = jnp.bfloat16

D_MODEL = 2048
BATCH = 32
SEQ = 256
DEPTH = 2
DEC_BATCH = 2
DEC_SEQ = 1024
PAST_LEN = 512
GRID_W = 64
RMS_EPS = 1e-6
N_MOD = 9
D_FF = 5632
GLA_HEADS = 4
GLA_DK = 128
GLA_DV = 256
GLA_RANK = 16
GLA_GATE_NORM = 16.0
GLA_CHUNK = 16
SSM_HEADS = 16
SSM_HEADDIM = 64
SSM_GROUPS = 4
SSM_HPG = SSM_HEADS // SSM_GROUPS
SSM_STATE = 128
SSM_CONV = 5
SSM_INNER = SSM_HEADS * SSM_HEADDIM
SSM_BC = SSM_GROUPS * SSM_STATE
CONV_CH = SSM_INNER + 2 * SSM_BC
ATTN_HEADS = 8
KV_HEADS = 2
Q_PER_KV = ATTN_HEADS // KV_HEADS
HEAD_DIM = 128
WINDOW = 128
ROPE_THETA = 10000.0
GLA_QK = GLA_HEADS * GLA_DK
GLA_VAL = GLA_HEADS * GLA_DV
ATTN_Q = ATTN_HEADS * HEAD_DIM
ATTN_KV = KV_HEADS * HEAD_DIM

N_PROMPT = BATCH * SEQ
N_SAMPLE = DEC_BATCH * DEC_SEQ
MOD_ROWS = 8

COL_GQ = 0
COL_GK = COL_GQ + GLA_QK
COL_GV = COL_GK + GLA_QK
COL_GR = COL_GV + GLA_VAL
COL_SZ = COL_GR + GLA_VAL
COL_SX = COL_SZ + SSM_INNER
COL_SB = COL_SX + SSM_INNER
COL_SC = COL_SB + SSM_BC
COL_AQ = COL_SC + SSM_BC
COL_AK = COL_AQ + ATTN_Q
COL_AV = COL_AK + ATTN_KV
COL_SMALL = COL_AV + ATTN_KV
SMALL_W = 128
SMALL_DT = 2 * GLA_RANK
COL_BR = 8192
D_INP = COL_BR + 3 * D_MODEL

TM = 256
TM_BIG = 1024
TF_BIG = 256
TN_IN = 1024
AUX_TILE = COL_AK // TN_IN
AUX_SMALL = (COL_SMALL - COL_AK) // SMALL_W
TN_MOD = 1024
TJ = 512
GLA_SB = 256
GLA_TILE = 128
GLA_HPS = 4
SSD_CHUNK = 128
SSD_GPS = 2
VMEM_LIMIT = 58 * 1024 * 1024

NT = (((1,), (1,)), ((), ()))


def _silu(x):
    return x * jax.nn.sigmoid(x)


class _Stream(NamedTuple):
    n_tok: int
    row0: int
    per_row: int

    def mod_row(self, i, tm):
        return self.row0 + (i * tm) // self.per_row


CTX = _Stream(N_PROMPT, 0, N_PROMPT)
LAT = _Stream(N_SAMPLE, 1, DEC_SEQ)


def _cparams(sem):
    return pltpu.CompilerParams(dimension_semantics=sem, vmem_limit_bytes=VMEM_LIMIT)


def _mod_kernel(c_ref, w_ref, b_ref, o_ref):
    s = _silu(c_ref[...]).astype(bf16)
    o_ref[0] = jnp.dot(s, w_ref[0].astype(bf16), preferred_element_type=f32) + b_ref[0]


def _modulation(cvec, w_mod, b_mod):
    n = N_MOD * D_MODEL
    return pl.pallas_call(
        _mod_kernel,
        grid=(DEPTH, n // TN_MOD),
        in_specs=[
            pl.BlockSpec((MOD_ROWS, D_MODEL), lambda l, j: (0, 0)),
            pl.BlockSpec((1, D_MODEL, TN_MOD), lambda l, j: (l, 0, j)),
            pl.BlockSpec((1, 1, TN_MOD), lambda l, j: (l, 0, j)),
        ],
        out_specs=pl.BlockSpec((1, MOD_ROWS, TN_MOD), lambda l, j: (l, 0, j)),
        out_shape=jax.ShapeDtypeStruct((DEPTH, MOD_ROWS, n), f32),
        compiler_params=_cparams(("arbitrary", "arbitrary")),
        name="modulation",
    )(cvec, w_mod, b_mod.reshape(DEPTH, 1, n))


def _norm_mod(x, nw, sh, sc):
    ms = jnp.mean(x * x, axis=-1, keepdims=True)
    h = x * lax.rsqrt(ms + RMS_EPS) * nw
    return h * (1.0 + sc) + sh


def _ffn_kernel(x_ref, sh_ref, sc_ref, gt_ref, nw_ref, wg_ref, wu_ref, wd_ref, *rest, final):
    if final:
        fw_ref, o_ref, h_ref = rest
    else:
        o_ref, h_ref = rest
    j = pl.program_id(1)

    def hidden_tile():
        h = h_ref[...]
        g = jnp.dot(h, wg_ref[...].astype(bf16), preferred_element_type=f32)
        u = jnp.dot(h, wu_ref[...].astype(bf16), preferred_element_type=f32)
        a = (_silu(g) * u).astype(bf16)
        return jnp.dot(a, wd_ref[...].astype(bf16), preferred_element_type=f32)

    @pl.when(j == 0)
    def _():
        h_ref[...] = _norm_mod(x_ref[...], nw_ref[...], sh_ref[0], sc_ref[0]).astype(bf16)
        o_ref[...] = hidden_tile()

    @pl.when(j > 0)
    def _():
        o_ref[...] += hidden_tile()

    @pl.when(j == pl.num_programs(1) - 1)
    def _():
        y = x_ref[...] + 0.5 * gt_ref[0] * o_ref[...]
        if final:
            ms = jnp.mean(y * y, axis=-1, keepdims=True)
            y = y * lax.rsqrt(ms + RMS_EPS) * fw_ref[...]
        o_ref[...] = y


def _ffn(st, x, mod3, l, slot, nw, wg, wu, wd, final_w=None):
    final = final_w is not None
    tm, tf = TM_BIG, TF_BIG

    def mod_spec(k):
        return pl.BlockSpec((1, 1, D_MODEL), lambda i, j: (st.mod_row(i, tm), 0, 3 * slot + k))

    in_specs = [
        pl.BlockSpec((tm, D_MODEL), lambda i, j: (i, 0)),
        mod_spec(0), mod_spec(1), mod_spec(2),
        pl.BlockSpec((None, 1, D_MODEL), lambda i, j: (l, 0, 0)),
        pl.BlockSpec((None, D_MODEL, tf), lambda i, j: (l, 0, j)),
        pl.BlockSpec((None, D_MODEL, tf), lambda i, j: (l, 0, j)),
        pl.BlockSpec((None, tf, D_MODEL), lambda i, j: (l, j, 0)),
    ]
    args = [x, mod3, mod3, mod3, nw, wg, wu, wd]
    if final:
        in_specs.append(pl.BlockSpec((1, D_MODEL), lambda i, j: (0, 0)))
        args.append(final_w.reshape(1, D_MODEL))
    return pl.pallas_call(
        functools.partial(_ffn_kernel, final=final),
        grid=(st.n_tok // tm, D_FF // tf),
        in_specs=in_specs,
        out_specs=pl.BlockSpec((tm, D_MODEL), lambda i, j: (i, 0)),
        out_shape=jax.ShapeDtypeStruct((st.n_tok, D_MODEL), f32),
        scratch_shapes=[pltpu.VMEM((tm, D_MODEL), bf16)],
        compiler_params=_cparams(("arbitrary", "arbitrary")),
        name="ffn_final" if final else "ffn",
    )(*args)


def _inproj_kernel(x_ref, sh_ref, sc_ref, nw_ref, w_ref, o_ref, os_ref, h_ref):
    j = pl.program_id(1)

    def tile():
        return jnp.dot(h_ref[...], w_ref[...], preferred_element_type=f32)

    @pl.when(j == 0)
    def _():
        h_ref[...] = _norm_mod(x_ref[...], nw_ref[...], sh_ref[0], sc_ref[0]).astype(bf16)
        o_ref[...] = tile().astype(bf16)

    @pl.when((j > 0) & (j < COL_BR // TN_IN) & (j != AUX_TILE))
    def _():
        o_ref[...] = tile().astype(bf16)

    @pl.when(j == AUX_TILE)
    def _():
        t = tile()
        os_ref[...] = t
        o_ref[...] = t.astype(bf16)

    @pl.when(j >= COL_BR // TN_IN)
    def _():
        o_ref[...] = jax.nn.sigmoid(tile()).astype(bf16)


def _inproj(st, x, mod3, l, nw, w):
    tm = TM_BIG

    def mod_spec(k):
        return pl.BlockSpec((1, 1, D_MODEL), lambda i, j: (st.mod_row(i, tm), 0, 3 + k))

    return pl.pallas_call(
        _inproj_kernel,
        grid=(st.n_tok // tm, D_INP // TN_IN),
        in_specs=[
            pl.BlockSpec((tm, D_MODEL), lambda i, j: (i, 0)),
            mod_spec(0), mod_spec(1),
            pl.BlockSpec((None, 1, D_MODEL), lambda i, j: (l, 0, 0)),
            pl.BlockSpec((None, D_MODEL, TN_IN), lambda i, j: (l, 0, j)),
        ],
        out_specs=[pl.BlockSpec((tm, TN_IN), lambda i, j: (i, j)),
                   pl.BlockSpec((tm, TN_IN), lambda i, j: (i, 0))],
        out_shape=[jax.ShapeDtypeStruct((st.n_tok, D_INP), bf16), jax.ShapeDtypeStruct((st.n_tok, TN_IN), f32)],
        scratch_shapes=[pltpu.VMEM((tm, D_MODEL), bf16)],
        compiler_params=_cparams(("arbitrary", "arbitrary")),
        name="inproj",
    )(x, mod3, mod3, nw, w)


def _split_dot(m, x):
    n = x.shape[1]
    hi = x.astype(bf16)
    lo = (x - hi.astype(f32)).astype(bf16)
    r = jnp.dot(m, jnp.concatenate([hi, lo], axis=1), preferred_element_type=f32)
    return r[:, :n] + r[:, n:]


def _log_sigmoid(x):
    return jnp.minimum(x, 0.0) - jnp.log(1.0 + jnp.exp(-jnp.abs(x)))


def _gla_kernel(q_ref, k_ref, v_ref, r_ref, sm_ref, wz_ref, bz_ref, gn_ref, *rest, seq, has_h0, emit_state,
                alias_prev):
    rest = list(rest)
    h0_ref = rest.pop(0) if has_h0 else None
    if alias_prev:
        rest.pop(0)
    o_ref = rest.pop(0)
    st_ref = rest.pop(0) if emit_state else None
    oin_ref, qe_ref, ke_ref, dec_ref, vt_ref = rest
    hps = GLA_HPS
    sb = GLA_SB
    tile = GLA_TILE
    per_tile = tile // GLA_CHUNK
    n_tile = seq // tile

    row = lax.broadcasted_iota(jnp.int32, (sb, sb), 0)
    col = lax.broadcasted_iota(jnp.int32, (sb, sb), 1)
    same = (row // GLA_CHUNK) == (col // GLA_CHUNK)
    masks = (same & (col <= row), same & (col >= row))
    blk = jnp.where(same, 1.0, 0.0).astype(bf16)
    tri_blk = tuple(jnp.concatenate([jnp.where(m, 1.0, 0.0).astype(bf16), blk], axis=0) for m in masks)

    chains = [(hh, d) for hh in range(hps) for d in range(2)]
    kcols = [slice(hh * GLA_DK, (hh + 1) * GLA_DK) for hh in range(hps)]
    vcols = [slice(hh * GLA_DV, (hh + 1) * GLA_DV) for hh in range(hps)]

    for s in range(seq // sb):
        rows = slice(s * sb, (s + 1) * sb)
        sm = sm_ref[rows, :].astype(bf16)
        q = [q_ref[rows, kcols[hh]].astype(f32) * (GLA_DK ** -0.5) for hh in range(hps)]
        k = [k_ref[rows, kcols[hh]].astype(f32) for hh in range(hps)]
        v = [v_ref[rows, vcols[hh]] for hh in range(hps)]
        for hh in range(hps):
            v_t = v[hh].astype(f32).T.astype(bf16)
            for t in range(sb // tile):
                vt_ref[hh, s * (sb // tile) + t] = v_t[:, t * tile:(t + 1) * tile]
        gz = [jnp.dot(sm, wz_ref[d, :, kcols[hh]], preferred_element_type=f32) + bz_ref[d, :, kcols[hh]]
              for hh, d in chains]
        la = [_log_sigmoid(g) * (1.0 / GLA_GATE_NORM) for g in gz]
        sums = [_split_dot(tri_blk[d], la[i]) for i, (hh, d) in enumerate(chains)]
        cum = [x[:sb] for x in sums]
        tot = [x[sb:] for x in sums]
        qe = [q[hh] * jnp.exp(cum[i]) for i, (hh, d) in enumerate(chains)]
        kinv = [k[hh] * jnp.exp(-cum[i]) for i, (hh, d) in enumerate(chains)]
        kend = [k[hh] * jnp.exp(tot[i] - cum[i]) for i, (hh, d) in enumerate(chains)]
        a = [lax.dot_general(qe[i].astype(bf16), kinv[i].astype(bf16), NT, preferred_element_type=f32)
             for i in range(len(chains))]
        a = [jnp.where(masks[d], a[i], 0.0).astype(bf16) for i, (hh, d) in enumerate(chains)]
        for i, (hh, d) in enumerate(chains):
            oin_ref[hh, d, rows, :] = jnp.dot(a[i], v[hh], preferred_element_type=f32)
            qe_ref[hh, d, rows, :] = qe[i].astype(bf16)
            ke_ref[hh, d, rows, :] = kend[i].astype(bf16)
            dec_ref[hh, d, rows, :] = jnp.exp(tot[i])

    if has_h0:
        s_init = tuple(h0_ref[d, hh].T for hh, d in chains)
    else:
        s_init = (jnp.zeros((GLA_DV, GLA_DK), f32),) * len(chains)

    def scan_tile(o, carry):
        sts = list(carry)
        for u in range(per_tile):
            for ci, (hh, d) in enumerate(chains):
                t_idx = o if d == 0 else n_tile - 1 - o
                pos = u if d == 0 else per_tile - 1 - u
                i0 = t_idx * tile + pos * GLA_CHUNK
                if not isinstance(i0, int):
                    i0 = pl.multiple_of(i0, GLA_CHUNK)
                rows = pl.ds(i0, GLA_CHUNK)
                inter = lax.dot_general(qe_ref[hh, d, rows, :], sts[ci].astype(bf16), NT,
                                        preferred_element_type=f32)
                oin_ref[hh, d, rows, :] += inter
                pieces = [ke_ref[hh, d, rows, :]]
                if pos:
                    pieces.insert(0, jnp.zeros((pos * GLA_CHUNK, GLA_DK), bf16))
                if pos < per_tile - 1:
                    pieces.append(jnp.zeros(((per_tile - 1 - pos) * GLA_CHUNK, GLA_DK), bf16))
                upd = jnp.dot(vt_ref[hh, t_idx], jnp.concatenate(pieces, axis=0), preferred_element_type=f32)
                sts[ci] = sts[ci] * dec_ref[hh, d, pl.ds(i0, 1), :] + upd
        return tuple(sts)

    if n_tile <= 2:
        s_fin = s_init
        for o in range(n_tile):
            s_fin = scan_tile(o, s_fin)
    else:
        s_fin = lax.fori_loop(0, n_tile, scan_tile, s_init)

    for hh in range(hps):
        vcols = slice(hh * GLA_DV, (hh + 1) * GLA_DV)
        o = oin_ref[hh, 0] + oin_ref[hh, 1]
        ms = jnp.mean(o * o, axis=-1, keepdims=True)
        o = o * lax.rsqrt(ms + RMS_EPS) * gn_ref[...]
        o_ref[:, vcols] = (o * _silu(r_ref[:, vcols].astype(f32))).astype(o_ref.dtype)
    if emit_state:
        for ci, (hh, d) in enumerate(chains):
            st_ref[0, d, hh] = s_fin[ci].T


def _gla(proj, small, row_blk0, n_seq, seq, wz, bz, gn, h0, layer=0, prev_state=None):
    has_h0 = h0 is not None
    emit_state = not has_h0
    aliases = {}
    kq, kv_, hps = GLA_DK, GLA_DV, GLA_HPS
    kw, vw = hps * kq, hps * kv_
    in_specs = [
        pl.BlockSpec((seq, kw), lambda b, h: (row_blk0 + b, COL_GQ // kw + h)),
        pl.BlockSpec((seq, kw), lambda b, h: (row_blk0 + b, COL_GK // kw + h)),
        pl.BlockSpec((seq, vw), lambda b, h: (row_blk0 + b, COL_GV // vw + h)),
        pl.BlockSpec((seq, vw), lambda b, h: (row_blk0 + b, COL_GR // vw + h)),
        pl.BlockSpec((seq, SMALL_W), lambda b, h: (row_blk0 + b, AUX_SMALL)),
        pl.BlockSpec((2, SMALL_W, kw), lambda b, h: (0, 0, h)),
        pl.BlockSpec((2, 1, kw), lambda b, h: (0, 0, h)),
        pl.BlockSpec((1, kv_), lambda b, h: (0, 0)),
    ]
    args = [proj, proj, proj, proj, small, wz, bz, gn]
    if has_h0:
        in_specs.append(pl.BlockSpec((None, 2, hps, kq, kv_), lambda b, h: (b, 0, h, 0, 0)))
        args.append(h0)
    out_specs = [pl.BlockSpec((seq, vw), lambda b, h: (b, h))]
    out_shape = [jax.ShapeDtypeStruct((n_seq * seq, GLA_VAL), bf16)]
    if emit_state:
        out_specs.append(pl.BlockSpec((1, None, 2, hps, kq, kv_), lambda b, h: (b, layer, 0, h, 0, 0)))
        out_shape.append(jax.ShapeDtypeStruct((n_seq, DEPTH, 2, GLA_HEADS, kq, kv_), f32))
        if prev_state is not None:
            in_specs.append(pl.BlockSpec(memory_space=pl.ANY))
            args.append(prev_state)
            aliases = {len(args) - 1: 1}
    res = pl.pallas_call(
        functools.partial(_gla_kernel, seq=seq, has_h0=has_h0, emit_state=emit_state, alias_prev=bool(aliases)),
        input_output_aliases=aliases,
        grid=(n_seq, GLA_HEADS // hps),
        in_specs=in_specs,
        out_specs=out_specs,
        out_shape=out_shape,
        scratch_shapes=[
            pltpu.VMEM((hps, 2, seq, kv_), f32),
            pltpu.VMEM((hps, 2, seq, kq), bf16),
            pltpu.VMEM((hps, 2, seq, kq), bf16),
            pltpu.VMEM((hps, 2, seq, kq), f32),
            pltpu.VMEM((hps, seq // GLA_TILE, kv_, GLA_TILE), bf16),
        ],
        compiler_params=_cparams(("parallel", "arbitrary")),
        name="gla_lat" if has_h0 else "gla_ctx",
    )(*args)
    return res if emit_state else (res[0], None)


def _softplus(x):
    return jnp.maximum(x, 0.0) + jnp.log(1.0 + jnp.exp(-jnp.abs(x)))


def _split3(x):
    hi = x.astype(bf16)
    r1 = x - hi.astype(f32)
    mid = r1.astype(bf16)
    lo = (r1 - mid.astype(f32)).astype(bf16)
    return [hi, mid, lo]


def _split3_dot(m, x):
    n = x.shape[1]
    r = jnp.dot(m, jnp.concatenate(_split3(x), axis=1), preferred_element_type=f32)
    return r[:, :n] + r[:, n:2 * n] + r[:, 2 * n:]


def _dot_sel(x, sel):
    return sum(jnp.dot(t, sel, preferred_element_type=f32) for t in _split3(x))


def _shift_rows(x, d, t_idx):
    n = x.shape[0]
    if d == 0:
        return x
    y = pltpu.roll(x, (-d) % n, 0)
    ok = (t_idx + d >= 0) & (t_idx + d < n)
    return jnp.where(ok, y, 0.0)


def _ssd_kernel(z_ref, x_ref, b_ref, c_ref, sm_ref, cwx_ref, cwb_ref, cwc_ref, cbx_ref, cbb_ref, cbc_ref,
                dtb_ref, a_ref, dsk_ref, *rest, seq, has_h0, emit_state, alias_prev):
    rest = list(rest)
    h0_ref = rest.pop(0) if has_h0 else None
    if alias_prev:
        rest.pop(0)
    o_ref = rest.pop(0)
    st_ref = rest.pop(0) if emit_state else None
    y_ref, xs_ref, bm_ref, cm_ref, dt_ref = rest
    gps = SSD_GPS
    cs = SSD_CHUNK
    n_chunk = seq // cs
    hp = SSM_HPG * SSM_HEADDIM

    t_idx = lax.broadcasted_iota(jnp.int32, (seq, 1), 0)

    def conv_silu(src_ref, w_ref, bias_ref):
        xin = src_ref[...].astype(f32)
        acc = jnp.zeros_like(xin) + bias_ref[...]
        for j in range(SSM_CONV):
            acc = acc + w_ref[j:j + 1, :] * _shift_rows(xin, j - SSM_CONV // 2, t_idx)
        return _silu(acc)

    xs_ref[...] = conv_silu(x_ref, cwx_ref, cbx_ref)
    bm_ref[...] = conv_silu(b_ref, cwb_ref, cbb_ref)
    cm_ref[...] = conv_silu(c_ref, cwc_ref, cbc_ref)

    row = lax.broadcasted_iota(jnp.int32, (cs, cs), 0)
    col = lax.broadcasted_iota(jnp.int32, (cs, cs), 1)
    masks = (col <= row, col >= row)
    tris = tuple(jnp.where(m, 1.0, 0.0).astype(bf16) for m in masks)
    lane = lax.broadcasted_iota(jnp.int32, (1, gps * SMALL_W), 1) % SMALL_W

    dt_lane = (lane >= SMALL_DT) & (lane < SMALL_DT + 2 * SSM_HPG)
    a_row = a_ref[...]

    def selectors(d):
        r = lax.broadcasted_iota(jnp.int32, (SMALL_W, SSM_HPG * cs), 0)
        c = lax.broadcasted_iota(jnp.int32, (SMALL_W, SSM_HPG * cs), 1)
        bcast = jnp.where(r == SMALL_DT + d * SSM_HPG + c // cs, 1.0, 0.0).astype(bf16)
        r = lax.broadcasted_iota(jnp.int32, (SMALL_W, hp), 0)
        c = lax.broadcasted_iota(jnp.int32, (SMALL_W, hp), 1)
        expand = jnp.where(r == SMALL_DT + d * SSM_HPG + c // SSM_HEADDIM, 1.0, 0.0).astype(bf16)
        return jnp.concatenate([bcast, expand], axis=1)

    head_of_lane = lax.broadcasted_iota(jnp.int32, (cs, hp), 1) // SSM_HEADDIM
    sl = [slice(g * SMALL_W, (g + 1) * SMALL_W) for g in range(gps)]
    hl = [slice(g * hp, (g + 1) * hp) for g in range(gps)]
    bm_t = [[bm_ref[c * cs:(c + 1) * cs, sl[g]].T.astype(bf16) for c in range(n_chunk)]
            for g in range(gps)]

    dt_ref[...] = jnp.where(dt_lane, _softplus(sm_ref[...] + dtb_ref[...]), 0.0)
    y_ref[...] = jnp.zeros_like(y_ref)
    sel = [selectors(0), selectors(1)]
    chains = [(g, d) for g in range(gps) for d in range(2)]
    n = range(len(chains))
    if has_h0:
        st = [h0_ref[d, g * SSM_HPG:(g + 1) * SSM_HPG].reshape(hp, SSM_STATE).T for g, d in chains]
    else:
        st = [jnp.zeros((SSM_STATE, hp), f32)] * len(chains)
    for t in range(n_chunk):
        cidx = [t if d == 0 else n_chunk - 1 - t for g, d in chains]
        rows = [slice(c * cs, (c + 1) * cs) for c in cidx]
        dt = [dt_ref[rows[i], sl[g]] for i, (g, d) in enumerate(chains)]
        cum = [_split3_dot(tris[d], dt[i] * a_row[:, sl[g]]) for i, (g, d) in enumerate(chains)]
        cum_t = [cum[i].T for i in n]
        cum_s = [_dot_sel(cum[i], sel[d]) for i, (g, d) in enumerate(chains)]
        cum_b = [x[:, :SSM_HPG * cs] for x in cum_s]
        cum_e = [x[:, SSM_HPG * cs:] for x in cum_s]
        dt_e = [_dot_sel(dt[i], sel[d][:, SSM_HPG * cs:]) for i, (g, d) in enumerate(chains)]
        tot_e = [cum_e[i][cs - 1:cs, :] if d == 0 else cum_e[i][0:1, :] for i, (g, d) in enumerate(chains)]
        xs = [xs_ref[rows[i], hl[g]] for i, (g, d) in enumerate(chains)]
        cm = [cm_ref[rows[i], sl[g]].astype(bf16) for i, (g, d) in enumerate(chains)]
        cb = [lax.dot_general(cm[i], bm_ref[rows[i], sl[g]].astype(bf16), NT, preferred_element_type=f32)
              for i, (g, d) in enumerate(chains)]
        seg = [[jnp.exp(jnp.where(masks[d], cum_b[i][:, j * cs:(j + 1) * cs]
                                  - cum_t[i][SMALL_DT + d * SSM_HPG + j:SMALL_DT + d * SSM_HPG + j + 1, :],
                                  -jnp.inf)) for j in range(SSM_HPG)] for i, (g, d) in enumerate(chains)]
        w = [jnp.concatenate([(cb[i] * seg[i][j]).astype(bf16) for j in range(SSM_HPG)], axis=1)
             for i in n]
        xd = [xs[i] * dt_e[i] for i in n]
        xd_bd = [jnp.concatenate([jnp.where(head_of_lane == j, xd[i], 0.0).astype(bf16)
                                  for j in range(SSM_HPG)], axis=0) for i in n]
        inter = [jnp.dot(cm[i], st[i].astype(bf16), preferred_element_type=f32) for i in n]
        y_new = [jnp.dot(w[i], xd_bd[i], preferred_element_type=f32) + jnp.exp(cum_e[i]) * inter[i] for i in n]
        xw = [(xd[i] * jnp.exp(tot_e[i] - cum_e[i])).astype(bf16) for i in n]
        st = [st[i] * jnp.exp(tot_e[i]) + jnp.dot(bm_t[g][cidx[i]], xw[i], preferred_element_type=f32)
              for i, (g, d) in enumerate(chains)]
        for i, (g, d) in enumerate(chains):
            y_ref[rows[i], hl[g]] += y_new[i]

    y = y_ref[...] + dsk_ref[...] * xs_ref[...]
    o_ref[...] = y * _silu(z_ref[...].astype(f32))
    if emit_state:
        for i, (g, d) in enumerate(chains):
            st_ref[0, d, g * SSM_HPG:(g + 1) * SSM_HPG] = st[i].T.reshape(SSM_HPG, SSM_HEADDIM, SSM_STATE)


def _ssd(proj, small, row_blk0, n_seq, seq, conv_w, conv_b, dt_bias_row, a_row, d_skip, h0, layer=0,
         prev_state=None):
    has_h0 = h0 is not None
    emit_state = not has_h0
    aliases = {}
    gps = SSD_GPS
    hw = gps * SSM_HPG * SSM_HEADDIM
    nw = gps * SSM_STATE
    hg = gps * SSM_HPG
    in_specs = [
        pl.BlockSpec((seq, hw), lambda b, g: (row_blk0 + b, COL_SZ // hw + g)),
        pl.BlockSpec((seq, hw), lambda b, g: (row_blk0 + b, COL_SX // hw + g)),
        pl.BlockSpec((seq, nw), lambda b, g: (row_blk0 + b, COL_SB // nw + g)),
        pl.BlockSpec((seq, nw), lambda b, g: (row_blk0 + b, COL_SC // nw + g)),
        pl.BlockSpec((seq, nw), lambda b, g: (row_blk0 + b, AUX_SMALL // gps + g)),
        pl.BlockSpec((8, hw), lambda b, g: (0, g)),
        pl.BlockSpec((8, nw), lambda b, g: (0, SSM_INNER // nw + g)),
        pl.BlockSpec((8, nw), lambda b, g: (0, (SSM_INNER + SSM_BC) // nw + g)),
        pl.BlockSpec((1, hw), lambda b, g: (0, g)),
        pl.BlockSpec((1, nw), lambda b, g: (0, SSM_INNER // nw + g)),
        pl.BlockSpec((1, nw), lambda b, g: (0, (SSM_INNER + SSM_BC) // nw + g)),
        pl.BlockSpec((1, nw), lambda b, g: (0, g)),
        pl.BlockSpec((1, nw), lambda b, g: (0, g)),
        pl.BlockSpec((1, hw), lambda b, g: (0, g)),
    ]
    args = [proj, proj, proj, proj, small, conv_w, conv_w, conv_w, conv_b, conv_b, conv_b,
            dt_bias_row, a_row, d_skip]
    if has_h0:
        in_specs.append(pl.BlockSpec((None, 2, hg, SSM_HEADDIM, SSM_STATE), lambda b, g: (b, 0, g, 0, 0)))
        args.append(h0)
    out_specs = [pl.BlockSpec((seq, hw), lambda b, g: (b, g))]
    out_shape = [jax.ShapeDtypeStruct((n_seq * seq, SSM_INNER), f32)]
    if emit_state:
        out_specs.append(pl.BlockSpec((1, None, 2, hg, SSM_HEADDIM, SSM_STATE), lambda b, g: (b, layer, 0, g, 0, 0)))
        out_shape.append(jax.ShapeDtypeStruct((n_seq, DEPTH, 2, SSM_HEADS, SSM_HEADDIM, SSM_STATE), f32))
        if prev_state is not None:
            in_specs.append(pl.BlockSpec(memory_space=pl.ANY))
            args.append(prev_state)
            aliases = {len(args) - 1: 1}
    res = pl.pallas_call(
        functools.partial(_ssd_kernel, seq=seq, has_h0=has_h0, emit_state=emit_state, alias_prev=bool(aliases)),
        input_output_aliases=aliases,
        grid=(n_seq, SSM_GROUPS // gps),
        in_specs=in_specs,
        out_specs=out_specs,
        out_shape=out_shape,
        scratch_shapes=[
            pltpu.VMEM((seq, hw), f32),
            pltpu.VMEM((seq, hw), f32),
            pltpu.VMEM((seq, nw), f32),
            pltpu.VMEM((seq, nw), f32),
            pltpu.VMEM((seq, nw), f32),
        ],
        compiler_params=_cparams(("parallel", "arbitrary")),
        name="ssd_lat" if has_h0 else "ssd_ctx",
    )(*args)
    return res if emit_state else (res[0], None)


def _attn_ctx_kernel(q_ref, k_ref, v_ref, sink_ref, o_ref):
    heads = range(ATTN_HEADS)
    kv = [slice(h // Q_PER_KV * HEAD_DIM, (h // Q_PER_KV + 1) * HEAD_DIM) for h in heads]
    cols = [slice(h * HEAD_DIM, (h + 1) * HEAD_DIM) for h in heads]
    s = [lax.dot_general(q_ref[:, cols[h]], k_ref[:, kv[h]], NT, preferred_element_type=f32) * (HEAD_DIM ** -0.5)
         for h in heads]
    sink = [sink_ref[h:h + 1, 0:1] for h in heads]
    m = [jnp.maximum(jnp.max(s[h], axis=-1, keepdims=True), sink[h]) for h in heads]
    p = [jnp.exp(s[h] - m[h]) for h in heads]
    den = [jnp.sum(p[h], axis=-1, keepdims=True) + jnp.exp(sink[h] - m[h]) for h in heads]
    for h in heads:
        o = jnp.dot(p[h].astype(bf16), v_ref[:, kv[h]], preferred_element_type=f32) / den[h]
        o_ref[:, cols[h]] = o.astype(o_ref.dtype)


def _attn_ctx(proj, sink8, n_seq=BATCH):
    return pl.pallas_call(
        _attn_ctx_kernel,
        grid=(n_seq,),
        in_specs=[
            pl.BlockSpec((SEQ, ATTN_Q), lambda b: (b, COL_AQ // ATTN_Q)),
            pl.BlockSpec((SEQ, ATTN_KV), lambda b: (b, COL_AK // ATTN_KV)),
            pl.BlockSpec((SEQ, ATTN_KV), lambda b: (b, COL_AV // ATTN_KV)),
            pl.BlockSpec((ATTN_HEADS, 128), lambda b: (0, 0)),
        ],
        out_specs=pl.BlockSpec((SEQ, ATTN_Q), lambda b: (b, 0)),
        out_shape=jax.ShapeDtypeStruct((n_seq * SEQ, ATTN_Q), bf16),
        compiler_params=_cparams(("arbitrary",)),
        name="attn_ctx",
    )(proj, proj, proj, sink8)


def _rope(x, cos, sin_signed):
    quarter = HEAD_DIM // 4
    lane = lax.broadcasted_iota(jnp.int32, x.shape, 1)
    first = (lane % (2 * quarter)) < quarter
    partner = jnp.where(first, pltpu.roll(x, HEAD_DIM - quarter, 1), pltpu.roll(x, quarter, 1))
    return x * cos + partner * sin_signed


def _attn_lat_kernel(q_ref, k_ref, v_ref, kc_ref, vc_ref, cos_ref, sin_ref, sink_ref, o_ref, kr_ref):
    kvh = pl.program_id(1)
    cos = cos_ref[...]
    sin = sin_ref[...]
    kr_ref[...] = _rope(k_ref[...].astype(f32), cos, sin).astype(bf16)
    kc = kc_ref[...].astype(bf16)
    vc = vc_ref[...].astype(bf16)
    blk = WINDOW
    n_blk = DEC_SEQ // blk
    scale = HEAD_DIM ** -0.5
    for i in range(n_blk):
        lo = max(i - 1, 0) * blk
        hi = min(i + 2, n_blk) * blk
        kw = kr_ref[lo:hi, :]
        vw = v_ref[lo:hi, :].astype(bf16)
        qpos = i * blk + lax.broadcasted_iota(jnp.int32, (blk, hi - lo), 0)
        kpos = lo + lax.broadcasted_iota(jnp.int32, (blk, hi - lo), 1)
        win = jnp.abs(qpos - kpos) <= WINDOW
        rows = slice(i * blk, (i + 1) * blk)
        grp = range(Q_PER_KV)
        cols = [slice(g * HEAD_DIM, (g + 1) * HEAD_DIM) for g in grp]
        q = [_rope(q_ref[rows, cols[g]].astype(f32), cos[rows, :], sin[rows, :]).astype(bf16) for g in grp]
        s_c = [lax.dot_general(q[g], kc, NT, preferred_element_type=f32) * scale for g in grp]
        s_w = [jnp.where(win, lax.dot_general(q[g], kw, NT, preferred_element_type=f32) * scale, -jnp.inf)
               for g in grp]
        sink = [sink_ref[pl.ds(kvh * Q_PER_KV + g, 1), 0:1] for g in grp]
        m = [jnp.maximum(jnp.maximum(jnp.max(s_c[g], axis=-1, keepdims=True),
                                     jnp.max(s_w[g], axis=-1, keepdims=True)), sink[g]) for g in grp]
        p_c = [jnp.exp(s_c[g] - m[g]) for g in grp]
        p_w = [jnp.exp(s_w[g] - m[g]) for g in grp]
        den = [jnp.sum(p_c[g], axis=-1, keepdims=True) + jnp.sum(p_w[g], axis=-1, keepdims=True)
               + jnp.exp(sink[g] - m[g]) for g in grp]
        for g in grp:
            o = (jnp.dot(p_c[g].astype(bf16), vc, preferred_element_type=f32)
                 + jnp.dot(p_w[g].astype(bf16), vw, preferred_element_type=f32)) / den[g]
            o_ref[rows, cols[g]] = o.astype(o_ref.dtype)


def _attn_lat(proj, cache_k, cache_v, l, cos, sin, sink8, rb0=0, n_seq=DEC_BATCH):
    qw = Q_PER_KV * HEAD_DIM
    ck = cache_k.reshape(n_seq, DEPTH, PAST_LEN, ATTN_KV)
    cv = cache_v.reshape(n_seq, DEPTH, PAST_LEN, ATTN_KV)
    return pl.pallas_call(
        _attn_lat_kernel,
        grid=(n_seq, KV_HEADS),
        in_specs=[
            pl.BlockSpec((DEC_SEQ, qw), lambda b, h: (rb0 + b, COL_AQ // qw + h)),
            pl.BlockSpec((DEC_SEQ, HEAD_DIM), lambda b, h: (rb0 + b, COL_AK // HEAD_DIM + h)),
            pl.BlockSpec((DEC_SEQ, HEAD_DIM), lambda b, h: (rb0 + b, COL_AV // HEAD_DIM + h)),
            pl.BlockSpec((None, None, PAST_LEN, HEAD_DIM), lambda b, h: (b, l, 0, h)),
            pl.BlockSpec((None, None, PAST_LEN, HEAD_DIM), lambda b, h: (b, l, 0, h)),
            pl.BlockSpec((DEC_SEQ, HEAD_DIM), lambda b, h: (0, 0)),
            pl.BlockSpec((DEC_SEQ, HEAD_DIM), lambda b, h: (0, 0)),
            pl.BlockSpec((ATTN_HEADS, 128), lambda b, h: (0, 0)),
        ],
        out_specs=pl.BlockSpec((DEC_SEQ, qw), lambda b, h: (b, h)),
        out_shape=jax.ShapeDtypeStruct((n_seq * DEC_SEQ, ATTN_Q), bf16),
        scratch_shapes=[pltpu.VMEM((DEC_SEQ, HEAD_DIM), bf16)],
        compiler_params=_cparams(("parallel", "arbitrary")),
        name="attn_lat",
    )(proj, proj, proj, ck, cv, cos, sin, sink8)


def _rope_tables():
    quarter = HEAD_DIM // 4
    freqs = ROPE_THETA ** (-np.arange(quarter, dtype=np.float32) / quarter)
    t = np.arange(DEC_SEQ)
    cos = np.zeros((DEC_SEQ, HEAD_DIM), np.float32)
    sin = np.zeros((DEC_SEQ, HEAD_DIM), np.float32)
    for half, pos in enumerate((t // GRID_W, t % GRID_W)):
        ang = pos.astype(np.float32)[:, None] * freqs[None, :]
        base = half * 2 * quarter
        cos[:, base:base + quarter] = np.cos(ang)
        cos[:, base + quarter:base + 2 * quarter] = np.cos(ang)
        sin[:, base:base + quarter] = -np.sin(ang)
        sin[:, base + quarter:base + 2 * quarter] = np.sin(ang)
    return jnp.asarray(cos), jnp.asarray(sin)


def _merge_kernel(x_ref, gt_ref, og_ref, os_ref, oa_ref, b0_ref, b1_ref, b2_ref, sn_ref,
                  wg_ref, ws_ref, wa_ref, wo_ref, o_ref):
    y = os_ref[...]
    ms = jnp.mean(y * y, axis=-1, keepdims=True)
    osn = (y * lax.rsqrt(ms + RMS_EPS) * sn_ref[...]).astype(bf16)
    og = og_ref[...]
    oa = oa_ref[...]
    for jt in range(D_MODEL // TJ):
        cols = slice(jt * TJ, (jt + 1) * TJ)
        m = (b0_ref[:, cols].astype(f32) * jnp.dot(og, wg_ref[:, cols], preferred_element_type=f32)
             + b1_ref[:, cols].astype(f32) * jnp.dot(osn, ws_ref[:, cols], preferred_element_type=f32)
             + b2_ref[:, cols].astype(f32) * jnp.dot(oa, wa_ref[:, cols], preferred_element_type=f32))
        part = jnp.dot(m.astype(bf16), wo_ref[cols, :], preferred_element_type=f32)
        if jt == 0:
            o_ref[...] = part
        else:
            o_ref[...] += part
    o_ref[...] = x_ref[...] + gt_ref[0] * o_ref[...]


def _merge(st, x, mod3, l, proj, o_gla, o_ssm, o_att, ssm_norm, w_g, w_s, w_a, w_o):
    tm = TM

    def br_spec(k):
        return pl.BlockSpec((tm, D_MODEL), lambda i: (i, COL_BR // D_MODEL + k))

    def wbr_spec():
        return pl.BlockSpec((None, GLA_VAL, D_MODEL), lambda i: (l, 0, 0), pipeline_mode=pl.Buffered(1))

    return pl.pallas_call(
        _merge_kernel,
        grid=(st.n_tok // tm,),
        in_specs=[
            pl.BlockSpec((tm, D_MODEL), lambda i: (i, 0)),
            pl.BlockSpec((1, 1, D_MODEL), lambda i: (st.mod_row(i, tm), 0, 5)),
            pl.BlockSpec((tm, GLA_VAL), lambda i: (i, 0)),
            pl.BlockSpec((tm, SSM_INNER), lambda i: (i, 0)),
            pl.BlockSpec((tm, ATTN_Q), lambda i: (i, 0)),
            br_spec(0), br_spec(1), br_spec(2),
            pl.BlockSpec((None, 1, SSM_INNER), lambda i: (l, 0, 0)),
            wbr_spec(), wbr_spec(), wbr_spec(),
            pl.BlockSpec((None, D_MODEL, D_MODEL), lambda i: (l, 0, 0), pipeline_mode=pl.Buffered(1)),
        ],
        out_specs=pl.BlockSpec((tm, D_MODEL), lambda i: (i, 0)),
        out_shape=jax.ShapeDtypeStruct((st.n_tok, D_MODEL), f32),
        compiler_params=_cparams(("arbitrary",)),
        name="merge",
    )(x, mod3, o_gla, o_ssm, o_att, proj, proj, proj, ssm_norm, w_g, w_s, w_a, w_o)


_IN_SRC = np.cumsum((0, GLA_QK, GLA_QK, GLA_VAL, GLA_VAL, 2 * GLA_RANK, SSM_INNER, CONV_CH, 2 * SSM_HEADS,
                     ATTN_Q, ATTN_KV, ATTN_KV, 3 * D_MODEL))
SRC_GDOWN, SRC_SZ, SRC_DT, SRC_AQ, SRC_BR, SRC_END = (int(_IN_SRC[k]) for k in (4, 5, 7, 8, 11, 12))
W_PREP_K = 256
W_PREP_CHUNK = 512


def _w_in_prep_kernel(wt_ref, o_ref):
    def move(src, dst, width):
        for c0 in range(0, width, W_PREP_CHUNK):
            w = min(W_PREP_CHUNK, width - c0)
            o_ref[0, :, dst + c0:dst + c0 + w] = wt_ref[0, src + c0:src + c0 + w, :].T.astype(bf16)

    move(0, COL_GQ, SRC_GDOWN)
    move(SRC_SZ, COL_SZ, SRC_DT - SRC_SZ)
    move(SRC_AQ, COL_AQ, SRC_BR - SRC_AQ)
    move(SRC_BR, COL_BR, SRC_END - SRC_BR)

    n_gd, n_dt = 2 * GLA_RANK, 2 * SSM_HEADS
    t_gd = wt_ref[0, SRC_GDOWN:SRC_GDOWN + n_gd, :].astype(bf16)
    t_dt = wt_ref[0, SRC_DT:SRC_DT + n_dt, :].astype(bf16)
    r = lax.broadcasted_iota(jnp.int32, (SMALL_W, n_gd), 0)
    c = lax.broadcasted_iota(jnp.int32, (SMALL_W, n_gd), 1)
    sel_gd = jnp.where(r == c, 1.0, 0.0).astype(bf16)
    k = r - SMALL_DT
    for g in range(SSM_GROUPS):
        src_row = (k // SSM_HPG) * SSM_HEADS + g * SSM_HPG + k % SSM_HPG
        sel_dt = jnp.where((k >= 0) & (k < 2 * SSM_HPG) & (c == src_row), 1.0, 0.0).astype(bf16)
        blk = (jnp.dot(sel_gd, t_gd, preferred_element_type=f32)
               + jnp.dot(sel_dt, t_dt, preferred_element_type=f32))
        o_ref[0, :, COL_SMALL + g * SMALL_W:COL_SMALL + (g + 1) * SMALL_W] = blk.T.astype(bf16)


def _w_in_prep(w_in):
    wt = jnp.swapaxes(w_in, 1, 2)
    return pl.pallas_call(
        _w_in_prep_kernel,
        grid=(DEPTH, D_MODEL // W_PREP_K),
        in_specs=[pl.BlockSpec((1, SRC_END, W_PREP_K), lambda l, i: (l, 0, i))],
        out_specs=pl.BlockSpec((1, W_PREP_K, D_INP), lambda l, i: (l, i, 0)),
        out_shape=jax.ShapeDtypeStruct((DEPTH, D_MODEL, D_INP), bf16),
        compiler_params=_cparams(("arbitrary", "arbitrary")),
        name="w_in_prep",
    )(wt)


def kernel(x_prompt, x_sample, c, cache_k, cache_v, state_gla, state_ssm, c_ctx, w_mod, b_mod, ffn1_norm,
           ffn1_w_gate, ffn1_w_up, ffn1_w_down, mix_norm, w_in, gla_w_up, gla_b_up, gla_norm, ssm_conv_w,
           ssm_conv_b, ssm_dt_bias, ssm_a_log, ssm_d, ssm_norm, attn_sink, w_br_gla, w_br_ssm, w_br_attn,
           w_out, ffn2_norm, ffn2_w_gate, ffn2_w_up, ffn2_w_down, final_norm):
    xs = {CTX: x_prompt.reshape(N_PROMPT, D_MODEL), LAT: x_sample.reshape(N_SAMPLE, D_MODEL)}
    cvec = jnp.concatenate([c_ctx[None], c, jnp.zeros((MOD_ROWS - 1 - DEC_BATCH, D_MODEL), f32)], axis=0)
    mod = _modulation(cvec, w_mod, b_mod)

    w_in_p = _w_in_prep(w_in)
    wbg, wbs, wba, wo = (w.astype(bf16) for w in (w_br_gla, w_br_ssm, w_br_attn, w_out))

    wz = jnp.zeros((DEPTH, 2, SMALL_W, GLA_QK), f32)
    for d in range(2):
        wz = wz.at[:, d, d * GLA_RANK:(d + 1) * GLA_RANK, :].set(gla_w_up[:, d])
    wz = wz.astype(bf16)
    bz = gla_b_up.reshape(DEPTH, 2, 1, GLA_QK)

    def dt_lanes(v):
        vg = v.reshape(DEPTH, 2, SSM_GROUPS, SSM_HPG).transpose(0, 2, 1, 3).reshape(DEPTH, SSM_GROUPS, 2 * SSM_HPG)
        row = jnp.zeros((DEPTH, SSM_GROUPS, SMALL_W), f32)
        row = row.at[:, :, SMALL_DT:SMALL_DT + 2 * SSM_HPG].set(vg)
        return row.reshape(DEPTH, 1, SSM_GROUPS * SMALL_W)

    dtb_rows = dt_lanes(ssm_dt_bias)
    a_rows = dt_lanes(-jnp.exp(ssm_a_log))
    d_skip = jnp.repeat(ssm_d, SSM_HEADDIM, axis=-1).reshape(DEPTH, 1, SSM_INNER)
    conv_w = jnp.concatenate([ssm_conv_w, jnp.zeros((DEPTH, 8 - SSM_CONV, CONV_CH), f32)], axis=1)
    conv_b = ssm_conv_b.reshape(DEPTH, 1, CONV_CH)
    sink8 = jnp.broadcast_to(attn_sink[:, :, None], (DEPTH, ATTN_HEADS, 128))
    cos, sin = _rope_tables()

    n1, nm, n2 = (w.reshape(DEPTH, 1, D_MODEL) for w in (ffn1_norm, mix_norm, ffn2_norm))
    sn = ssm_norm.reshape(DEPTH, 1, SSM_INNER)

    new_k, new_v = [], []
    st_g = st_s = None
    for l in range(DEPTH):
        mod3 = mod[l].reshape(MOD_ROWS, 1, N_MOD * D_MODEL)
        last = l == DEPTH - 1
        for st in (CTX, LAT):
            x = _ffn(st, xs[st], mod3, l, 0, n1, ffn1_w_gate, ffn1_w_up, ffn1_w_down)
            proj, aux = _inproj(st, x, mod3, l, nm, w_in_p)
            if st is CTX:
                o_gla, st_g = _gla(proj, aux, 0, BATCH, SEQ, wz[l], bz[l], gla_norm[l][None], None, l, st_g)
                o_ssm, st_s = _ssd(proj, aux, 0, BATCH, SEQ, conv_w[l], conv_b[l], dtb_rows[l], a_rows[l],
                                   d_skip[l], None, l, st_s)
                o_att = _attn_ctx(proj, sink8[l])
                new_k.append(aux[:, :ATTN_KV].reshape(BATCH, SEQ, KV_HEADS, HEAD_DIM))
                new_v.append(aux[:, ATTN_KV:2 * ATTN_KV].reshape(BATCH, SEQ, KV_HEADS, HEAD_DIM))
            else:
                o_gla, _ = _gla(proj, aux, 0, DEC_BATCH, DEC_SEQ, wz[l], bz[l], gla_norm[l][None], state_gla[:, l])
                o_ssm, _ = _ssd(proj, aux, 0, DEC_BATCH, DEC_SEQ, conv_w[l], conv_b[l], dtb_rows[l], a_rows[l],
                                d_skip[l], state_ssm[:, l])
                o_att = _attn_lat(proj, cache_k, cache_v, l, cos, sin, sink8[l])
            x = _merge(st, x, mod3, l, proj, o_gla, o_ssm, o_att, sn, wbg, wbs, wba, wo)
            xs[st] = _ffn(st, x, mod3, l, 2, n2, ffn2_w_gate, ffn2_w_up, ffn2_w_down,
                          final_w=final_norm if last else None)

    y_prompt = xs[CTX].reshape(BATCH, SEQ, D_MODEL)
    y_sample = xs[LAT].reshape(DEC_BATCH, DEC_SEQ, D_MODEL)
    return (y_prompt, y_sample, jnp.stack(new_k, axis=1), jnp.stack(new_v, axis=1), st_g, st_s)
```

```python
import functools
import math
from typing import NamedTuple

import numpy as np
import jax
import jax.numpy as jnp
from jax import lax
from jax.experimental import pallas as pl
from jax.experimental.pallas import tpu as pltpu

f32 = jnp.float32
bf16 = jnp.bfloat16

D_MODEL = 2048
BATCH = 32
SEQ = 256
DEPTH = 2
DEC_BATCH = 2
DEC_SEQ = 1024
PAST_LEN = 512
GRID_W = 64
RMS_EPS = 1e-6
N_MOD = 9
D_FF = 5632
GLA_HEADS = 4
GLA_DK = 128
GLA_DV = 256
GLA_RANK = 16
GLA_GATE_NORM = 16.0
GLA_CHUNK = 16
SSM_HEADS = 16
SSM_HEADDIM = 64
SSM_GROUPS = 4
SSM_HPG = SSM_HEADS // SSM_GROUPS
SSM_STATE = 128
SSM_CONV = 5
SSM_INNER = SSM_HEADS * SSM_HEADDIM
SSM_BC = SSM_GROUPS * SSM_STATE
CONV_CH = SSM_INNER + 2 * SSM_BC
ATTN_HEADS = 8
KV_HEADS = 2
Q_PER_KV = ATTN_HEADS // KV_HEADS
HEAD_DIM = 128
WINDOW = 128
ROPE_THETA = 10000.0
GLA_QK = GLA_HEADS * GLA_DK
GLA_VAL = GLA_HEADS * GLA_DV
ATTN_Q = ATTN_HEADS * HEAD_DIM
ATTN_KV = KV_HEADS * HEAD_DIM

N_PROMPT = BATCH * SEQ
N_SAMPLE = DEC_BATCH * DEC_SEQ
MOD_ROWS = 8

COL_GQ = 0
COL_GK = COL_GQ + GLA_QK
COL_GV = COL_GK + GLA_QK
COL_GR = COL_GV + GLA_VAL
COL_SZ = COL_GR + GLA_VAL
COL_SX = COL_SZ + SSM_INNER
COL_SB = COL_SX + SSM_INNER
COL_SC = COL_SB + SSM_BC
COL_AQ = COL_SC + SSM_BC
COL_AK = COL_AQ + ATTN_Q
COL_AV = COL_AK + ATTN_KV
COL_SMALL = COL_AV + ATTN_KV
SMALL_W = 128
SMALL_DT = 2 * GLA_RANK
COL_BR = 8192
D_INP = COL_BR + 3 * D_MODEL

TM = 256
TM_BIG = 1024
TF_BIG = 256
TN_IN = 1024
AUX_TILE = COL_AK // TN_IN
AUX_SMALL = (COL_SMALL - COL_AK) // SMALL_W
TN_MOD = 1024
TJ = 512
GLA_SB = 256
GLA_TILE = 128
GLA_HPS = 4
SSD_CHUNK = 128
SSD_GPS = 2
VMEM_LIMIT = 58 * 1024 * 1024

NT = (((1,), (1,)), ((), ()))


def _silu(x):
    return x * jax.nn.sigmoid(x)


class _Stream(NamedTuple):
    n_tok: int
    row0: int
    per_row: int

    def mod_row(self, i, tm):
        return self.row0 + (i * tm) // self.per_row


CTX = _Stream(N_PROMPT, 0, N_PROMPT)
LAT = _Stream(N_SAMPLE, 1, DEC_SEQ)


def _cparams(sem):
    return pltpu.CompilerParams(dimension_semantics=sem, vmem_limit_bytes=VMEM_LIMIT)


def _mod_kernel(c_ref, w_ref, b_ref, o_ref):
    s = _silu(c_ref[...]).astype(bf16)
    o_ref[0] = jnp.dot(s, w_ref[0].astype(bf16), preferred_element_type=f32) + b_ref[0]


def _modulation(cvec, w_mod, b_mod):
    n = N_MOD * D_MODEL
    return pl.pallas_call(
        _mod_kernel,
        grid=(DEPTH, n // TN_MOD),
        in_specs=[
            pl.BlockSpec((MOD_ROWS, D_MODEL), lambda l, j: (0, 0)),
            pl.BlockSpec((1, D_MODEL, TN_MOD), lambda l, j: (l, 0, j)),
            pl.BlockSpec((1, 1, TN_MOD), lambda l, j: (l, 0, j)),
        ],
        out_specs=pl.BlockSpec((1, MOD_ROWS, TN_MOD), lambda l, j: (l, 0, j)),
        out_shape=jax.ShapeDtypeStruct((DEPTH, MOD_ROWS, n), f32),
        compiler_params=_cparams(("arbitrary", "arbitrary")),
        name="modulation",
    )(cvec, w_mod, b_mod.reshape(DEPTH, 1, n))


def _norm_mod(x, nw, sh, sc):
    ms = jnp.mean(x * x, axis=-1, keepdims=True)
    h = x * lax.rsqrt(ms + RMS_EPS) * nw
    return h * (1.0 + sc) + sh


def _ffn_kernel(x_ref, sh_ref, sc_ref, gt_ref, nw_ref, wg_ref, wu_ref, wd_ref, *rest, final):
    if final:
        fw_ref, o_ref, h_ref = rest
    else:
        o_ref, h_ref = rest
    j = pl.program_id(1)

    def hidden_tile():
        h = h_ref[...]
        g = jnp.dot(h, wg_ref[...].astype(bf16), preferred_element_type=f32)
        u = jnp.dot(h, wu_ref[...].astype(bf16), preferred_element_type=f32)
        a = (_silu(g) * u).astype(bf16)
        return jnp.dot(a, wd_ref[...].astype(bf16), preferred_element_type=f32)

    @pl.when(j == 0)
    def _():
        h_ref[...] = _norm_mod(x_ref[...], nw_ref[...], sh_ref[0], sc_ref[0]).astype(bf16)
        o_ref[...] = hidden_tile()

    @pl.when(j > 0)
    def _():
        o_ref[...] += hidden_tile()

    @pl.when(j == pl.num_programs(1) - 1)
    def _():
        y = x_ref[...] + 0.5 * gt_ref[0] * o_ref[...]
        if final:
            ms = jnp.mean(y * y, axis=-1, keepdims=True)
            y = y * lax.rsqrt(ms + RMS_EPS) * fw_ref[...]
        o_ref[...] = y


def _ffn(st, x, mod3, l, slot, nw, wg, wu, wd, final_w=None):
    final = final_w is not None
    tm, tf = TM_BIG, TF_BIG

    def mod_spec(k):
        return pl.BlockSpec((1, 1, D_MODEL), lambda i, j: (st.mod_row(i, tm), 0, 3 * slot + k))

    in_specs = [
        pl.BlockSpec((tm, D_MODEL), lambda i, j: (i, 0)),
        mod_spec(0), mod_spec(1), mod_spec(2),
        pl.BlockSpec((None, 1, D_MODEL), lambda i, j: (l, 0, 0)),
        pl.BlockSpec((None, D_MODEL, tf), lambda i, j: (l, 0, j)),
        pl.BlockSpec((None, D_MODEL, tf), lambda i, j: (l, 0, j)),
        pl.BlockSpec((None, tf, D_MODEL), lambda i, j: (l, j, 0)),
    ]
    args = [x, mod3, mod3, mod3, nw, wg, wu, wd]
    if final:
        in_specs.append(pl.BlockSpec((1, D_MODEL), lambda i, j: (0, 0)))
        args.append(final_w.reshape(1, D_MODEL))
    return pl.pallas_call(
        functools.partial(_ffn_kernel, final=final),
        grid=(st.n_tok // tm, D_FF // tf),
        in_specs=in_specs,
        out_specs=pl.BlockSpec((tm, D_MODEL), lambda i, j: (i, 0)),
        out_shape=jax.ShapeDtypeStruct((st.n_tok, D_MODEL), f32),
        scratch_shapes=[pltpu.VMEM((tm, D_MODEL), bf16)],
        compiler_params=_cparams(("arbitrary", "arbitrary")),
        name="ffn_final" if final else "ffn",
    )(*args)


def _inproj_kernel(x_ref, sh_ref, sc_ref, nw_ref, w_ref, o_ref, os_ref, h_ref):
    j = pl.program_id(1)

    def tile():
        return jnp.dot(h_ref[...], w_ref[...], preferred_element_type=f32)

    @pl.when(j == 0)
    def _():
        h_ref[...] = _norm_mod(x_ref[...], nw_ref[...], sh_ref[0], sc_ref[0]).astype(bf16)
        o_ref[...] = tile().astype(bf16)

    @pl.when((j > 0) & (j < COL_BR // TN_IN) & (j != AUX_TILE))
    def _():
        o_ref[...] = tile().astype(bf16)

    @pl.when(j == AUX_TILE)
    def _():
        t = tile()
        os_ref[...] = t
        o_ref[...] = t.astype(bf16)

    @pl.when(j >= COL_BR // TN_IN)
    def _():
        o_ref[...] = jax.nn.sigmoid(tile()).astype(bf16)


def _inproj(st, x, mod3, l, nw, w):
    tm = TM_BIG

    def mod_spec(k):
        return pl.BlockSpec((1, 1, D_MODEL), lambda i, j: (st.mod_row(i, tm), 0, 3 + k))

    return pl.pallas_call(
        _inproj_kernel,
        grid=(st.n_tok // tm, D_INP // TN_IN),
        in_specs=[
            pl.BlockSpec((tm, D_MODEL), lambda i, j: (i, 0)),
            mod_spec(0), mod_spec(1),
            pl.BlockSpec((None, 1, D_MODEL), lambda i, j: (l, 0, 0)),
            pl.BlockSpec((None, D_MODEL, TN_IN), lambda i, j: (l, 0, j)),
        ],
        out_specs=[pl.BlockSpec((tm, TN_IN), lambda i, j: (i, j)),
                   pl.BlockSpec((tm, TN_IN), lambda i, j: (i, 0))],
        out_shape=[jax.ShapeDtypeStruct((st.n_tok, D_INP), bf16), jax.ShapeDtypeStruct((st.n_tok, TN_IN), f32)],
        scratch_shapes=[pltpu.VMEM((tm, D_MODEL), bf16)],
        compiler_params=_cparams(("arbitrary", "arbitrary")),
        name="inproj",
    )(x, mod3, mod3, nw, w)


def _split_dot(m, x):
    n = x.shape[1]
    hi = x.astype(bf16)
    lo = (x - hi.astype(f32)).astype(bf16)
    r = jnp.dot(m, jnp.concatenate([hi, lo], axis=1), preferred_element_type=f32)
    return r[:, :n] + r[:, n:]


def _log_sigmoid(x):
    return jnp.minimum(x, 0.0) - jnp.log(1.0 + jnp.exp(-jnp.abs(x)))


def _state_slot(st_ref, layer, alias_prev):
    if alias_prev:
        return st_ref.at[0]
    for other in range(DEPTH):
        if other != layer:
            st_ref[0, other] = jnp.zeros(st_ref.shape[2:], st_ref.dtype)
    return st_ref.at[0, layer]


def _gla_kernel(q_ref, k_ref, v_ref, r_ref, sm_ref, wz_ref, bz_ref, gn_ref, *rest, seq, has_h0, emit_state,
                alias_prev, layer):
    rest = list(rest)
    h0_ref = rest.pop(0) if has_h0 else None
    if alias_prev:
        rest.pop(0)
    o_ref = rest.pop(0)
    st_ref = rest.pop(0) if emit_state else None
    oin_ref, qe_ref, ke_ref, dec_ref, vt_ref = rest
    hps = GLA_HPS
    sb = GLA_SB
    tile = GLA_TILE
    per_tile = tile // GLA_CHUNK
    n_tile = seq // tile

    row = lax.broadcasted_iota(jnp.int32, (sb, sb), 0)
    col = lax.broadcasted_iota(jnp.int32, (sb, sb), 1)
    same = (row // GLA_CHUNK) == (col // GLA_CHUNK)
    masks = (same & (col <= row), same & (col >= row))
    blk = jnp.where(same, 1.0, 0.0).astype(bf16)
    tri_blk = tuple(jnp.concatenate([jnp.where(m, 1.0, 0.0).astype(bf16), blk], axis=0) for m in masks)

    chains = [(hh, d) for hh in range(hps) for d in range(2)]
    kcols = [slice(hh * GLA_DK, (hh + 1) * GLA_DK) for hh in range(hps)]
    vcols = [slice(hh * GLA_DV, (hh + 1) * GLA_DV) for hh in range(hps)]

    for s in range(seq // sb):
        rows = slice(s * sb, (s + 1) * sb)
        sm = sm_ref[rows, :].astype(bf16)
        q = [q_ref[rows, kcols[hh]].astype(f32) * (GLA_DK ** -0.5) for hh in range(hps)]
        k = [k_ref[rows, kcols[hh]].astype(f32) for hh in range(hps)]
        v = [v_ref[rows, vcols[hh]] for hh in range(hps)]
        for hh in range(hps):
            v_t = v[hh].astype(f32).T.astype(bf16)
            for t in range(sb // tile):
                vt_ref[hh, s * (sb // tile) + t] = v_t[:, t * tile:(t + 1) * tile]
        gz = [jnp.dot(sm, wz_ref[d, :, kcols[hh]], preferred_element_type=f32) + bz_ref[d, :, kcols[hh]]
              for hh, d in chains]
        la = [_log_sigmoid(g) * (1.0 / GLA_GATE_NORM) for g in gz]
        sums = [_split_dot(tri_blk[d], la[i]) for i, (hh, d) in enumerate(chains)]
        cum = [x[:sb] for x in sums]
        tot = [x[sb:] for x in sums]
        qe = [q[hh] * jnp.exp(cum[i]) for i, (hh, d) in enumerate(chains)]
        kinv = [k[hh] * jnp.exp(-cum[i]) for i, (hh, d) in enumerate(chains)]
        kend = [k[hh] * jnp.exp(tot[i] - cum[i]) for i, (hh, d) in enumerate(chains)]
        a = [lax.dot_general(qe[i].astype(bf16), kinv[i].astype(bf16), NT, preferred_element_type=f32)
             for i in range(len(chains))]
        a = [jnp.where(masks[d], a[i], 0.0).astype(bf16) for i, (hh, d) in enumerate(chains)]
        for i, (hh, d) in enumerate(chains):
            oin_ref[hh, d, rows, :] = jnp.dot(a[i], v[hh], preferred_element_type=f32)
            qe_ref[hh, d, rows, :] = qe[i].astype(bf16)
            ke_ref[hh, d, rows, :] = kend[i].astype(bf16)
            dec_ref[hh, d, rows, :] = jnp.exp(tot[i])

    if has_h0:
        s_init = tuple(h0_ref[d, hh].T for hh, d in chains)
    else:
        s_init = (jnp.zeros((GLA_DV, GLA_DK), f32),) * len(chains)

    def scan_tile(o, carry):
        sts = list(carry)
        for u in range(per_tile):
            for ci, (hh, d) in enumerate(chains):
                t_idx = o if d == 0 else n_tile - 1 - o
                pos = u if d == 0 else per_tile - 1 - u
                i0 = t_idx * tile + pos * GLA_CHUNK
                if not isinstance(i0, int):
                    i0 = pl.multiple_of(i0, GLA_CHUNK)
                rows = pl.ds(i0, GLA_CHUNK)
                inter = lax.dot_general(qe_ref[hh, d, rows, :], sts[ci].astype(bf16), NT,
                                        preferred_element_type=f32)
                oin_ref[hh, d, rows, :] += inter
                pieces = [ke_ref[hh, d, rows, :]]
                if pos:
                    pieces.insert(0, jnp.zeros((pos * GLA_CHUNK, GLA_DK), bf16))
                if pos < per_tile - 1:
                    pieces.append(jnp.zeros(((per_tile - 1 - pos) * GLA_CHUNK, GLA_DK), bf16))
                upd = jnp.dot(vt_ref[hh, t_idx], jnp.concatenate(pieces, axis=0), preferred_element_type=f32)
                sts[ci] = sts[ci] * dec_ref[hh, d, pl.ds(i0, 1), :] + upd
        return tuple(sts)

    if n_tile <= 2:
        s_fin = s_init
        for o in range(n_tile):
            s_fin = scan_tile(o, s_fin)
    else:
        s_fin = lax.fori_loop(0, n_tile, scan_tile, s_init)

    for hh in range(hps):
        vcols = slice(hh * GLA_DV, (hh + 1) * GLA_DV)
        o = oin_ref[hh, 0] + oin_ref[hh, 1]
        ms = jnp.mean(o * o, axis=-1, keepdims=True)
        o = o * lax.rsqrt(ms + RMS_EPS) * gn_ref[...]
        o_ref[:, vcols] = (o * _silu(r_ref[:, vcols].astype(f32))).astype(o_ref.dtype)
    if emit_state:
        slot = _state_slot(st_ref, layer, alias_prev)
        for ci, (hh, d) in enumerate(chains):
            slot[d, hh] = s_fin[ci].T


def _gla(proj, small, row_blk0, n_seq, seq, wz, bz, gn, h0, layer=0, prev_state=None):
    has_h0 = h0 is not None
    emit_state = not has_h0
    aliases = {}
    kq, kv_, hps = GLA_DK, GLA_DV, GLA_HPS
    kw, vw = hps * kq, hps * kv_
    in_specs = [
        pl.BlockSpec((seq, kw), lambda b, h: (row_blk0 + b, COL_GQ // kw + h)),
        pl.BlockSpec((seq, kw), lambda b, h: (row_blk0 + b, COL_GK // kw + h)),
        pl.BlockSpec((seq, vw), lambda b, h: (row_blk0 + b, COL_GV // vw + h)),
        pl.BlockSpec((seq, vw), lambda b, h: (row_blk0 + b, COL_GR // vw + h)),
        pl.BlockSpec((seq, SMALL_W), lambda b, h: (row_blk0 + b, AUX_SMALL)),
        pl.BlockSpec((2, SMALL_W, kw), lambda b, h: (0, 0, h)),
        pl.BlockSpec((2, 1, kw), lambda b, h: (0, 0, h)),
        pl.BlockSpec((1, kv_), lambda b, h: (0, 0)),
    ]
    args = [proj, proj, proj, proj, small, wz, bz, gn]
    if has_h0:
        in_specs.append(pl.BlockSpec((None, 2, hps, kq, kv_), lambda b, h: (b, 0, h, 0, 0)))
        args.append(h0)
    out_specs = [pl.BlockSpec((seq, vw), lambda b, h: (b, h))]
    out_shape = [jax.ShapeDtypeStruct((n_seq * seq, GLA_VAL), bf16)]
    if emit_state:
        out_shape.append(jax.ShapeDtypeStruct((n_seq, DEPTH, 2, GLA_HEADS, kq, kv_), f32))
        if prev_state is None:
            out_specs.append(pl.BlockSpec((1, DEPTH, 2, hps, kq, kv_), lambda b, h: (b, 0, 0, h, 0, 0)))
        else:
            out_specs.append(pl.BlockSpec((1, None, 2, hps, kq, kv_), lambda b, h: (b, layer, 0, h, 0, 0)))
            in_specs.append(pl.BlockSpec(memory_space=pl.ANY))
            args.append(prev_state)
            aliases = {len(args) - 1: 1}
    res = pl.pallas_call(
        functools.partial(_gla_kernel, seq=seq, has_h0=has_h0, emit_state=emit_state, alias_prev=bool(aliases),
                          layer=layer),
        input_output_aliases=aliases,
        grid=(n_seq, GLA_HEADS // hps),
        in_specs=in_specs,
        out_specs=out_specs,
        out_shape=out_shape,
        scratch_shapes=[
            pltpu.VMEM((hps, 2, seq, kv_), f32),
            pltpu.VMEM((hps, 2, seq, kq), bf16),
            pltpu.VMEM((hps, 2, seq, kq), bf16),
            pltpu.VMEM((hps, 2, seq, kq), f32),
            pltpu.VMEM((hps, seq // GLA_TILE, kv_, GLA_TILE), bf16),
        ],
        compiler_params=_cparams(("parallel", "arbitrary")),
        name="gla_lat" if has_h0 else "gla_ctx",
    )(*args)
    return res if emit_state else (res[0], None)


def _softplus(x):
    return jnp.maximum(x, 0.0) + jnp.log(1.0 + jnp.exp(-jnp.abs(x)))


def _split3(x):
    hi = x.astype(bf16)
    r1 = x - hi.astype(f32)
    mid = r1.astype(bf16)
    lo = (r1 - mid.astype(f32)).astype(bf16)
    return [hi, mid, lo]


def _split3_dot(m, x):
    n = x.shape[1]
    r = jnp.dot(m, jnp.concatenate(_split3(x), axis=1), preferred_element_type=f32)
    return r[:, :n] + r[:, n:2 * n] + r[:, 2 * n:]


def _dot_sel(x, sel):
    return sum(jnp.dot(t, sel, preferred_element_type=f32) for t in _split3(x))


def _shift_rows(x, d, t_idx):
    n = x.shape[0]
    if d == 0:
        return x
    y = pltpu.roll(x, (-d) % n, 0)
    ok = (t_idx + d >= 0) & (t_idx + d < n)
    return jnp.where(ok, y, 0.0)


def _ssd_kernel(z_ref, x_ref, b_ref, c_ref, sm_ref, cwx_ref, cwb_ref, cwc_ref, cbx_ref, cbb_ref, cbc_ref,
                dtb_ref, a_ref, dsk_ref, *rest, seq, has_h0, emit_state, alias_prev, layer):
    rest = list(rest)
    h0_ref = rest.pop(0) if has_h0 else None
    if alias_prev:
        rest.pop(0)
    o_ref = rest.pop(0)
    st_ref = rest.pop(0) if emit_state else None
    y_ref, xs_ref, bm_ref, cm_ref, dt_ref = rest
    gps = SSD_GPS
    cs = SSD_CHUNK
    n_chunk = seq // cs
    hp = SSM_HPG * SSM_HEADDIM

    t_idx = lax.broadcasted_iota(jnp.int32, (seq, 1), 0)

    def conv_silu(src_ref, w_ref, bias_ref):
        xin = src_ref[...].astype(f32)
        acc = jnp.zeros_like(xin) + bias_ref[...]
        for j in range(SSM_CONV):
            acc = acc + w_ref[j:j + 1, :] * _shift_rows(xin, j - SSM_CONV // 2, t_idx)
        return _silu(acc)

    xs_ref[...] = conv_silu(x_ref, cwx_ref, cbx_ref)
    bm_ref[...] = conv_silu(b_ref, cwb_ref, cbb_ref)
    cm_ref[...] = conv_silu(c_ref, cwc_ref, cbc_ref)

    row = lax.broadcasted_iota(jnp.int32, (cs, cs), 0)
    col = lax.broadcasted_iota(jnp.int32, (cs, cs), 1)
    masks = (col <= row, col >= row)
    tris = tuple(jnp.where(m, 1.0, 0.0).astype(bf16) for m in masks)
    lane = lax.broadcasted_iota(jnp.int32, (1, gps * SMALL_W), 1) % SMALL_W

    dt_lane = (lane >= SMALL_DT) & (lane < SMALL_DT + 2 * SSM_HPG)
    a_row = a_ref[...]

    def selectors(d):
        r = lax.broadcasted_iota(jnp.int32, (SMALL_W, SSM_HPG * cs), 0)
        c = lax.broadcasted_iota(jnp.int32, (SMALL_W, SSM_HPG * cs), 1)
        bcast = jnp.where(r == SMALL_DT + d * SSM_HPG + c // cs, 1.0, 0.0).astype(bf16)
        r = lax.broadcasted_iota(jnp.int32, (SMALL_W, hp), 0)
        c = lax.broadcasted_iota(jnp.int32, (SMALL_W, hp), 1)
        expand = jnp.where(r == SMALL_DT + d * SSM_HPG + c // SSM_HEADDIM, 1.0, 0.0).astype(bf16)
        return jnp.concatenate([bcast, expand], axis=1)

    head_of_lane = lax.broadcasted_iota(jnp.int32, (cs, hp), 1) // SSM_HEADDIM
    sl = [slice(g * SMALL_W, (g + 1) * SMALL_W) for g in range(gps)]
    hl = [slice(g * hp, (g + 1) * hp) for g in range(gps)]
    bm_t = [[bm_ref[c * cs:(c + 1) * cs, sl[g]].T.astype(bf16) for c in range(n_chunk)]
            for g in range(gps)]

    dt_ref[...] = jnp.where(dt_lane, _softplus(sm_ref[...] + dtb_ref[...]), 0.0)
    y_ref[...] = jnp.zeros_like(y_ref)
    sel = [selectors(0), selectors(1)]
    chains = [(g, d) for g in range(gps) for d in range(2)]
    n = range(len(chains))
    if has_h0:
        st = [h0_ref[d, g * SSM_HPG:(g + 1) * SSM_HPG].reshape(hp, SSM_STATE).T for g, d in chains]
    else:
        st = [jnp.zeros((SSM_STATE, hp), f32)] * len(chains)
    for t in range(n_chunk):
        cidx = [t if d == 0 else n_chunk - 1 - t for g, d in chains]
        rows = [slice(c * cs, (c + 1) * cs) for c in cidx]
        dt = [dt_ref[rows[i], sl[g]] for i, (g, d) in enumerate(chains)]
        cum = [_split3_dot(tris[d], dt[i] * a_row[:, sl[g]]) for i, (g, d) in enumerate(chains)]
        cum_t = [cum[i].T for i in n]
        cum_s = [_dot_sel(cum[i], sel[d]) for i, (g, d) in enumerate(chains)]
        cum_b = [x[:, :SSM_HPG * cs] for x in cum_s]
        cum_e = [x[:, SSM_HPG * cs:] for x in cum_s]
        dt_e = [_dot_sel(dt[i], sel[d][:, SSM_HPG * cs:]) for i, (g, d) in enumerate(chains)]
        tot_e = [cum_e[i][cs - 1:cs, :] if d == 0 else cum_e[i][0:1, :] for i, (g, d) in enumerate(chains)]
        xs = [xs_ref[rows[i], hl[g]] for i, (g, d) in enumerate(chains)]
        cm = [cm_ref[rows[i], sl[g]].astype(bf16) for i, (g, d) in enumerate(chains)]
        cb = [lax.dot_general(cm[i], bm_ref[rows[i], sl[g]].astype(bf16), NT, preferred_element_type=f32)
              for i, (g, d) in enumerate(chains)]
        seg = [[jnp.exp(jnp.where(masks[d], cum_b[i][:, j * cs:(j + 1) * cs]
                                  - cum_t[i][SMALL_DT + d * SSM_HPG + j:SMALL_DT + d * SSM_HPG + j + 1, :],
                                  -jnp.inf)) for j in range(SSM_HPG)] for i, (g, d) in enumerate(chains)]
        w = [jnp.concatenate([(cb[i] * seg[i][j]).astype(bf16) for j in range(SSM_HPG)], axis=1)
             for i in n]
        xd = [xs[i] * dt_e[i] for i in n]
        xd_bd = [jnp.concatenate([jnp.where(head_of_lane == j, xd[i], 0.0).astype(bf16)
                                  for j in range(SSM_HPG)], axis=0) for i in n]
        inter = [jnp.dot(cm[i], st[i].astype(bf16), preferred_element_type=f32) for i in n]
        y_new = [jnp.dot(w[i], xd_bd[i], preferred_element_type=f32) + jnp.exp(cum_e[i]) * inter[i] for i in n]
        xw = [(xd[i] * jnp.exp(tot_e[i] - cum_e[i])).astype(bf16) for i in n]
        st = [st[i] * jnp.exp(tot_e[i]) + jnp.dot(bm_t[g][cidx[i]], xw[i], preferred_element_type=f32)
              for i, (g, d) in enumerate(chains)]
        for i, (g, d) in enumerate(chains):
            y_ref[rows[i], hl[g]] += y_new[i]

    y = y_ref[...] + dsk_ref[...] * xs_ref[...]
    o_ref[...] = y * _silu(z_ref[...].astype(f32))
    if emit_state:
        slot = _state_slot(st_ref, layer, alias_prev)
        for i, (g, d) in enumerate(chains):
            slot[d, g * SSM_HPG:(g + 1) * SSM_HPG] = st[i].T.reshape(SSM_HPG, SSM_HEADDIM, SSM_STATE)


def _ssd(proj, small, row_blk0, n_seq, seq, conv_w, conv_b, dt_bias_row, a_row, d_skip, h0, layer=0,
         prev_state=None):
    has_h0 = h0 is not None
    emit_state = not has_h0
    aliases = {}
    gps = SSD_GPS
    hw = gps * SSM_HPG * SSM_HEADDIM
    nw = gps * SSM_STATE
    hg = gps * SSM_HPG
    in_specs = [
        pl.BlockSpec((seq, hw), lambda b, g: (row_blk0 + b, COL_SZ // hw + g)),
        pl.BlockSpec((seq, hw), lambda b, g: (row_blk0 + b, COL_SX // hw + g)),
        pl.BlockSpec((seq, nw), lambda b, g: (row_blk0 + b, COL_SB // nw + g)),
        pl.BlockSpec((seq, nw), lambda b, g: (row_blk0 + b, COL_SC // nw + g)),
        pl.BlockSpec((seq, nw), lambda b, g: (row_blk0 + b, AUX_SMALL // gps + g)),
        pl.BlockSpec((8, hw), lambda b, g: (0, g)),
        pl.BlockSpec((8, nw), lambda b, g: (0, SSM_INNER // nw + g)),
        pl.BlockSpec((8, nw), lambda b, g: (0, (SSM_INNER + SSM_BC) // nw + g)),
        pl.BlockSpec((1, hw), lambda b, g: (0, g)),
        pl.BlockSpec((1, nw), lambda b, g: (0, SSM_INNER // nw + g)),
        pl.BlockSpec((1, nw), lambda b, g: (0, (SSM_INNER + SSM_BC) // nw + g)),
        pl.BlockSpec((1, nw), lambda b, g: (0, g)),
        pl.BlockSpec((1, nw), lambda b, g: (0, g)),
        pl.BlockSpec((1, hw), lambda b, g: (0, g)),
    ]
    args = [proj, proj, proj, proj, small, conv_w, conv_w, conv_w, conv_b, conv_b, conv_b,
            dt_bias_row, a_row, d_skip]
    if has_h0:
        in_specs.append(pl.BlockSpec((None, 2, hg, SSM_HEADDIM, SSM_STATE), lambda b, g: (b, 0, g, 0, 0)))
        args.append(h0)
    out_specs = [pl.BlockSpec((seq, hw), lambda b, g: (b, g))]
    out_shape = [jax.ShapeDtypeStruct((n_seq * seq, SSM_INNER), f32)]
    if emit_state:
        out_shape.append(jax.ShapeDtypeStruct((n_seq, DEPTH, 2, SSM_HEADS, SSM_HEADDIM, SSM_STATE), f32))
        if prev_state is None:
            out_specs.append(pl.BlockSpec((1, DEPTH, 2, hg, SSM_HEADDIM, SSM_STATE), lambda b, g: (b, 0, 0, g, 0, 0)))
        else:
            out_specs.append(pl.BlockSpec((1, None, 2, hg, SSM_HEADDIM, SSM_STATE),
                                          lambda b, g: (b, layer, 0, g, 0, 0)))
            in_specs.append(pl.BlockSpec(memory_space=pl.ANY))
            args.append(prev_state)
            aliases = {len(args) - 1: 1}
    res = pl.pallas_call(
        functools.partial(_ssd_kernel, seq=seq, has_h0=has_h0, emit_state=emit_state, alias_prev=bool(aliases),
                          layer=layer),
        input_output_aliases=aliases,
        grid=(n_seq, SSM_GROUPS // gps),
        in_specs=in_specs,
        out_specs=out_specs,
        out_shape=out_shape,
        scratch_shapes=[
            pltpu.VMEM((seq, hw), f32),
            pltpu.VMEM((seq, hw), f32),
            pltpu.VMEM((seq, nw), f32),
            pltpu.VMEM((seq, nw), f32),
            pltpu.VMEM((seq, nw), f32),
        ],
        compiler_params=_cparams(("parallel", "arbitrary")),
        name="ssd_lat" if has_h0 else "ssd_ctx",
    )(*args)
    return res if emit_state else (res[0], None)


def _attn_ctx_kernel(q_ref, k_ref, v_ref, sink_ref, o_ref):
    heads = range(ATTN_HEADS)
    kv = [slice(h // Q_PER_KV * HEAD_DIM, (h // Q_PER_KV + 1) * HEAD_DIM) for h in heads]
    cols = [slice(h * HEAD_DIM, (h + 1) * HEAD_DIM) for h in heads]
    s = [lax.dot_general(q_ref[:, cols[h]], k_ref[:, kv[h]], NT, preferred_element_type=f32) * (HEAD_DIM ** -0.5)
         for h in heads]
    sink = [sink_ref[h:h + 1, 0:1] for h in heads]
    m = [jnp.maximum(jnp.max(s[h], axis=-1, keepdims=True), sink[h]) for h in heads]
    p = [jnp.exp(s[h] - m[h]) for h in heads]
    den = [jnp.sum(p[h], axis=-1, keepdims=True) + jnp.exp(sink[h] - m[h]) for h in heads]
    for h in heads:
        o = jnp.dot(p[h].astype(bf16), v_ref[:, kv[h]], preferred_element_type=f32) / den[h]
        o_ref[:, cols[h]] = o.astype(o_ref.dtype)


def _attn_ctx(proj, sink8, n_seq=BATCH):
    return pl.pallas_call(
        _attn_ctx_kernel,
        grid=(n_seq,),
        in_specs=[
            pl.BlockSpec((SEQ, ATTN_Q), lambda b: (b, COL_AQ // ATTN_Q)),
            pl.BlockSpec((SEQ, ATTN_KV), lambda b: (b, COL_AK // ATTN_KV)),
            pl.BlockSpec((SEQ, ATTN_KV), lambda b: (b, COL_AV // ATTN_KV)),
            pl.BlockSpec((ATTN_HEADS, 128), lambda b: (0, 0)),
        ],
        out_specs=pl.BlockSpec((SEQ, ATTN_Q), lambda b: (b, 0)),
        out_shape=jax.ShapeDtypeStruct((n_seq * SEQ, ATTN_Q), bf16),
        compiler_params=_cparams(("arbitrary",)),
        name="attn_ctx",
    )(proj, proj, proj, sink8)


def _rope(x, cos, sin_signed):
    quarter = HEAD_DIM // 4
    lane = lax.broadcasted_iota(jnp.int32, x.shape, 1)
    first = (lane % (2 * quarter)) < quarter
    partner = jnp.where(first, pltpu.roll(x, HEAD_DIM - quarter, 1), pltpu.roll(x, quarter, 1))
    return x * cos + partner * sin_signed


def _attn_lat_kernel(q_ref, k_ref, v_ref, kc_ref, vc_ref, cos_ref, sin_ref, sink_ref, o_ref, kr_ref):
    kvh = pl.program_id(1)
    cos = cos_ref[...]
    sin = sin_ref[...]
    kr_ref[...] = _rope(k_ref[...].astype(f32), cos, sin).astype(bf16)
    kc = kc_ref[...].astype(bf16)
    vc = vc_ref[...].astype(bf16)
    blk = WINDOW
    n_blk = DEC_SEQ // blk
    scale = HEAD_DIM ** -0.5
    for i in range(n_blk):
        lo = max(i - 1, 0) * blk
        hi = min(i + 2, n_blk) * blk
        kw = kr_ref[lo:hi, :]
        vw = v_ref[lo:hi, :].astype(bf16)
        qpos = i * blk + lax.broadcasted_iota(jnp.int32, (blk, hi - lo), 0)
        kpos = lo + lax.broadcasted_iota(jnp.int32, (blk, hi - lo), 1)
        win = jnp.abs(qpos - kpos) <= WINDOW
        rows = slice(i * blk, (i + 1) * blk)
        grp = range(Q_PER_KV)
        cols = [slice(g * HEAD_DIM, (g + 1) * HEAD_DIM) for g in grp]
        q = [_rope(q_ref[rows, cols[g]].astype(f32), cos[rows, :], sin[rows, :]).astype(bf16) for g in grp]
        s_c = [lax.dot_general(q[g], kc, NT, preferred_element_type=f32) * scale for g in grp]
        s_w = [jnp.where(win, lax.dot_general(q[g], kw, NT, preferred_element_type=f32) * scale, -jnp.inf)
               for g in grp]
        sink = [sink_ref[pl.ds(kvh * Q_PER_KV + g, 1), 0:1] for g in grp]
        m = [jnp.maximum(jnp.maximum(jnp.max(s_c[g], axis=-1, keepdims=True),
                                     jnp.max(s_w[g], axis=-1, keepdims=True)), sink[g]) for g in grp]
        p_c = [jnp.exp(s_c[g] - m[g]) for g in grp]
        p_w = [jnp.exp(s_w[g] - m[g]) for g in grp]
        den = [jnp.sum(p_c[g], axis=-1, keepdims=True) + jnp.sum(p_w[g], axis=-1, keepdims=True)
               + jnp.exp(sink[g] - m[g]) for g in grp]
        for g in grp:
            o = (jnp.dot(p_c[g].astype(bf16), vc, preferred_element_type=f32)
                 + jnp.dot(p_w[g].astype(bf16), vw, preferred_element_type=f32)) / den[g]
            o_ref[rows, cols[g]] = o.astype(o_ref.dtype)


def _attn_lat(proj, cache_k, cache_v, l, cos, sin, sink8, rb0=0, n_seq=DEC_BATCH):
    qw = Q_PER_KV * HEAD_DIM
    ck = cache_k.reshape(n_seq, DEPTH, PAST_LEN, ATTN_KV)
    cv = cache_v.reshape(n_seq, DEPTH, PAST_LEN, ATTN_KV)
    return pl.pallas_call(
        _attn_lat_kernel,
        grid=(n_seq, KV_HEADS),
        in_specs=[
            pl.BlockSpec((DEC_SEQ, qw), lambda b, h: (rb0 + b, COL_AQ // qw + h)),
            pl.BlockSpec((DEC_SEQ, HEAD_DIM), lambda b, h: (rb0 + b, COL_AK // HEAD_DIM + h)),
            pl.BlockSpec((DEC_SEQ, HEAD_DIM), lambda b, h: (rb0 + b, COL_AV // HEAD_DIM + h)),
            pl.BlockSpec((None, None, PAST_LEN, HEAD_DIM), lambda b, h: (b, l, 0, h)),
            pl.BlockSpec((None, None, PAST_LEN, HEAD_DIM), lambda b, h: (b, l, 0, h)),
            pl.BlockSpec((DEC_SEQ, HEAD_DIM), lambda b, h: (0, 0)),
            pl.BlockSpec((DEC_SEQ, HEAD_DIM), lambda b, h: (0, 0)),
            pl.BlockSpec((ATTN_HEADS, 128), lambda b, h: (0, 0)),
        ],
        out_specs=pl.BlockSpec((DEC_SEQ, qw), lambda b, h: (b, h)),
        out_shape=jax.ShapeDtypeStruct((n_seq * DEC_SEQ, ATTN_Q), bf16),
        scratch_shapes=[pltpu.VMEM((DEC_SEQ, HEAD_DIM), bf16)],
        compiler_params=_cparams(("parallel", "arbitrary")),
        name="attn_lat",
    )(proj, proj, proj, ck, cv, cos, sin, sink8)


def _rope_tables():
    quarter = HEAD_DIM // 4
    freqs = ROPE_THETA ** (-np.arange(quarter, dtype=np.float32) / quarter)
    t = np.arange(DEC_SEQ)
    cos = np.zeros((DEC_SEQ, HEAD_DIM), np.float32)
    sin = np.zeros((DEC_SEQ, HEAD_DIM), np.float32)
    for half, pos in enumerate((t // GRID_W, t % GRID_W)):
        ang = pos.astype(np.float32)[:, None] * freqs[None, :]
        base = half * 2 * quarter
        cos[:, base:base + quarter] = np.cos(ang)
        cos[:, base + quarter:base + 2 * quarter] = np.cos(ang)
        sin[:, base:base + quarter] = -np.sin(ang)
        sin[:, base + quarter:base + 2 * quarter] = np.sin(ang)
    return jnp.asarray(cos), jnp.asarray(sin)


def _merge_kernel(x_ref, gt_ref, og_ref, os_ref, oa_ref, b0_ref, b1_ref, b2_ref, sn_ref,
                  wg_ref, ws_ref, wa_ref, wo_ref, o_ref):
    y = os_ref[...]
    ms = jnp.mean(y * y, axis=-1, keepdims=True)
    osn = (y * lax.rsqrt(ms + RMS_EPS) * sn_ref[...]).astype(bf16)
    og = og_ref[...]
    oa = oa_ref[...]
    for jt in range(D_MODEL // TJ):
        cols = slice(jt * TJ, (jt + 1) * TJ)
        m = (b0_ref[:, cols].astype(f32) * jnp.dot(og, wg_ref[:, cols], preferred_element_type=f32)
             + b1_ref[:, cols].astype(f32) * jnp.dot(osn, ws_ref[:, cols], preferred_element_type=f32)
             + b2_ref[:, cols].astype(f32) * jnp.dot(oa, wa_ref[:, cols], preferred_element_type=f32))
        part = jnp.dot(m.astype(bf16), wo_ref[cols, :], preferred_element_type=f32)
        if jt == 0:
            o_ref[...] = part
        else:
            o_ref[...] += part
    o_ref[...] = x_ref[...] + gt_ref[0] * o_ref[...]


def _merge(st, x, mod3, l, proj, o_gla, o_ssm, o_att, ssm_norm, w_g, w_s, w_a, w_o):
    tm = TM

    def br_spec(k):
        return pl.BlockSpec((tm, D_MODEL), lambda i: (i, COL_BR // D_MODEL + k))

    def wbr_spec():
        return pl.BlockSpec((None, GLA_VAL, D_MODEL), lambda i: (l, 0, 0), pipeline_mode=pl.Buffered(1))

    return pl.pallas_call(
        _merge_kernel,
        grid=(st.n_tok // tm,),
        in_specs=[
            pl.BlockSpec((tm, D_MODEL), lambda i: (i, 0)),
            pl.BlockSpec((1, 1, D_MODEL), lambda i: (st.mod_row(i, tm), 0, 5)),
            pl.BlockSpec((tm, GLA_VAL), lambda i: (i, 0)),
            pl.BlockSpec((tm, SSM_INNER), lambda i: (i, 0)),
            pl.BlockSpec((tm, ATTN_Q), lambda i: (i, 0)),
            br_spec(0), br_spec(1), br_spec(2),
            pl.BlockSpec((None, 1, SSM_INNER), lambda i: (l, 0, 0)),
            wbr_spec(), wbr_spec(), wbr_spec(),
            pl.BlockSpec((None, D_MODEL, D_MODEL), lambda i: (l, 0, 0), pipeline_mode=pl.Buffered(1)),
        ],
        out_specs=pl.BlockSpec((tm, D_MODEL), lambda i: (i, 0)),
        out_shape=jax.ShapeDtypeStruct((st.n_tok, D_MODEL), f32),
        compiler_params=_cparams(("arbitrary",)),
        name="merge",
    )(x, mod3, o_gla, o_ssm, o_att, proj, proj, proj, ssm_norm, w_g, w_s, w_a, w_o)


_IN_SRC = np.cumsum((0, GLA_QK, GLA_QK, GLA_VAL, GLA_VAL, 2 * GLA_RANK, SSM_INNER, CONV_CH, 2 * SSM_HEADS,
                     ATTN_Q, ATTN_KV, ATTN_KV, 3 * D_MODEL))
SRC_GDOWN, SRC_SZ, SRC_DT, SRC_AQ, SRC_BR, SRC_END = (int(_IN_SRC[k]) for k in (4, 5, 7, 8, 11, 12))
W_PREP_K = 256
W_PREP_CHUNK = 512


def _w_in_prep_kernel(wt_ref, o_ref):
    def move(src, dst, width):
        for c0 in range(0, width, W_PREP_CHUNK):
            w = min(W_PREP_CHUNK, width - c0)
            o_ref[0, :, dst + c0:dst + c0 + w] = wt_ref[0, src + c0:src + c0 + w, :].T.astype(bf16)

    move(0, COL_GQ, SRC_GDOWN)
    move(SRC_SZ, COL_SZ, SRC_DT - SRC_SZ)
    move(SRC_AQ, COL_AQ, SRC_BR - SRC_AQ)
    move(SRC_BR, COL_BR, SRC_END - SRC_BR)

    n_gd, n_dt = 2 * GLA_RANK, 2 * SSM_HEADS
    t_gd = wt_ref[0, SRC_GDOWN:SRC_GDOWN + n_gd, :].astype(bf16)
    t_dt = wt_ref[0, SRC_DT:SRC_DT + n_dt, :].astype(bf16)
    r = lax.broadcasted_iota(jnp.int32, (SMALL_W, n_gd), 0)
    c = lax.broadcasted_iota(jnp.int32, (SMALL_W, n_gd), 1)
    sel_gd = jnp.where(r == c, 1.0, 0.0).astype(bf16)
    k = r - SMALL_DT
    for g in range(SSM_GROUPS):
        src_row = (k // SSM_HPG) * SSM_HEADS + g * SSM_HPG + k % SSM_HPG
        sel_dt = jnp.where((k >= 0) & (k < 2 * SSM_HPG) & (c == src_row), 1.0, 0.0).astype(bf16)
        blk = (jnp.dot(sel_gd, t_gd, preferred_element_type=f32)
               + jnp.dot(sel_dt, t_dt, preferred_element_type=f32))
        o_ref[0, :, COL_SMALL + g * SMALL_W:COL_SMALL + (g + 1) * SMALL_W] = blk.T.astype(bf16)


def _w_in_prep(w_in):
    wt = jnp.swapaxes(w_in, 1, 2)
    return pl.pallas_call(
        _w_in_prep_kernel,
        grid=(DEPTH, D_MODEL // W_PREP_K),
        in_specs=[pl.BlockSpec((1, SRC_END, W_PREP_K), lambda l, i: (l, 0, i))],
        out_specs=pl.BlockSpec((1, W_PREP_K, D_INP), lambda l, i: (l, i, 0)),
        out_shape=jax.ShapeDtypeStruct((DEPTH, D_MODEL, D_INP), bf16),
        compiler_params=_cparams(("arbitrary", "arbitrary")),
        name="w_in_prep",
    )(wt)


def kernel(x_prompt, x_sample, c, cache_k, cache_v, state_gla, state_ssm, c_ctx, w_mod, b_mod, ffn1_norm,
           ffn1_w_gate, ffn1_w_up, ffn1_w_down, mix_norm, w_in, gla_w_up, gla_b_up, gla_norm, ssm_conv_w,
           ssm_conv_b, ssm_dt_bias, ssm_a_log, ssm_d, ssm_norm, attn_sink, w_br_gla, w_br_ssm, w_br_attn,
           w_out, ffn2_norm, ffn2_w_gate, ffn2_w_up, ffn2_w_down, final_norm):
    xs = {CTX: x_prompt.reshape(N_PROMPT, D_MODEL), LAT: x_sample.reshape(N_SAMPLE, D_MODEL)}
    cvec = jnp.concatenate([c_ctx[None], c, jnp.zeros((MOD_ROWS - 1 - DEC_BATCH, D_MODEL), f32)], axis=0)
    mod = _modulation(cvec, w_mod, b_mod)

    w_in_p = _w_in_prep(w_in)
    wbg, wbs, wba, wo = (w.astype(bf16) for w in (w_br_gla, w_br_ssm, w_br_attn, w_out))

    wz = jnp.zeros((DEPTH, 2, SMALL_W, GLA_QK), f32)
    for d in range(2):
        wz = wz.at[:, d, d * GLA_RANK:(d + 1) * GLA_RANK, :].set(gla_w_up[:, d])
    wz = wz.astype(bf16)
    bz = gla_b_up.reshape(DEPTH, 2, 1, GLA_QK)

    def dt_lanes(v):
        vg = v.reshape(DEPTH, 2, SSM_GROUPS, SSM_HPG).transpose(0, 2, 1, 3).reshape(DEPTH, SSM_GROUPS, 2 * SSM_HPG)
        row = jnp.zeros((DEPTH, SSM_GROUPS, SMALL_W), f32)
        row = row.at[:, :, SMALL_DT:SMALL_DT + 2 * SSM_HPG].set(vg)
        return row.reshape(DEPTH, 1, SSM_GROUPS * SMALL_W)

    dtb_rows = dt_lanes(ssm_dt_bias)
    a_rows = dt_lanes(-jnp.exp(ssm_a_log))
    d_skip = jnp.repeat(ssm_d, SSM_HEADDIM, axis=-1).reshape(DEPTH, 1, SSM_INNER)
    conv_w = jnp.concatenate([ssm_conv_w, jnp.zeros((DEPTH, 8 - SSM_CONV, CONV_CH), f32)], axis=1)
    conv_b = ssm_conv_b.reshape(DEPTH, 1, CONV_CH)
    sink8 = jnp.broadcast_to(attn_sink[:, :, None], (DEPTH, ATTN_HEADS, 128))
    cos, sin = _rope_tables()

    n1, nm, n2 = (w.reshape(DEPTH, 1, D_MODEL) for w in (ffn1_norm, mix_norm, ffn2_norm))
    sn = ssm_norm.reshape(DEPTH, 1, SSM_INNER)

    new_k, new_v = [], []
    st_g = st_s = None
    for l in range(DEPTH):
        mod3 = mod[l].reshape(MOD_ROWS, 1, N_MOD * D_MODEL)
        last = l == DEPTH - 1
        for st in (CTX, LAT):
            x = _ffn(st, xs[st], mod3, l, 0, n1, ffn1_w_gate, ffn1_w_up, ffn1_w_down)
            proj, aux = _inproj(st, x, mod3, l, nm, w_in_p)
            if st is CTX:
                o_gla, st_g = _gla(proj, aux, 0, BATCH, SEQ, wz[l], bz[l], gla_norm[l][None], None, l, st_g)
                o_ssm, st_s = _ssd(proj, aux, 0, BATCH, SEQ, conv_w[l], conv_b[l], dtb_rows[l], a_rows[l],
                                   d_skip[l], None, l, st_s)
                o_att = _attn_ctx(proj, sink8[l])
                new_k.append(aux[:, :ATTN_KV].reshape(BATCH, SEQ, KV_HEADS, HEAD_DIM))
                new_v.append(aux[:, ATTN_KV:2 * ATTN_KV].reshape(BATCH, SEQ, KV_HEADS, HEAD_DIM))
            else:
                o_gla, _ = _gla(proj, aux, 0, DEC_BATCH, DEC_SEQ, wz[l], bz[l], gla_norm[l][None], state_gla[:, l])
                o_ssm, _ = _ssd(proj, aux, 0, DEC_BATCH, DEC_SEQ, conv_w[l], conv_b[l], dtb_rows[l], a_rows[l],
                                d_skip[l], state_ssm[:, l])
                o_att = _attn_lat(proj, cache_k, cache_v, l, cos, sin, sink8[l])
            x = _merge(st, x, mod3, l, proj, o_gla, o_ssm, o_att, sn, wbg, wbs, wba, wo)
            xs[st] = _ffn(st, x, mod3, l, 2, n2, ffn2_w_gate, ffn2_w_up, ffn2_w_down,
                          final_w=final_norm if last else None)

    y_prompt = xs[CTX].reshape(BATCH, SEQ, D_MODEL)
    y_sample = xs[LAT].reshape(DEC_BATCH, DEC_SEQ, D_MODEL)
    return (y_prompt, y_sample, jnp.stack(new_k, axis=1), jnp.stack(new_v, axis=1), st_g, st_s)
```
